```python
import math, functools
import jax, jax.numpy as jnp
from jax import lax
import numpy as np

D_MODEL = 1024
BATCH = 8
SEQ = 4096
DEPTH = 1
DEC_BATCH = 16
DEC_SEQ = 16
PAST_LEN = 2048

CHUNK = 64
N_Q_HEADS = 16
N_KV_HEADS = 2
HEAD_DIM = 64
Q_PER_KV = N_Q_HEADS // N_KV_HEADS
ATTN_WIDTH = N_Q_HEADS * HEAD_DIM
KV_WIDTH = N_KV_HEADS * HEAD_DIM
WINDOW = 128
WINDOW_CHUNKS = WINDOW // CHUNK
GMLP_WIDTH = 1024
GMLP_GROUPS = 4
GMLP_GROUP_DIM = GMLP_WIDTH // GMLP_GROUPS
GMLP_CHUNK = 128
NUM_BUCKETS = 32
MAX_DISTANCE = 128
D_FF = 2816
EPS = 1e-6
NEG_INF = -1e30
IN_WIDTH = ATTN_WIDTH + 2 * KV_WIDTH + 2 * GMLP_WIDTH + 2 * D_MODEL
SPLIT_POINTS = (ATTN_WIDTH,
                ATTN_WIDTH + KV_WIDTH,
                ATTN_WIDTH + 2 * KV_WIDTH,
                ATTN_WIDTH + 2 * KV_WIDTH + GMLP_WIDTH,
                ATTN_WIDTH + 2 * KV_WIDTH + 2 * GMLP_WIDTH,
                ATTN_WIDTH + 2 * KV_WIDTH + 2 * GMLP_WIDTH + D_MODEL)

kernel_name = 'hybrid_swa_gmlp_streaming_step'


def rms_norm(x, g):
    xf = x.astype(jnp.float32)
    y = xf * lax.rsqrt(jnp.mean(xf * xf, axis=-1, keepdims=True) + EPS)
    return (y * g.astype(jnp.float32)).astype(x.dtype)


def layer_norm(x, g, b):
    xf = x.astype(jnp.float32)
    mu = jnp.mean(xf, axis=-1, keepdims=True)
    xc = xf - mu
    var = jnp.mean(xc * xc, axis=-1, keepdims=True)
    return (xc * lax.rsqrt(var + EPS) * g.astype(jnp.float32) + b.astype(jnp.float32)).astype(x.dtype)


def swiglu(x, w_gate, w_up, w_down):
    return (jax.nn.silu(x @ w_gate) * (x @ w_up)) @ w_down


def t5_bucket(rel):
    half = NUM_BUCKETS // 2
    max_exact = half // 2
    ret = jnp.where(rel > 0, half, 0)
    n = jnp.abs(rel)
    nf = jnp.maximum(n, 1).astype(jnp.float32)
    large = max_exact + (jnp.log(nf / max_exact) / math.log(MAX_DISTANCE / max_exact)
                         * (half - max_exact)).astype(jnp.int32)
    large = jnp.minimum(large, half - 1)
    return ret + jnp.where(n < max_exact, n, large)


def relative_bias(table, q_pos, k_pos):
    rel = k_pos[None, :] - q_pos[:, None]
    b = table[t5_bucket(rel)].astype(jnp.float32)
    b = b.reshape(q_pos.shape[0], k_pos.shape[0], N_KV_HEADS, Q_PER_KV)
    return jnp.transpose(b, (2, 3, 0, 1))


def sink_attention(q, k, v, bias, sinks, mask=None):
    s = jnp.einsum('...qhgd,...khd->...hgqk', q, k).astype(jnp.float32) * (HEAD_DIM ** -0.5) + bias
    if mask is not None:
        s = jnp.where(mask, s, NEG_INF)
    sink = sinks.astype(jnp.float32).reshape(N_KV_HEADS, Q_PER_KV)[:, :, None, None]
    sink_col = jnp.broadcast_to(sink, s.shape[:-1] + (1,))
    p = jax.nn.softmax(jnp.concatenate([s, sink_col], axis=-1), axis=-1)[..., :-1]
    return jnp.einsum('...hgqk,...khd->...qhgd', p.astype(v.dtype), v)


def window_attention_prompt(q, k, v, rel_table, sinks):
    B, S = q.shape[0], q.shape[1]
    n_chunks = S // CHUNK
    n_keys = (WINDOW_CHUNKS + 1) * CHUNK
    qb = q.reshape(B, n_chunks, CHUNK, N_KV_HEADS, Q_PER_KV, HEAD_DIM)

    def key_blocks(t):
        tp = jnp.pad(t.reshape(B, S, N_KV_HEADS, HEAD_DIM),
                     ((0, 0), (WINDOW_CHUNKS * CHUNK, 0), (0, 0), (0, 0)))
        tp = tp.reshape(B, n_chunks + WINDOW_CHUNKS, CHUNK, N_KV_HEADS, HEAD_DIM)
        return jnp.concatenate([tp[:, i:i + n_chunks] for i in range(WINDOW_CHUNKS + 1)], axis=2)

    kb, vb = key_blocks(k), key_blocks(v)
    key_chunk = jnp.arange(n_chunks)[:, None] + jnp.arange(n_keys)[None, :] // CHUNK - WINDOW_CHUNKS
    mask = (key_chunk >= 0)[None, :, None, None, None, :]
    bias = relative_bias(rel_table, jnp.arange(CHUNK), jnp.arange(n_keys) - WINDOW_CHUNKS * CHUNK)
    o = sink_attention(qb, kb, vb, bias, sinks, mask)
    return o.reshape(B, S, ATTN_WIDTH)


def spatial_gate(u, vn, w_s, b_s):
    L = u.shape[2]
    blk = jnp.arange(L) // CHUNK
    mask = blk[None, :] <= blk[:, None]
    w = jnp.where(mask[None], w_s[:, :L, :L], 0).astype(vn.dtype)
    bias = jnp.transpose(b_s[:, :L])[:, :, None].astype(vn.dtype)
    return u * (jnp.einsum('gij,bcjgd->bcigd', w, vn) + bias)


def prompt_mixer(q, k, v, u, vn, rel_table, sinks, w_s, b_s):
    B, S = q.shape[0], q.shape[1]
    attn = window_attention_prompt(q, k, v, rel_table, sinks)
    nb = S // GMLP_CHUNK
    shp = (B, nb, GMLP_CHUNK, GMLP_GROUPS, GMLP_GROUP_DIM)
    gm = spatial_gate(u.reshape(shp), vn.reshape(shp), w_s, b_s).reshape(B, S, GMLP_WIDTH)
    kw = k.reshape(B, S, N_KV_HEADS, HEAD_DIM)[:, S - WINDOW:]
    vw = v.reshape(B, S, N_KV_HEADS, HEAD_DIM)[:, S - WINDOW:]
    return attn, gm, (kw, vw)


def sample_mixer(q, k, v, u, vn, cache_k, cache_v, rel_table, sinks, w_s, b_s):
    Bd, Q = q.shape[0], q.shape[1]
    qh = q.reshape(Bd, Q, N_KV_HEADS, Q_PER_KV, HEAD_DIM)
    kh = k.reshape(Bd, Q, N_KV_HEADS, HEAD_DIM)
    vh = v.reshape(Bd, Q, N_KV_HEADS, HEAD_DIM)
    n_cache = cache_k.shape[1]
    k_all = jnp.concatenate([cache_k.astype(kh.dtype), kh], axis=1)
    v_all = jnp.concatenate([cache_v.astype(vh.dtype), vh], axis=1)
    bias = relative_bias(rel_table, jnp.arange(Q), jnp.arange(n_cache + Q) - n_cache)
    attn = sink_attention(qh, k_all, v_all, bias, sinks).reshape(Bd, Q, ATTN_WIDTH)
    shp = (Bd, 1, Q, GMLP_GROUPS, GMLP_GROUP_DIM)
    gm = spatial_gate(u.reshape(shp), vn.reshape(shp), w_s, b_s).reshape(Bd, Q, GMLP_WIDTH)
    return attn, gm, (kh, vh, vn)


def layer_forward(x, mixer, norm_g, f1_g, f1_u, f1_d, w_in, ln_g, ln_b,
                  w_ba, w_bg, w_o, f2_g, f2_u, f2_d):
    x = x + 0.5 * rms_norm(swiglu(rms_norm(x, norm_g[0]), f1_g, f1_u, f1_d), norm_g[1])
    h = rms_norm(x, norm_g[2])
    q, k, v, u, gv, ga, gb = jnp.split(h @ w_in, SPLIT_POINTS, axis=-1)
    u = jax.nn.gelu(u, approximate=False)
    vn = layer_norm(jax.nn.gelu(gv, approximate=False), ln_g, ln_b)
    attn, gm, state = mixer(q, k, v, u, vn)
    merged = jax.nn.sigmoid(ga) * (attn @ w_ba) + jax.nn.sigmoid(gb) * (gm @ w_bg)
    x = x + rms_norm(merged @ w_o, norm_g[3])
    x = x + 0.5 * rms_norm(swiglu(rms_norm(x, norm_g[4]), f2_g, f2_u, f2_d), norm_g[5])
    return x, state


def setup_inputs(seed: int = 0) -> dict:
    key = jax.random.key(seed)
    ks = jax.random.split(key, 24)
    f32 = jnp.float32

    def nrm(k, shape, scale):
        return jax.random.normal(k, shape, f32) * scale

    n_cache = min(WINDOW, PAST_LEN)
    return {
        'x_prompt': nrm(ks[0], (BATCH, SEQ, D_MODEL), 1.0),
        'x_sample': nrm(ks[1], (DEC_BATCH, DEC_SEQ, D_MODEL), 1.0),
        'cache_win_k': nrm(ks[2], (DEPTH, DEC_BATCH, n_cache, N_KV_HEADS, HEAD_DIM), 1.0),
        'cache_win_v': nrm(ks[3], (DEPTH, DEC_BATCH, n_cache, N_KV_HEADS, HEAD_DIM), 1.0),
        'rel_bias_table': nrm(ks[4], (NUM_BUCKETS, N_Q_HEADS), 0.5),
        'norm_gains': 1.0 + nrm(ks[5], (DEPTH, 6, D_MODEL), 0.05),
        'ffn1_w_gate': nrm(ks[6], (DEPTH, D_MODEL, D_FF), D_MODEL ** -0.5),
        'ffn1_w_up': nrm(ks[7], (DEPTH, D_MODEL, D_FF), D_MODEL ** -0.5),
        'ffn1_w_down': nrm(ks[8], (DEPTH, D_FF, D_MODEL), D_FF ** -0.5),
        'w_in': nrm(ks[9], (DEPTH, D_MODEL, IN_WIDTH), D_MODEL ** -0.5),
        'attn_sinks': nrm(ks[10], (DEPTH, N_Q_HEADS), 0.5),
        'gmlp_ln_g': 1.0 + nrm(ks[11], (DEPTH, GMLP_WIDTH), 0.05),
        'gmlp_ln_b': nrm(ks[12], (DEPTH, GMLP_WIDTH), 0.02),
        'gmlp_w_s': nrm(ks[13], (DEPTH, GMLP_GROUPS, GMLP_CHUNK, GMLP_CHUNK), GMLP_CHUNK ** -0.5),
        'gmlp_b_s': 1.0 + nrm(ks[14], (DEPTH, GMLP_GROUPS, GMLP_CHUNK), 0.05),
        'w_branch_attn': nrm(ks[15], (DEPTH, ATTN_WIDTH, D_MODEL), ATTN_WIDTH ** -0.5),
        'w_branch_gmlp': nrm(ks[16], (DEPTH, GMLP_WIDTH, D_MODEL), GMLP_WIDTH ** -0.5),
        'w_out': nrm(ks[17], (DEPTH, D_MODEL, D_MODEL), D_MODEL ** -0.5),
        'ffn2_w_gate': nrm(ks[18], (DEPTH, D_MODEL, D_FF), D_MODEL ** -0.5),
        'ffn2_w_up': nrm(ks[19], (DEPTH, D_MODEL, D_FF), D_MODEL ** -0.5),
        'ffn2_w_down': nrm(ks[20], (DEPTH, D_FF, D_MODEL), D_FF ** -0.5),
    }


def reference(x_prompt, x_sample, cache_win_k, cache_win_v, rel_bias_table, norm_gains,
              ffn1_w_gate, ffn1_w_up, ffn1_w_down, w_in, attn_sinks, gmlp_ln_g, gmlp_ln_b,
              gmlp_w_s, gmlp_b_s, w_branch_attn, w_branch_gmlp, w_out,
              ffn2_w_gate, ffn2_w_up, ffn2_w_down):
    xp, xs = x_prompt, x_sample
    kp_list, vp_list, ks_list, vs_list, gs_list = [], [], [], [], []
    for l in range(DEPTH):
        shared = (norm_gains[l], ffn1_w_gate[l], ffn1_w_up[l], ffn1_w_down[l], w_in[l],
                  gmlp_ln_g[l], gmlp_ln_b[l], w_branch_attn[l], w_branch_gmlp[l], w_out[l],
                  ffn2_w_gate[l], ffn2_w_up[l], ffn2_w_down[l])
        p_mix = functools.partial(prompt_mixer, rel_table=rel_bias_table, sinks=attn_sinks[l],
                                  w_s=gmlp_w_s[l], b_s=gmlp_b_s[l])
        s_mix = functools.partial(sample_mixer, cache_k=cache_win_k[l], cache_v=cache_win_v[l],
                                  rel_table=rel_bias_table, sinks=attn_sinks[l],
                                  w_s=gmlp_w_s[l], b_s=gmlp_b_s[l])
        xp, (kw, vw) = layer_forward(xp, p_mix, *shared)
        xs, (kn, vn_rows, gv_rows) = layer_forward(xs, s_mix, *shared)
        kp_list.append(kw)
        vp_list.append(vw)
        ks_list.append(kn)
        vs_list.append(vn_rows)
        gs_list.append(gv_rows)
    win_k_prompt = jnp.stack(kp_list, axis=0)
    win_v_prompt = jnp.stack(vp_list, axis=0)
    new_k_sample = jnp.stack(ks_list, axis=0)
    new_v_sample = jnp.stack(vs_list, axis=0)
    gmlp_v_sample = jnp.stack(gs_list, axis=0)
    return (xp, xs, win_k_prompt, win_v_prompt, new_k_sample, new_v_sample, gmlp_v_sample)
```

```python
import functools
import math

import jax
import jax.numpy as jnp
import numpy as np
from jax import lax
from jax.experimental import pallas as pl
from jax.experimental.pallas import tpu as pltpu

D_MODEL = 1024
CHUNK = 64
N_Q_HEADS = 16
N_KV_HEADS = 2
HEAD_DIM = 64
Q_PER_KV = N_Q_HEADS // N_KV_HEADS
ATTN_WIDTH = N_Q_HEADS * HEAD_DIM
KV_WIDTH = N_KV_HEADS * HEAD_DIM
WINDOW = 128
GMLP_WIDTH = 1024
GMLP_GROUPS = 4
GMLP_GROUP_DIM = GMLP_WIDTH // GMLP_GROUPS
GMLP_CHUNK = 128
NUM_BUCKETS = 32
MAX_DISTANCE = 128
D_FF = 2816
EPS = 1e-6
NEG_INF = -1e30

OFF_Q = 0
OFF_KV = ATTN_WIDTH
OFF_U = OFF_KV + 2 * KV_WIDTH
OFF_GV = OFF_U + GMLP_WIDTH
OFF_GA = OFF_GV + GMLP_WIDTH
OFF_GB = OFF_GA + D_MODEL

V7X_VMEM_BYTES = 64 * 1024 * 1024
VMEM_LIMIT_BYTES = V7X_VMEM_BYTES - 8 * 1024 * 1024
MXU_TILE = 256

BF16 = jnp.bfloat16
F32 = jnp.float32


def _dot(a, b):
    return jnp.dot(a, b, preferred_element_type=F32)


def _rms(x, g):
    return x * lax.rsqrt(jnp.mean(x * x, axis=-1, keepdims=True) + EPS) * g


def _resident(shape):
    zeros = (0,) * len(shape)
    return pl.BlockSpec(shape, lambda *_: zeros, pipeline_mode=pl.Buffered(1))


def _params(n_axes):
    return pltpu.CompilerParams(dimension_semantics=("arbitrary",) * n_axes,
                                vmem_limit_bytes=VMEM_LIMIT_BYTES)


def _ff_chunks():
    step = 3 * MXU_TILE
    return [(s, min(step, D_FF - s)) for s in range(0, D_FF, step)]


def _ffn_body(emit_next, x_ref, g_ref, wg_ref, wu_ref, wd_ref, *out_refs):
    x = x_ref[...]
    h = _rms(x, g_ref[0:1, :]).astype(BF16)
    acc = None
    for start, size in _ff_chunks():
        gate = _dot(h, wg_ref[:, start:start + size])
        up = _dot(h, wu_ref[:, start:start + size])
        act = (jax.nn.silu(gate) * up).astype(BF16)
        part = _dot(act, wd_ref[start:start + size, :])
        acc = part if acc is None else acc + part
    y = x + 0.5 * _rms(acc, g_ref[1:2, :])
    out_refs[0][...] = y
    if emit_next:
        out_refs[1][...] = _rms(y, g_ref[2:3, :]).astype(BF16)


def _ffn(x, gains, wg, wu, wd, *, tm, emit_next, name):
    t = x.shape[0]
    assert t % tm == 0
    row = lambda i: (i, 0)
    out_shape = [jax.ShapeDtypeStruct((t, D_MODEL), F32)]
    out_specs = [pl.BlockSpec((tm, D_MODEL), row)]
    if emit_next:
        out_shape.append(jax.ShapeDtypeStruct((t, D_MODEL), BF16))
        out_specs.append(pl.BlockSpec((tm, D_MODEL), row))
    return pl.pallas_call(
        functools.partial(_ffn_body, emit_next),
        grid=(t // tm,),
        in_specs=[pl.BlockSpec((tm, D_MODEL), row), _resident(gains.shape),
                  _resident(wg.shape), _resident(wu.shape), _resident(wd.shape)],
        out_specs=out_specs,
        out_shape=out_shape,
        compiler_params=_params(1),
        name=name,
    )(x, gains, wg, wu, wd)


def _gelu(x):
    return 0.5 * x * (1.0 + lax.erf(x * np.sqrt(0.5).astype(np.float32)))


def _layer_norm(x, g, b):
    mu = jnp.mean(x, axis=-1, keepdims=True)
    xc = x - mu
    var = jnp.mean(xc * xc, axis=-1, keepdims=True)
    return xc * lax.rsqrt(var + EPS) * g + b


def _in_proj_body(h_ref, w_ref, ln_ref, q_ref, kv_ref, u_ref, vn_ref, ga_ref, gb_ref):
    h = h_ref[...]
    q_ref[...] = (_dot(h, w_ref[:, OFF_Q:OFF_KV]) * (HEAD_DIM ** -0.5)).astype(q_ref.dtype)
    kv_ref[...] = _dot(h, w_ref[:, OFF_KV:OFF_U])
    u_ref[...] = _gelu(_dot(h, w_ref[:, OFF_U:OFF_GV])).astype(u_ref.dtype)
    gv = _gelu(_dot(h, w_ref[:, OFF_GV:OFF_GA]))
    vn_ref[...] = _layer_norm(gv, ln_ref[0:1, :], ln_ref[1:2, :]).astype(vn_ref.dtype)
    ga_ref[...] = jax.nn.sigmoid(_dot(h, w_ref[:, OFF_GA:OFF_GB])).astype(ga_ref.dtype)
    gb_ref[...] = jax.nn.sigmoid(_dot(h, w_ref[:, OFF_GB:])).astype(gb_ref.dtype)


def _in_proj(h, w_in, ln, *, tm, vn_dtype, name):
    t = h.shape[0]
    assert t % tm == 0
    row = lambda i: (i, 0)
    wide = lambda dt: jax.ShapeDtypeStruct((t, D_MODEL), dt)
    wide_spec = pl.BlockSpec((tm, D_MODEL), row)
    return pl.pallas_call(
        _in_proj_body,
        grid=(t // tm,),
        in_specs=[wide_spec, _resident(w_in.shape), _resident(ln.shape)],
        out_specs=[wide_spec, pl.BlockSpec((tm, 2 * KV_WIDTH), row),
                   wide_spec, wide_spec, wide_spec, wide_spec],
        out_shape=[wide(BF16), jax.ShapeDtypeStruct((t, 2 * KV_WIDTH), F32),
                   wide(BF16), wide(vn_dtype), wide(BF16), wide(BF16)],
        compiler_params=_params(1),
        name=name,
    )(h, w_in, ln)


def _t5_bucket(rel):
    half = NUM_BUCKETS // 2
    max_exact = half // 2
    ret = jnp.where(rel > 0, half, 0)
    n = jnp.abs(rel)
    nf = jnp.maximum(n, 1).astype(F32)
    large = max_exact + (jnp.log(nf / max_exact) / math.log(MAX_DISTANCE / max_exact)
                         * (half - max_exact)).astype(jnp.int32)
    large = jnp.minimum(large, half - 1)
    return ret + jnp.where(n < max_exact, n, large)


def _bias_body(table_ref, bucket_ref, out_ref):
    bucket = bucket_ref[...]
    hits = [bucket == b for b in range(NUM_BUCKETS)]
    for head in range(N_Q_HEADS):
        acc = jnp.zeros(bucket.shape, F32)
        for b in range(NUM_BUCKETS):
            acc = jnp.where(hits[b], table_ref[b, head], acc)
        out_ref[head] = acc


def _relative_bias(table, n_q, n_keys, n_past):
    rel = (jnp.arange(n_keys) - n_past)[None, :] - jnp.arange(n_q)[:, None]
    bucket = _t5_bucket(rel).astype(jnp.int32)
    bias = pl.pallas_call(
        _bias_body,
        in_specs=[pl.BlockSpec(memory_space=pltpu.SMEM),
                  pl.BlockSpec(memory_space=pltpu.VMEM)],
        out_specs=pl.BlockSpec(memory_space=pltpu.VMEM),
        out_shape=jax.ShapeDtypeStruct((N_Q_HEADS, n_q, n_keys), F32),
        name="rel_bias_%d" % n_q,
    )(table, bucket)
    return bias.reshape(N_KV_HEADS, Q_PER_KV * n_q, n_keys)


def _sink_attention(qh, kh, vh, bias, sink, invalid=None):
    s = lax.dot_general(qh, kh, (((1,), (1,)), ((), ())), preferred_element_type=F32) + bias
    if invalid is not None:
        s = jnp.where(invalid, NEG_INF, s)
    m = jnp.maximum(jnp.max(s, axis=-1, keepdims=True), sink)
    p = jnp.exp(s - m)
    denom = jnp.sum(p, axis=-1, keepdims=True) + jnp.exp(sink - m)
    return _dot(p.astype(BF16), vh) / denom


def _stack_heads(q, kv_head):
    base = kv_head * Q_PER_KV * HEAD_DIM
    return jnp.concatenate(
        [q[:, base + g * HEAD_DIM: base + (g + 1) * HEAD_DIM] for g in range(Q_PER_KV)], axis=0)


def _unstack_heads(o, n):
    return jnp.concatenate([o[g * n:(g + 1) * n, :] for g in range(Q_PER_KV)], axis=1)


def _merge_out(x1, attn, gm, ga, gb, wba_ref, wbg_ref, wo_ref, g_post):
    merged = ga * _dot(attn, wba_ref[...]) + gb * _dot(gm, wbg_ref[...])
    return x1 + _rms(_dot(merged.astype(BF16), wo_ref[...]), g_post)


def _prompt_mixer_body(ts, q_ref, kv_ref, kvp_ref, u_ref, vn_ref, ga_ref, gb_ref, x1_ref,
                       bias_ref, sink_ref, ws_ref, bs_ref, wba_ref, wbg_ref, wo_ref, g_ref,
                       out_ref, attn_ref, gm_ref):
    first_tile = pl.program_id(1) == 0
    kv_all = jnp.concatenate([kvp_ref[0], kv_ref[0]], axis=0).astype(BF16)
    n_keys = WINDOW + CHUNK
    key_chunk = lax.broadcasted_iota(jnp.int32, (1, n_keys), 1) // CHUNK
    for j in range(ts // CHUNK):
        rows = slice(j * CHUNK, (j + 1) * CHUNK)
        q = q_ref[0, rows, :]
        invalid = None
        if j < WINDOW // CHUNK:
            invalid = jnp.logical_and(first_tile, key_chunk + (j - WINDOW // CHUNK) < 0)
        for h in range(N_KV_HEADS):
            kh = kv_all[j * CHUNK: j * CHUNK + n_keys, h * HEAD_DIM:(h + 1) * HEAD_DIM]
            vh = kv_all[j * CHUNK: j * CHUNK + n_keys,
                        KV_WIDTH + h * HEAD_DIM: KV_WIDTH + (h + 1) * HEAD_DIM]
            o = _sink_attention(_stack_heads(q, h), kh, vh, bias_ref[h], sink_ref[h], invalid)
            width = Q_PER_KV * HEAD_DIM
            attn_ref[rows, h * width:(h + 1) * width] = _unstack_heads(o, CHUNK).astype(BF16)

    blk = lax.broadcasted_iota(jnp.int32, (GMLP_CHUNK, GMLP_CHUNK), 0) // CHUNK
    blk_t = lax.broadcasted_iota(jnp.int32, (GMLP_CHUNK, GMLP_CHUNK), 1) // CHUNK
    for g in range(GMLP_GROUPS):
        w = jnp.where(blk_t <= blk, ws_ref[g], 0.0).astype(BF16)
        b = bs_ref[:, g:g + 1]
        cols = slice(g * GMLP_GROUP_DIM, (g + 1) * GMLP_GROUP_DIM)
        for c in range(ts // GMLP_CHUNK):
            rows = slice(c * GMLP_CHUNK, (c + 1) * GMLP_CHUNK)
            sp = _dot(w, vn_ref[0, rows, cols]) + b
            gm_ref[rows, cols] = (u_ref[0, rows, cols].astype(F32) * sp).astype(BF16)

    out_ref[0] = _merge_out(x1_ref[0], attn_ref[...], gm_ref[...],
                            ga_ref[0].astype(F32), gb_ref[0].astype(F32),
                            wba_ref, wbg_ref, wo_ref, g_ref[...])


def _prompt_mixer(q, kv, u, vn, ga, gb, x1, bias, sink, w_s, b_s_t, wba, wbg, wo, g_post, *, ts):
    batch, seq, _ = q.shape
    assert seq % ts == 0 and ts % GMLP_CHUNK == 0
    tile = lambda b, t: (b, t, 0)
    prev = lambda b, t: (b, jnp.maximum(t * (ts // WINDOW) - 1, 0), 0)
    wide = pl.BlockSpec((1, ts, D_MODEL), tile)
    return pl.pallas_call(
        functools.partial(_prompt_mixer_body, ts),
        grid=(batch, seq // ts),
        in_specs=[wide, pl.BlockSpec((1, ts, 2 * KV_WIDTH), tile),
                  pl.BlockSpec((1, WINDOW, 2 * KV_WIDTH), prev),
                  wide, wide, wide, wide, wide,
                  _resident(bias.shape), _resident(sink.shape), _resident(w_s.shape),
                  _resident(b_s_t.shape), _resident(wba.shape), _resident(wbg.shape),
                  _resident(wo.shape), _resident(g_post.shape)],
        out_specs=wide,
        out_shape=jax.ShapeDtypeStruct((batch, seq, D_MODEL), F32),
        scratch_shapes=[pltpu.VMEM((ts, ATTN_WIDTH), BF16), pltpu.VMEM((ts, GMLP_WIDTH), BF16)],
        compiler_params=_params(2),
        name="prompt_mixer",
    )(q, kv, kv, u, vn, ga, gb, x1, bias, sink, w_s, b_s_t, wba, wbg, wo, g_post)


def _sample_mixer_body(n_batch, n_new, q_ref, kv_ref, ck_ref, cv_ref, u_ref, vn_ref, ga_ref, gb_ref,
                       x1_ref, bias_ref, sink_ref, ws_ref, bs_ref, wba_ref, wbg_ref, wo_ref, g_ref,
                       out_ref, attn_ref, gm_ref):
    for b in range(n_batch):
        rows = slice(b * n_new, (b + 1) * n_new)
        q = q_ref[rows, :]
        kv = kv_ref[rows, :]
        k_all = jnp.concatenate([ck_ref[b], kv[:, :KV_WIDTH]], axis=0).astype(BF16)
        v_all = jnp.concatenate([cv_ref[b], kv[:, KV_WIDTH:]], axis=0).astype(BF16)
        for h in range(N_KV_HEADS):
            cols = slice(h * HEAD_DIM, (h + 1) * HEAD_DIM)
            o = _sink_attention(_stack_heads(q, h), k_all[:, cols], v_all[:, cols],
                                bias_ref[h], sink_ref[h])
            width = Q_PER_KV * HEAD_DIM
            attn_ref[rows, h * width:(h + 1) * width] = _unstack_heads(o, n_new).astype(BF16)
        for g in range(GMLP_GROUPS):
            cols = slice(g * GMLP_GROUP_DIM, (g + 1) * GMLP_GROUP_DIM)
            w = ws_ref[g, :n_new, :n_new].astype(BF16)
            sp = _dot(w, vn_ref[rows, cols].astype(BF16)) + bs_ref[:n_new, g:g + 1]
            gm_ref[rows, cols] = (u_ref[rows, cols].astype(F32) * sp).astype(BF16)

    out_ref[...] = _merge_out(x1_ref[...], attn_ref[...], gm_ref[...],
                              ga_ref[...].astype(F32), gb_ref[...].astype(F32),
                              wba_ref, wbg_ref, wo_ref, g_ref[...])


def _sample_mixer(q, kv, cache_k, cache_v, u, vn, ga, gb, x1, bias, sink, w_s, b_s_t,
                  wba, wbg, wo, g_post, *, n_batch, n_new):
    assert n_new <= CHUNK
    t = q.shape[0]
    vmem = pl.BlockSpec(memory_space=pltpu.VMEM)
    return pl.pallas_call(
        functools.partial(_sample_mixer_body, n_batch, n_new),
        in_specs=[vmem] * 17,
        out_specs=vmem,
        out_shape=jax.ShapeDtypeStruct((t, D_MODEL), F32),
        scratch_shapes=[pltpu.VMEM((t, ATTN_WIDTH), BF16), pltpu.VMEM((t, GMLP_WIDTH), BF16)],
        compiler_params=pltpu.CompilerParams(vmem_limit_bytes=VMEM_LIMIT_BYTES),
        name="sample_mixer",
    )(q, kv, cache_k, cache_v, u, vn, ga, gb, x1, bias, sink, w_s, b_s_t, wba, wbg, wo, g_post)


def _sink_rows(sinks, n_q):
    s = jnp.broadcast_to(sinks.astype(F32).reshape(N_KV_HEADS, Q_PER_KV, 1), (N_KV_HEADS, Q_PER_KV, n_q))
    return s.reshape(N_KV_HEADS, Q_PER_KV * n_q, 1)


PROMPT_TM = 512
PROMPT_TS = 256


def kernel(x_prompt, x_sample, cache_win_k, cache_win_v, rel_bias_table, norm_gains, ffn1_w_gate, ffn1_w_up, ffn1_w_down, w_in, attn_sinks, gmlp_ln_g, gmlp_ln_b, gmlp_w_s, gmlp_b_s, w_branch_attn, w_branch_gmlp, w_out, ffn2_w_gate, ffn2_w_up, ffn2_w_down):
    depth = norm_gains.shape[0]
    batch, seq, _ = x_prompt.shape
    dec_batch, dec_seq, _ = x_sample.shape
    n_cache = cache_win_k.shape[2]
    assert seq % CHUNK == 0 and seq >= WINDOW and dec_seq <= CHUNK

    bias_p = _relative_bias(rel_bias_table, CHUNK, WINDOW + CHUNK, WINDOW)
    bias_s = _relative_bias(rel_bias_table, dec_seq, n_cache + dec_seq, n_cache)

    xp = x_prompt.reshape(batch * seq, D_MODEL)
    xs = x_sample.reshape(dec_batch * dec_seq, D_MODEL)
    t_s = xs.shape[0]
    kp, vp, ks, vs, gs = [], [], [], [], []
    for l in range(depth):
        g = norm_gains[l].astype(F32)
        w1 = [w[l].astype(BF16) for w in (ffn1_w_gate, ffn1_w_up, ffn1_w_down)]
        w2 = [w[l].astype(BF16) for w in (ffn2_w_gate, ffn2_w_up, ffn2_w_down)]
        win = w_in[l].astype(BF16)
        wba, wbg, wo = (w[l].astype(BF16) for w in (w_branch_attn, w_branch_gmlp, w_out))
        ln = jnp.stack([gmlp_ln_g[l], gmlp_ln_b[l]]).astype(F32)
        w_s = gmlp_w_s[l].astype(F32)
        b_s_t = jnp.transpose(gmlp_b_s[l]).astype(F32)
        g_ffn1, g_post2, g_ffn2 = g[0:3], g[3:4], jnp.concatenate([g[4:6], g[5:6]])
        sink_p = _sink_rows(attn_sinks[l], CHUNK)
        sink_s = _sink_rows(attn_sinks[l], dec_seq)

        x1, h2 = _ffn(xp, g_ffn1, *w1, tm=PROMPT_TM, emit_next=True, name="ffn1_prompt")
        q, kv, u, vn, ga, gb = _in_proj(h2, win, ln, tm=PROMPT_TM, vn_dtype=BF16, name="in_proj_prompt")
        b3 = lambda a: a.reshape(batch, seq, a.shape[-1])
        x2 = _prompt_mixer(b3(q), b3(kv), b3(u), b3(vn), b3(ga), b3(gb), b3(x1), bias_p, sink_p,
                           w_s, b_s_t, wba, wbg, wo, g_post2, ts=PROMPT_TS)
        (xp,) = _ffn(x2.reshape(batch * seq, D_MODEL), g_ffn2, *w2, tm=PROMPT_TM, emit_next=False,
                     name="ffn2_prompt")
        kv_win = b3(kv)[:, seq - WINDOW:, :]
        kp.append(kv_win[..., :KV_WIDTH].reshape(batch, WINDOW, N_KV_HEADS, HEAD_DIM))
        vp.append(kv_win[..., KV_WIDTH:].reshape(batch, WINDOW, N_KV_HEADS, HEAD_DIM))

        x1, h2 = _ffn(xs, g_ffn1, *w1, tm=t_s, emit_next=True, name="ffn1_sample")
        q, kv, u, vn, ga, gb = _in_proj(h2, win, ln, tm=t_s, vn_dtype=F32, name="in_proj_sample")
        ck = cache_win_k[l].reshape(dec_batch, n_cache, KV_WIDTH)
        cv = cache_win_v[l].reshape(dec_batch, n_cache, KV_WIDTH)
        x2 = _sample_mixer(q, kv, ck, cv, u, vn, ga, gb, x1, bias_s, sink_s, w_s, b_s_t,
                           wba, wbg, wo, g_post2, n_batch=dec_batch, n_new=dec_seq)
        (xs,) = _ffn(x2, g_ffn2, *w2, tm=t_s, emit_next=False, name="ffn2_sample")
        ks.append(kv[:, :KV_WIDTH].reshape(dec_batch, dec_seq, N_KV_HEADS, HEAD_DIM))
        vs.append(kv[:, KV_WIDTH:].reshape(dec_batch, dec_seq, N_KV_HEADS, HEAD_DIM))
        gs.append(vn.reshape(dec_batch, dec_seq, GMLP_WIDTH))

    return (xp.reshape(batch, seq, D_MODEL), xs.reshape(dec_batch, dec_seq, D_MODEL),
            jnp.stack(kp), jnp.stack(vp), jnp.stack(ks), jnp.stack(vs), jnp.stack(gs))
```

```python
import functools
import math

import jax
import jax.numpy as jnp
import numpy as np
from jax import lax
from jax.experimental import pallas as pl
from jax.experimental.pallas import tpu as pltpu

D_MODEL = 1024
CHUNK = 64
N_Q_HEADS = 16
N_KV_HEADS = 2
HEAD_DIM = 64
Q_PER_KV = N_Q_HEADS // N_KV_HEADS
ATTN_WIDTH = N_Q_HEADS * HEAD_DIM
KV_WIDTH = N_KV_HEADS * HEAD_DIM
WINDOW = 128
GMLP_WIDTH = 1024
GMLP_GROUPS = 4
GMLP_GROUP_DIM = GMLP_WIDTH // GMLP_GROUPS
GMLP_CHUNK = 128
NUM_BUCKETS = 32
MAX_DISTANCE = 128
D_FF = 2816
EPS = 1e-6
NEG_INF = -1e30

OFF_Q = 0
OFF_KV = ATTN_WIDTH
OFF_U = OFF_KV + 2 * KV_WIDTH
OFF_GV = OFF_U + GMLP_WIDTH
OFF_GA = OFF_GV + GMLP_WIDTH
OFF_GB = OFF_GA + D_MODEL

V7X_VMEM_BYTES = 64 * 1024 * 1024
VMEM_LIMIT_BYTES = V7X_VMEM_BYTES - 8 * 1024 * 1024
MXU_TILE = 256
LANES = 128

QBLK = 2 * CHUNK
KBLK = WINDOW + QBLK
HEADS_PER_TILE = LANES // HEAD_DIM
N_HEAD_PAIRS = N_Q_HEADS // HEADS_PER_TILE
PAIRS_PER_KV = Q_PER_KV // HEADS_PER_TILE
PAIR_GROUPS = PAIRS_PER_KV // 2

BF16 = jnp.bfloat16
F32 = jnp.float32


def _dot(a, b):
    return jnp.dot(a, b, preferred_element_type=F32)


def _rms(x, g):
    return x * lax.rsqrt(jnp.mean(x * x, axis=-1, keepdims=True) + EPS) * g


def _resident(shape):
    zeros = (0,) * len(shape)
    return pl.BlockSpec(shape, lambda *_: zeros, pipeline_mode=pl.Buffered(1))


def _params(n_axes):
    return pltpu.CompilerParams(dimension_semantics=("arbitrary",) * n_axes,
                                vmem_limit_bytes=VMEM_LIMIT_BYTES)


def _ff_chunks():
    step = 3 * MXU_TILE
    return [(s, min(step, D_FF - s)) for s in range(0, D_FF, step)]


def _ffn_body(emit_next, x_ref, g_ref, wg_ref, wu_ref, wd_ref, *out_refs):
    x = x_ref[...]
    h = _rms(x, g_ref[0:1, :]).astype(BF16)
    acc = None
    for start, size in _ff_chunks():
        gate = _dot(h, wg_ref[:, start:start + size])
        up = _dot(h, wu_ref[:, start:start + size])
        act = (jax.nn.silu(gate) * up).astype(BF16)
        part = _dot(act, wd_ref[start:start + size, :])
        acc = part if acc is None else acc + part
    y = x + 0.5 * _rms(acc, g_ref[1:2, :])
    out_refs[0][...] = y
    if emit_next:
        out_refs[1][...] = _rms(y, g_ref[2:3, :]).astype(BF16)


def _ffn(x, gains, wg, wu, wd, *, tm, emit_next, name):
    t = x.shape[0]
    assert t % tm == 0
    row = lambda i: (i, 0)
    out_shape = [jax.ShapeDtypeStruct((t, D_MODEL), F32)]
    out_specs = [pl.BlockSpec((tm, D_MODEL), row)]
    if emit_next:
        out_shape.append(jax.ShapeDtypeStruct((t, D_MODEL), BF16))
        out_specs.append(pl.BlockSpec((tm, D_MODEL), row))
    return pl.pallas_call(
        functools.partial(_ffn_body, emit_next),
        grid=(t // tm,),
        in_specs=[pl.BlockSpec((tm, D_MODEL), row), _resident(gains.shape),
                  _resident(wg.shape), _resident(wu.shape), _resident(wd.shape)],
        out_specs=out_specs,
        out_shape=out_shape,
        compiler_params=_params(1),
        name=name,
    )(x, gains, wg, wu, wd)


def _gelu(x):
    return 0.5 * x * (1.0 + lax.erf(x * np.sqrt(0.5).astype(np.float32)))


def _layer_norm(x, g, b):
    mu = jnp.mean(x, axis=-1, keepdims=True)
    xc = x - mu
    var = jnp.mean(xc * xc, axis=-1, keepdims=True)
    return xc * lax.rsqrt(var + EPS) * g + b


def _in_proj_body(q_blocked, h_ref, w_ref, ln_ref, q_ref, kv_ref, u_ref, vn_ref, ga_ref, gb_ref):
    h = h_ref[...]
    q = (_dot(h, w_ref[:, OFF_Q:OFF_KV]) * (HEAD_DIM ** -0.5)).astype(q_ref.dtype)
    if q_blocked:
        for b in range(q.shape[0] // QBLK):
            for p in range(N_HEAD_PAIRS):
                q_ref[b, p * QBLK:(p + 1) * QBLK, :] = q[b * QBLK:(b + 1) * QBLK, p * LANES:(p + 1) * LANES]
    else:
        q_ref[...] = q
    kv_ref[...] = _dot(h, w_ref[:, OFF_KV:OFF_U])
    u_ref[...] = _gelu(_dot(h, w_ref[:, OFF_U:OFF_GV])).astype(u_ref.dtype)
    gv = _gelu(_dot(h, w_ref[:, OFF_GV:OFF_GA]))
    vn_ref[...] = _layer_norm(gv, ln_ref[0:1, :], ln_ref[1:2, :]).astype(vn_ref.dtype)
    ga_ref[...] = jax.nn.sigmoid(_dot(h, w_ref[:, OFF_GA:OFF_GB])).astype(ga_ref.dtype)
    gb_ref[...] = jax.nn.sigmoid(_dot(h, w_ref[:, OFF_GB:])).astype(gb_ref.dtype)


def _in_proj(h, w_in, ln, *, tm, vn_dtype, q_blocked, name):
    t = h.shape[0]
    assert t % tm == 0
    row = lambda i: (i, 0)
    wide = lambda dt: jax.ShapeDtypeStruct((t, D_MODEL), dt)
    wide_spec = pl.BlockSpec((tm, D_MODEL), row)
    if q_blocked:
        assert tm % QBLK == 0
        q_shape = jax.ShapeDtypeStruct((t // QBLK, N_HEAD_PAIRS * QBLK, LANES), BF16)
        q_spec = pl.BlockSpec((tm // QBLK, N_HEAD_PAIRS * QBLK, LANES), lambda i: (i, 0, 0))
    else:
        q_shape, q_spec = wide(BF16), wide_spec
    return pl.pallas_call(
        functools.partial(_in_proj_body, q_blocked),
        grid=(t // tm,),
        in_specs=[wide_spec, _resident(w_in.shape), _resident(ln.shape)],
        out_specs=[q_spec, pl.BlockSpec((tm, 2 * KV_WIDTH), row),
                   wide_spec, wide_spec, wide_spec, wide_spec],
        out_shape=[q_shape, jax.ShapeDtypeStruct((t, 2 * KV_WIDTH), F32),
                   wide(BF16), wide(vn_dtype), wide(BF16), wide(BF16)],
        compiler_params=_params(1),
        name=name,
    )(h, w_in, ln)


def _t5_bucket(rel):
    half = NUM_BUCKETS // 2
    max_exact = half // 2
    ret = jnp.where(rel > 0, half, 0)
    n = jnp.abs(rel)
    nf = jnp.maximum(n, 1).astype(F32)
    large = max_exact + (jnp.log(nf / max_exact) / math.log(MAX_DISTANCE / max_exact)
                         * (half - max_exact)).astype(jnp.int32)
    large = jnp.minimum(large, half - 1)
    return ret + jnp.where(n < max_exact, n, large)


def _bias_body(table_ref, bucket_ref, out_ref):
    bucket = bucket_ref[...]
    hits = [bucket == b for b in range(NUM_BUCKETS)]
    for head in range(N_Q_HEADS):
        acc = jnp.zeros(bucket.shape, F32)
        for b in range(NUM_BUCKETS):
            acc = jnp.where(hits[b], table_ref[b, head], acc)
        out_ref[head] = acc


def _prompt_bias_body(table_ref, bucket_ref, out_ref):
    bucket = bucket_ref[...]
    hits = [bucket == b for b in range(NUM_BUCKETS)]
    key_chunk = lax.broadcasted_iota(jnp.int32, bucket.shape, 0) // CHUNK
    q_chunk = lax.broadcasted_iota(jnp.int32, bucket.shape, 1) // CHUNK
    in_window = jnp.logical_and(key_chunk >= q_chunk, key_chunk <= q_chunk + WINDOW // CHUNK)
    valid = [in_window, jnp.logical_and(in_window, key_chunk >= WINDOW // CHUNK)]
    for head in range(N_Q_HEADS):
        acc = jnp.zeros(bucket.shape, F32)
        for b in range(NUM_BUCKETS):
            acc = jnp.where(hits[b], table_ref[b, head], acc)
        h, rest = divmod(head, Q_PER_KV)
        pair, par = divmod(rest, HEADS_PER_TILE)
        pg, pl_ = divmod(pair, 2)
        for v in range(2):
            out_ref[v, h, pg, par * KBLK:(par + 1) * KBLK, pl_ * QBLK:(pl_ + 1) * QBLK] = (
                jnp.where(valid[v], acc, NEG_INF))


def _prompt_bias(table):
    rel = (jnp.arange(KBLK) - WINDOW)[:, None] - jnp.arange(QBLK)[None, :]
    bucket = _t5_bucket(rel).astype(jnp.int32)
    return pl.pallas_call(
        _prompt_bias_body,
        in_specs=[pl.BlockSpec(memory_space=pltpu.SMEM),
                  pl.BlockSpec(memory_space=pltpu.VMEM)],
        out_specs=pl.BlockSpec(memory_space=pltpu.VMEM),
        out_shape=jax.ShapeDtypeStruct((2, N_KV_HEADS, PAIR_GROUPS, 2 * KBLK, 2 * QBLK), F32),
        name="rel_bias_prompt",
    )(table, bucket)


def _relative_bias(table, n_q, n_keys, n_past):
    rel = (jnp.arange(n_keys) - n_past)[None, :] - jnp.arange(n_q)[:, None]
    bucket = _t5_bucket(rel).astype(jnp.int32)
    bias = pl.pallas_call(
        _bias_body,
        in_specs=[pl.BlockSpec(memory_space=pltpu.SMEM),
                  pl.BlockSpec(memory_space=pltpu.VMEM)],
        out_specs=pl.BlockSpec(memory_space=pltpu.VMEM),
        out_shape=jax.ShapeDtypeStruct((N_Q_HEADS, n_q, n_keys), F32),
        name="rel_bias_%d" % n_q,
    )(table, bucket)
    return bias.reshape(N_KV_HEADS, Q_PER_KV * n_q, n_keys)


def _sink_attention(qh, kh, vh, bias, sink, invalid=None):
    s = lax.dot_general(qh, kh, (((1,), (1,)), ((), ())), preferred_element_type=F32) + bias
    if invalid is not None:
        s = jnp.where(invalid, NEG_INF, s)
    m = jnp.maximum(jnp.max(s, axis=-1, keepdims=True), sink)
    p = jnp.exp(s - m)
    denom = jnp.sum(p, axis=-1, keepdims=True) + jnp.exp(sink - m)
    return _dot(p.astype(BF16), vh) / denom


def _stack_heads(q, kv_head):
    base = kv_head * Q_PER_KV * HEAD_DIM
    return jnp.concatenate(
        [q[:, base + g * HEAD_DIM: base + (g + 1) * HEAD_DIM] for g in range(Q_PER_KV)], axis=0)


def _unstack_heads(o, n):
    return jnp.concatenate([o[g * n:(g + 1) * n, :] for g in range(Q_PER_KV)], axis=1)


def _merge_out(x1, attn, gm, ga, gb, wba_ref, wbg_ref, wo_ref, g_post):
    merged = ga * _dot(attn, wba_ref[...]) + gb * _dot(gm, wbg_ref[...])
    return x1 + _rms(_dot(merged.astype(BF16), wo_ref[...]), g_post)


def _prompt_mixer_body(ts, q_ref, kv_ref, kvp_ref, u_ref, vn_ref, ga_ref, gb_ref, x1_ref,
                       bias_ref, sink_ref, ws_ref, bs_ref, wba_ref, wbg_ref, wo_ref, g_ref,
                       out_ref, attn_t_ref, gm_ref):
    first_variant = jnp.where(pl.program_id(1) == 0, 1, 0)
    kv_all = jnp.concatenate([kvp_ref[0], kv_ref[0]], axis=0)
    k_all = kv_all[:, :KV_WIDTH]
    v_t = jnp.transpose(kv_all[:, KV_WIDTH:]).astype(BF16)
    low = lax.broadcasted_iota(jnp.int32, k_all.shape, 1) < HEAD_DIM
    k_swapped = pltpu.roll(k_all, HEAD_DIM, axis=1)
    zero = jnp.zeros_like(k_all)
    k_par = [[jnp.where(low, k_all, zero), jnp.where(low, zero, k_swapped)],
             [jnp.where(low, k_swapped, zero), jnp.where(low, zero, k_all)]]
    k_par = [[k.astype(BF16) for k in ks] for ks in k_par]

    for blk in range(ts // QBLK):
        keys = slice(blk * QBLK, blk * QBLK + KBLK)
        cols = slice(blk * QBLK, (blk + 1) * QBLK)
        variant = first_variant if blk == 0 else 0
        for h in range(N_KV_HEADS):
            k_blk = jnp.concatenate([k_par[h][0][keys], k_par[h][1][keys]], axis=0)
            vh_t = v_t[h * HEAD_DIM:(h + 1) * HEAD_DIM, keys]
            for pg in range(PAIR_GROUPS):
                pair0 = h * PAIRS_PER_KV + pg * 2
                qa = q_ref[blk, pair0 * QBLK:(pair0 + 2) * QBLK, :]
                s = lax.dot_general(k_blk, qa, (((1,), (1,)), ((), ())), preferred_element_type=F32)
                s = s + bias_ref[variant, h, pg]
                for par in range(HEADS_PER_TILE):
                    sp = s[par * KBLK:(par + 1) * KBLK]
                    row = (h * PAIR_GROUPS + pg) * HEADS_PER_TILE + par
                    sink = sink_ref[row:row + 1, :]
                    m = jnp.maximum(jnp.max(sp, axis=0, keepdims=True), sink)
                    p = jnp.exp(sp - m)
                    denom = jnp.sum(p, axis=0, keepdims=True) + jnp.exp(sink - m)
                    o = _dot(vh_t, p.astype(BF16)) * (1.0 / denom)
                    for pl_ in range(2):
                        head = (pair0 + pl_) * HEADS_PER_TILE + par
                        attn_t_ref[head * HEAD_DIM:(head + 1) * HEAD_DIM, cols] = (
                            o[:, pl_ * QBLK:(pl_ + 1) * QBLK])

    blk_i = lax.broadcasted_iota(jnp.int32, (GMLP_CHUNK, GMLP_CHUNK), 0) // CHUNK
    blk_j = lax.broadcasted_iota(jnp.int32, (GMLP_CHUNK, GMLP_CHUNK), 1) // CHUNK
    for g in range(GMLP_GROUPS):
        w = jnp.where(blk_j <= blk_i, ws_ref[g], 0.0).astype(BF16)
        b = bs_ref[:, g:g + 1]
        cols = slice(g * GMLP_GROUP_DIM, (g + 1) * GMLP_GROUP_DIM)
        for c in range(ts // GMLP_CHUNK):
            rows = slice(c * GMLP_CHUNK, (c + 1) * GMLP_CHUNK)
            sp = _dot(w, vn_ref[0, rows, cols]) + b
            gm_ref[rows, cols] = (u_ref[0, rows, cols].astype(F32) * sp).astype(BF16)

    attn = jnp.transpose(attn_t_ref[...]).astype(BF16)
    out_ref[0] = _merge_out(x1_ref[0], attn, gm_ref[...],
                            ga_ref[0].astype(F32), gb_ref[0].astype(F32),
                            wba_ref, wbg_ref, wo_ref, g_ref[...])


def _prompt_mixer(q, kv, u, vn, ga, gb, x1, bias, sink, w_s, b_s_t, wba, wbg, wo, g_post, *, ts):
    batch, seq, _ = kv.shape
    assert seq % ts == 0 and ts % QBLK == 0 and QBLK == GMLP_CHUNK == WINDOW
    tile = lambda b, t: (b, t, 0)
    prev = lambda b, t: (b, jnp.maximum(t * (ts // WINDOW) - 1, 0), 0)
    wide = pl.BlockSpec((1, ts, D_MODEL), tile)
    n_t = seq // ts
    q_spec = pl.BlockSpec((ts // QBLK, N_HEAD_PAIRS * QBLK, LANES), lambda b, t: (b * n_t + t, 0, 0))
    return pl.pallas_call(
        functools.partial(_prompt_mixer_body, ts),
        grid=(batch, n_t),
        in_specs=[q_spec, pl.BlockSpec((1, ts, 2 * KV_WIDTH), tile),
                  pl.BlockSpec((1, WINDOW, 2 * KV_WIDTH), prev),
                  wide, wide, wide, wide, wide,
                  _resident(bias.shape), _resident(sink.shape), _resident(w_s.shape),
                  _resident(b_s_t.shape), _resident(wba.shape), _resident(wbg.shape),
                  _resident(wo.shape), _resident(g_post.shape)],
        out_specs=wide,
        out_shape=jax.ShapeDtypeStruct((batch, seq, D_MODEL), F32),
        scratch_shapes=[pltpu.VMEM((ATTN_WIDTH, ts), F32), pltpu.VMEM((ts, GMLP_WIDTH), BF16)],
        compiler_params=_params(2),
        name="prompt_mixer",
    )(q, kv, kv, u, vn, ga, gb, x1, bias, sink, w_s, b_s_t, wba, wbg, wo, g_post)


def _sample_mixer_body(n_batch, n_new, q_ref, kv_ref, ck_ref, cv_ref, u_ref, vn_ref, ga_ref, gb_ref,
                       x1_ref, bias_ref, sink_ref, ws_ref, bs_ref, wba_ref, wbg_ref, wo_ref, g_ref,
                       out_ref, attn_ref, gm_ref):
    for b in range(n_batch):
        rows = slice(b * n_new, (b + 1) * n_new)
        q = q_ref[rows, :]
        kv = kv_ref[rows, :]
        k_all = jnp.concatenate([ck_ref[b], kv[:, :KV_WIDTH]], axis=0).astype(BF16)
        v_all = jnp.concatenate([cv_ref[b], kv[:, KV_WIDTH:]], axis=0).astype(BF16)
        for h in range(N_KV_HEADS):
            cols = slice(h * HEAD_DIM, (h + 1) * HEAD_DIM)
            o = _sink_attention(_stack_heads(q, h), k_all[:, cols], v_all[:, cols],
                                bias_ref[h], sink_ref[h])
            width = Q_PER_KV * HEAD_DIM
            attn_ref[rows, h * width:(h + 1) * width] = _unstack_heads(o, n_new).astype(BF16)
        for g in range(GMLP_GROUPS):
            cols = slice(g * GMLP_GROUP_DIM, (g + 1) * GMLP_GROUP_DIM)
            w = ws_ref[g, :n_new, :n_new].astype(BF16)
            sp = _dot(w, vn_ref[rows, cols].astype(BF16)) + bs_ref[:n_new, g:g + 1]
            gm_ref[rows, cols] = (u_ref[rows, cols].astype(F32) * sp).astype(BF16)

    out_ref[...] = _merge_out(x1_ref[...], attn_ref[...], gm_ref[...],
                              ga_ref[...].astype(F32), gb_ref[...].astype(F32),
                              wba_ref, wbg_ref, wo_ref, g_ref[...])


def _sample_mixer(q, kv, cache_k, cache_v, u, vn, ga, gb, x1, bias, sink, w_s, b_s_t,
                  wba, wbg, wo, g_post, *, n_batch, n_new):
    assert n_new <= CHUNK
    t = q.shape[0]
    vmem = pl.BlockSpec(memory_space=pltpu.VMEM)
    return pl.pallas_call(
        functools.partial(_sample_mixer_body, n_batch, n_new),
        in_specs=[vmem] * 17,
        out_specs=vmem,
        out_shape=jax.ShapeDtypeStruct((t, D_MODEL), F32),
        scratch_shapes=[pltpu.VMEM((t, ATTN_WIDTH), BF16), pltpu.VMEM((t, GMLP_WIDTH), BF16)],
        compiler_params=pltpu.CompilerParams(vmem_limit_bytes=VMEM_LIMIT_BYTES),
        name="sample_mixer",
    )(q, kv, cache_k, cache_v, u, vn, ga, gb, x1, bias, sink, w_s, b_s_t, wba, wbg, wo, g_post)


def _prompt_sink_rows(sinks):
    s = sinks.astype(F32).reshape(N_KV_HEADS, PAIR_GROUPS, 2, HEADS_PER_TILE)
    s = jnp.transpose(s, (0, 1, 3, 2))[..., None]
    s = jnp.broadcast_to(s, (N_KV_HEADS, PAIR_GROUPS, HEADS_PER_TILE, 2, QBLK))
    return s.reshape(N_KV_HEADS * PAIR_GROUPS * HEADS_PER_TILE, 2 * QBLK)


def _sink_rows(sinks, n_q):
    s = jnp.broadcast_to(sinks.astype(F32).reshape(N_KV_HEADS, Q_PER_KV, 1), (N_KV_HEADS, Q_PER_KV, n_q))
    return s.reshape(N_KV_HEADS, Q_PER_KV * n_q, 1)


PROMPT_TM = 512
PROMPT_TS = 256


def kernel(x_prompt, x_sample, cache_win_k, cache_win_v, rel_bias_table, norm_gains, ffn1_w_gate, ffn1_w_up, ffn1_w_down, w_in, attn_sinks, gmlp_ln_g, gmlp_ln_b, gmlp_w_s, gmlp_b_s, w_branch_attn, w_branch_gmlp, w_out, ffn2_w_gate, ffn2_w_up, ffn2_w_down):
    depth = norm_gains.shape[0]
    batch, seq, _ = x_prompt.shape
    dec_batch, dec_seq, _ = x_sample.shape
    n_cache = cache_win_k.shape[2]
    assert seq % PROMPT_TS == 0 and dec_seq <= CHUNK

    bias_p = _prompt_bias(rel_bias_table)
    bias_s = _relative_bias(rel_bias_table, dec_seq, n_cache + dec_seq, n_cache)

    xp = x_prompt.reshape(batch * seq, D_MODEL)
    xs = x_sample.reshape(dec_batch * dec_seq, D_MODEL)
    t_s = xs.shape[0]
    kp, vp, ks, vs, gs = [], [], [], [], []
    for l in range(depth):
        g = norm_gains[l].astype(F32)
        w1 = [w[l].astype(BF16) for w in (ffn1_w_gate, ffn1_w_up, ffn1_w_down)]
        w2 = [w[l].astype(BF16) for w in (ffn2_w_gate, ffn2_w_up, ffn2_w_down)]
        win = w_in[l].astype(BF16)
        wba, wbg, wo = (w[l].astype(BF16) for w in (w_branch_attn, w_branch_gmlp, w_out))
        ln = jnp.stack([gmlp_ln_g[l], gmlp_ln_b[l]]).astype(F32)
        w_s = gmlp_w_s[l].astype(F32)
        b_s_t = jnp.transpose(gmlp_b_s[l]).astype(F32)
        g_ffn1, g_post2, g_ffn2 = g[0:3], g[3:4], jnp.concatenate([g[4:6], g[5:6]])
        sink_p = _prompt_sink_rows(attn_sinks[l])
        sink_s = _sink_rows(attn_sinks[l], dec_seq)

        x1, h2 = _ffn(xp, g_ffn1, *w1, tm=PROMPT_TM, emit_next=True, name="ffn1_prompt")
        q, kv, u, vn, ga, gb = _in_proj(h2, win, ln, tm=PROMPT_TM, vn_dtype=BF16, q_blocked=True,
                                        name="in_proj_prompt")
        b3 = lambda a: a.reshape(batch, seq, a.shape[-1])
        x2 = _prompt_mixer(q, b3(kv), b3(u), b3(vn), b3(ga), b3(gb), b3(x1), bias_p, sink_p,
                           w_s, b_s_t, wba, wbg, wo, g_post2, ts=PROMPT_TS)
        (xp,) = _ffn(x2.reshape(batch * seq, D_MODEL), g_ffn2, *w2, tm=PROMPT_TM, emit_next=False,
                     name="ffn2_prompt")
        kv_win = b3(kv)[:, seq - WINDOW:, :]
        kp.append(kv_win[..., :KV_WIDTH].reshape(batch, WINDOW, N_KV_HEADS, HEAD_DIM))
        vp.append(kv_win[..., KV_WIDTH:].reshape(batch, WINDOW, N_KV_HEADS, HEAD_DIM))

        x1, h2 = _ffn(xs, g_ffn1, *w1, tm=t_s, emit_next=True, name="ffn1_sample")
        q, kv, u, vn, ga, gb = _in_proj(h2, win, ln, tm=t_s, vn_dtype=F32, q_blocked=False,
                                        name="in_proj_sample")
        ck = cache_win_k[l].reshape(dec_batch, n_cache, KV_WIDTH)
        cv = cache_win_v[l].reshape(dec_batch, n_cache, KV_WIDTH)
        x2 = _sample_mixer(q, kv, ck, cv, u, vn, ga, gb, x1, bias_s, sink_s, w_s, b_s_t,
                           wba, wbg, wo, g_post2, n_batch=dec_batch, n_new=dec_seq)
        (xs,) = _ffn(x2, g_ffn2, *w2, tm=t_s, emit_next=False, name="ffn2_sample")
        ks.append(kv[:, :KV_WIDTH].reshape(dec_batch, dec_seq, N_KV_HEADS, HEAD_DIM))
        vs.append(kv[:, KV_WIDTH:].reshape(dec_batch, dec_seq, N_KV_HEADS, HEAD_DIM))
        gs.append(vn.reshape(dec_batch, dec_seq, GMLP_WIDTH))

    return (xp.reshape(batch, seq, D_MODEL), xs.reshape(dec_batch, dec_seq, D_MODEL),
            jnp.stack(kp), jnp.stack(vp), jnp.stack(ks), jnp.stack(vs), jnp.stack(gs))
```

```python
import functools
import math

import jax
import jax.numpy as jnp
import numpy as np
from jax import lax
from jax.experimental import pallas as pl
from jax.experimental.pallas import tpu as pltpu

D_MODEL = 1024
CHUNK = 64
N_Q_HEADS = 16
N_KV_HEADS = 2
HEAD_DIM = 64
Q_PER_KV = N_Q_HEADS // N_KV_HEADS
ATTN_WIDTH = N_Q_HEADS * HEAD_DIM
KV_WIDTH = N_KV_HEADS * HEAD_DIM
WINDOW = 128
GMLP_WIDTH = 1024
GMLP_GROUPS = 4
GMLP_GROUP_DIM = GMLP_WIDTH // GMLP_GROUPS
GMLP_CHUNK = 128
NUM_BUCKETS = 32
MAX_DISTANCE = 128
D_FF = 2816
EPS = 1e-6
NEG_INF = -1e30

OFF_Q = 0
OFF_KV = ATTN_WIDTH
OFF_U = OFF_KV + 2 * KV_WIDTH
OFF_GV = OFF_U + GMLP_WIDTH
OFF_GA = OFF_GV + GMLP_WIDTH
OFF_GB = OFF_GA + D_MODEL

V7X_VMEM_BYTES = 64 * 1024 * 1024
VMEM_LIMIT_BYTES = V7X_VMEM_BYTES - 8 * 1024 * 1024
MXU_TILE = 256
LANES = 128

QBLK = 2 * CHUNK
KBLK = WINDOW + QBLK
HEADS_PER_TILE = LANES // HEAD_DIM
N_HEAD_PAIRS = N_Q_HEADS // HEADS_PER_TILE
PAIRS_PER_KV = Q_PER_KV // HEADS_PER_TILE
PAIR_GROUPS = PAIRS_PER_KV // 2

BF16 = jnp.bfloat16
F32 = jnp.float32


def _dot(a, b):
    return jnp.dot(a, b, preferred_element_type=F32)


def _rms(x, g):
    return x * lax.rsqrt(jnp.mean(x * x, axis=-1, keepdims=True) + EPS) * g


def _resident(shape):
    zeros = (0,) * len(shape)
    return pl.BlockSpec(shape, lambda *_: zeros, pipeline_mode=pl.Buffered(1))


def _params(n_axes):
    return pltpu.CompilerParams(dimension_semantics=("arbitrary",) * n_axes,
                                vmem_limit_bytes=VMEM_LIMIT_BYTES)


def _ff_chunks():
    step = 3 * MXU_TILE
    return [(s, min(step, D_FF - s)) for s in range(0, D_FF, step)]


def _ffn_body(emit_next, x_ref, g_ref, wg_ref, wu_ref, wd_ref, *out_refs):
    x = x_ref[...]
    h = _rms(x, g_ref[0:1, :]).astype(BF16)
    acc = None
    for start, size in _ff_chunks():
        gate = _dot(h, wg_ref[:, start:start + size])
        up = _dot(h, wu_ref[:, start:start + size])
        act = (jax.nn.silu(gate) * up).astype(BF16)
        part = _dot(act, wd_ref[start:start + size, :])
        acc = part if acc is None else acc + part
    y = x + 0.5 * _rms(acc, g_ref[1:2, :])
    out_refs[0][...] = y
    if emit_next:
        out_refs[1][...] = _rms(y, g_ref[2:3, :]).astype(BF16)


def _ffn(x, gains, wg, wu, wd, *, tm, emit_next, name):
    t = x.shape[0]
    assert t % tm == 0
    row = lambda i: (i, 0)
    out_shape = [jax.ShapeDtypeStruct((t, D_MODEL), F32)]
    out_specs = [pl.BlockSpec((tm, D_MODEL), row)]
    if emit_next:
        out_shape.append(jax.ShapeDtypeStruct((t, D_MODEL), BF16))
        out_specs.append(pl.BlockSpec((tm, D_MODEL), row))
    return pl.pallas_call(
        functools.partial(_ffn_body, emit_next),
        grid=(t // tm,),
        in_specs=[pl.BlockSpec((tm, D_MODEL), row), _resident(gains.shape),
                  _resident(wg.shape), _resident(wu.shape), _resident(wd.shape)],
        out_specs=out_specs,
        out_shape=out_shape,
        compiler_params=_params(1),
        name=name,
    )(x, gains, wg, wu, wd)


def _gelu(x):
    return 0.5 * x * (1.0 + lax.erf(x * np.sqrt(0.5).astype(np.float32)))


def _layer_norm(x, g, b):
    mu = jnp.mean(x, axis=-1, keepdims=True)
    xc = x - mu
    var = jnp.mean(xc * xc, axis=-1, keepdims=True)
    return xc * lax.rsqrt(var + EPS) * g + b


def _in_proj_body(q_blocked, h_ref, w_ref, ln_ref, q_ref, kv_ref, u_ref, vn_ref, ga_ref, gb_ref):
    h = h_ref[...]
    q = (_dot(h, w_ref[:, OFF_Q:OFF_KV]) * (HEAD_DIM ** -0.5)).astype(q_ref.dtype)
    if q_blocked:
        for b in range(q.shape[0] // QBLK):
            for p in range(N_HEAD_PAIRS):
                q_ref[b, p * QBLK:(p + 1) * QBLK, :] = q[b * QBLK:(b + 1) * QBLK, p * LANES:(p + 1) * LANES]
    else:
        q_ref[...] = q
    kv_ref[...] = _dot(h, w_ref[:, OFF_KV:OFF_U])
    u_ref[...] = _gelu(_dot(h, w_ref[:, OFF_U:OFF_GV])).astype(u_ref.dtype)
    gv = _gelu(_dot(h, w_ref[:, OFF_GV:OFF_GA]))
    vn_ref[...] = _layer_norm(gv, ln_ref[0:1, :], ln_ref[1:2, :]).astype(vn_ref.dtype)
    ga_ref[...] = jax.nn.sigmoid(_dot(h, w_ref[:, OFF_GA:OFF_GB])).astype(ga_ref.dtype)
    gb_ref[...] = jax.nn.sigmoid(_dot(h, w_ref[:, OFF_GB:])).astype(gb_ref.dtype)


def _in_proj(h, w_in, ln, *, tm, vn_dtype, q_blocked, name):
    t = h.shape[0]
    assert t % tm == 0
    row = lambda i: (i, 0)
    wide = lambda dt: jax.ShapeDtypeStruct((t, D_MODEL), dt)
    wide_spec = pl.BlockSpec((tm, D_MODEL), row)
    if q_blocked:
        assert tm % QBLK == 0
        q_shape = jax.ShapeDtypeStruct((t // QBLK, N_HEAD_PAIRS * QBLK, LANES), BF16)
        q_spec = pl.BlockSpec((tm // QBLK, N_HEAD_PAIRS * QBLK, LANES), lambda i: (i, 0, 0))
    else:
        q_shape, q_spec = wide(BF16), wide_spec
    return pl.pallas_call(
        functools.partial(_in_proj_body, q_blocked),
        grid=(t // tm,),
        in_specs=[wide_spec, _resident(w_in.shape), _resident(ln.shape)],
        out_specs=[q_spec, pl.BlockSpec((tm, 2 * KV_WIDTH), row),
                   wide_spec, wide_spec, wide_spec, wide_spec],
        out_shape=[q_shape, jax.ShapeDtypeStruct((t, 2 * KV_WIDTH), F32),
                   wide(BF16), wide(vn_dtype), wide(BF16), wide(BF16)],
        compiler_params=_params(1),
        name=name,
    )(h, w_in, ln)


def _t5_bucket(rel):
    half = NUM_BUCKETS // 2
    max_exact = half // 2
    ret = jnp.where(rel > 0, half, 0)
    n = jnp.abs(rel)
    nf = jnp.maximum(n, 1).astype(F32)
    large = max_exact + (jnp.log(nf / max_exact) / math.log(MAX_DISTANCE / max_exact)
                         * (half - max_exact)).astype(jnp.int32)
    large = jnp.minimum(large, half - 1)
    return ret + jnp.where(n < max_exact, n, large)


def _bias_body(table_ref, bucket_ref, out_ref):
    bucket = bucket_ref[...]
    hits = [bucket == b for b in range(NUM_BUCKETS)]
    for head in range(N_Q_HEADS):
        acc = jnp.zeros(bucket.shape, F32)
        for b in range(NUM_BUCKETS):
            acc = jnp.where(hits[b], table_ref[b, head], acc)
        out_ref[head] = acc


def _prompt_bias_body(table_ref, bucket_ref, out_ref):
    bucket = bucket_ref[...]
    hits = [bucket == b for b in range(NUM_BUCKETS)]
    key_chunk = lax.broadcasted_iota(jnp.int32, bucket.shape, 0) // CHUNK
    q_chunk = lax.broadcasted_iota(jnp.int32, bucket.shape, 1) // CHUNK
    in_window = jnp.logical_and(key_chunk >= q_chunk, key_chunk <= q_chunk + WINDOW // CHUNK)
    valid = [in_window, jnp.logical_and(in_window, key_chunk >= WINDOW // CHUNK)]
    for head in range(N_Q_HEADS):
        acc = jnp.zeros(bucket.shape, F32)
        for b in range(NUM_BUCKETS):
            acc = jnp.where(hits[b], table_ref[b, head], acc)
        h, rest = divmod(head, Q_PER_KV)
        pair, par = divmod(rest, HEADS_PER_TILE)
        pg, pl_ = divmod(pair, 2)
        for v in range(2):
            out_ref[v, h, pg, par * KBLK:(par + 1) * KBLK, pl_ * QBLK:(pl_ + 1) * QBLK] = (
                jnp.where(valid[v], acc, NEG_INF))


def _prompt_bias(table):
    rel = (jnp.arange(KBLK) - WINDOW)[:, None] - jnp.arange(QBLK)[None, :]
    bucket = _t5_bucket(rel).astype(jnp.int32)
    return pl.pallas_call(
        _prompt_bias_body,
        in_specs=[pl.BlockSpec(memory_space=pltpu.SMEM),
                  pl.BlockSpec(memory_space=pltpu.VMEM)],
        out_specs=pl.BlockSpec(memory_space=pltpu.VMEM),
        out_shape=jax.ShapeDtypeStruct((2, N_KV_HEADS, PAIR_GROUPS, 2 * KBLK, 2 * QBLK), F32),
        name="rel_bias_prompt",
    )(table, bucket)


def _relative_bias(table, n_q, n_keys, n_past):
    rel = (jnp.arange(n_keys) - n_past)[None, :] - jnp.arange(n_q)[:, None]
    bucket = _t5_bucket(rel).astype(jnp.int32)
    bias = pl.pallas_call(
        _bias_body,
        in_specs=[pl.BlockSpec(memory_space=pltpu.SMEM),
                  pl.BlockSpec(memory_space=pltpu.VMEM)],
        out_specs=pl.BlockSpec(memory_space=pltpu.VMEM),
        out_shape=jax.ShapeDtypeStruct((N_Q_HEADS, n_q, n_keys), F32),
        name="rel_bias_%d" % n_q,
    )(table, bucket)
    return bias.reshape(N_KV_HEADS, Q_PER_KV * n_q, n_keys)


def _sink_attention(qh, kh, vh, bias, sink, invalid=None):
    s = lax.dot_general(qh, kh, (((1,), (1,)), ((), ())), preferred_element_type=F32) + bias
    if invalid is not None:
        s = jnp.where(invalid, NEG_INF, s)
    m = jnp.maximum(jnp.max(s, axis=-1, keepdims=True), sink)
    p = jnp.exp(s - m)
    denom = jnp.sum(p, axis=-1, keepdims=True) + jnp.exp(sink - m)
    return _dot(p.astype(BF16), vh) / denom


def _stack_heads(q, kv_head):
    base = kv_head * Q_PER_KV * HEAD_DIM
    return jnp.concatenate(
        [q[:, base + g * HEAD_DIM: base + (g + 1) * HEAD_DIM] for g in range(Q_PER_KV)], axis=0)


def _unstack_heads(o, n):
    return jnp.concatenate([o[g * n:(g + 1) * n, :] for g in range(Q_PER_KV)], axis=1)


def _merge_out(x1, attn, gm, ga, gb, wba_ref, wbg_ref, wo_ref, g_post):
    merged = ga * _dot(attn, wba_ref[...]) + gb * _dot(gm, wbg_ref[...])
    return x1 + _rms(_dot(merged.astype(BF16), wo_ref[...]), g_post)


def _prompt_mixer_body(ts, q_ref, kv_ref, kvp_ref, u_ref, vn_ref, ga_ref, gb_ref, x1_ref,
                       bias_ref, sink_ref, ws_ref, bs_ref, wba_ref, wbg_ref, wo_ref, g_ref,
                       out_ref, attn_t_ref, gm_ref):
    first_variant = jnp.where(pl.program_id(1) == 0, 1, 0)
    kv_all = jnp.concatenate([kvp_ref[0], kv_ref[0]], axis=0)
    k_all = kv_all[:, :KV_WIDTH]
    v_t = jnp.transpose(kv_all[:, KV_WIDTH:]).astype(BF16)
    low = lax.broadcasted_iota(jnp.int32, k_all.shape, 1) < HEAD_DIM
    k_swapped = pltpu.roll(k_all, HEAD_DIM, axis=1)
    zero = jnp.zeros_like(k_all)
    k_par = [[jnp.where(low, k_all, zero), jnp.where(low, zero, k_swapped)],
             [jnp.where(low, k_swapped, zero), jnp.where(low, zero, k_all)]]
    k_par = [[k.astype(BF16) for k in ks] for ks in k_par]

    units = [(blk, h, pg) for blk in range(ts // QBLK) for h in range(N_KV_HEADS)
             for pg in range(PAIR_GROUPS)]

    def scores(blk, h, pg):
        keys = slice(blk * QBLK, blk * QBLK + KBLK)
        k_blk = jnp.concatenate([k_par[h][0][keys], k_par[h][1][keys]], axis=0)
        pair0 = h * PAIRS_PER_KV + pg * 2
        qa = q_ref[blk, pair0 * QBLK:(pair0 + 2) * QBLK, :]
        s = lax.dot_general(k_blk, qa, (((1,), (1,)), ((), ())), preferred_element_type=F32)
        variant = first_variant if blk == 0 else 0
        return s + bias_ref[variant, h, pg]

    def softmax(s, blk, h, pg):
        out = []
        for par in range(HEADS_PER_TILE):
            sp = s[par * KBLK:(par + 1) * KBLK]
            row = (h * PAIR_GROUPS + pg) * HEADS_PER_TILE + par
            sink = sink_ref[row:row + 1, :]
            m = jnp.maximum(jnp.max(sp, axis=0, keepdims=True), sink)
            p = jnp.exp(sp - m)
            denom = jnp.sum(p, axis=0, keepdims=True) + jnp.exp(sink - m)
            out.append((p.astype(BF16), 1.0 / denom))
        return out

    def weighted_values(probs, blk, h, pg):
        keys = slice(blk * QBLK, blk * QBLK + KBLK)
        cols = slice(blk * QBLK, (blk + 1) * QBLK)
        vh_t = v_t[h * HEAD_DIM:(h + 1) * HEAD_DIM, keys]
        pair0 = h * PAIRS_PER_KV + pg * 2
        for par, (p, inv) in enumerate(probs):
            o = _dot(vh_t, p) * inv
            for pl_ in range(2):
                head = (pair0 + pl_) * HEADS_PER_TILE + par
                attn_t_ref[head * HEAD_DIM:(head + 1) * HEAD_DIM, cols] = o[:, pl_ * QBLK:(pl_ + 1) * QBLK]

    blk_i = lax.broadcasted_iota(jnp.int32, (GMLP_CHUNK, GMLP_CHUNK), 0) // CHUNK
    blk_j = lax.broadcasted_iota(jnp.int32, (GMLP_CHUNK, GMLP_CHUNK), 1) // CHUNK
    for g in range(GMLP_GROUPS):
        w = jnp.where(blk_j <= blk_i, ws_ref[g], 0.0).astype(BF16)
        b = bs_ref[:, g:g + 1]
        cols = slice(g * GMLP_GROUP_DIM, (g + 1) * GMLP_GROUP_DIM)
        for c in range(ts // GMLP_CHUNK):
            rows = slice(c * GMLP_CHUNK, (c + 1) * GMLP_CHUNK)
            sp = _dot(w, vn_ref[0, rows, cols]) + b
            gm_ref[rows, cols] = (u_ref[0, rows, cols].astype(F32) * sp).astype(BF16)

    def gmlp_branch(c):
        cols = slice(c * MXU_TILE, (c + 1) * MXU_TILE)
        return gb_ref[0, :, cols].astype(F32) * _dot(gm_ref[...], wbg_ref[:, cols])

    n_side = D_MODEL // MXU_TILE
    side = []
    s_vals, p_vals = {}, {}
    for i in range(len(units) + 2):
        if i < len(units):
            s_vals[i] = scores(*units[i])
        if 1 <= i <= len(units):
            p_vals[i - 1] = softmax(s_vals.pop(i - 1), *units[i - 1])
        if i >= 2:
            weighted_values(p_vals.pop(i - 2), *units[i - 2])
        if i % (len(units) // n_side) == 1 and len(side) < n_side:
            side.append(gmlp_branch(len(side)))
    assert len(side) == n_side

    attn = jnp.transpose(attn_t_ref[...]).astype(BF16)
    merged = ga_ref[0].astype(F32) * _dot(attn, wba_ref[...]) + jnp.concatenate(side, axis=1)
    out_ref[0] = x1_ref[0] + _rms(_dot(merged.astype(BF16), wo_ref[...]), g_ref[...])


def _prompt_mixer(q, kv, u, vn, ga, gb, x1, bias, sink, w_s, b_s_t, wba, wbg, wo, g_post, *, ts):
    batch, seq, _ = kv.shape
    assert seq % ts == 0 and ts % QBLK == 0 and QBLK == GMLP_CHUNK == WINDOW
    tile = lambda b, t: (b, t, 0)
    prev = lambda b, t: (b, jnp.maximum(t * (ts // WINDOW) - 1, 0), 0)
    wide = pl.BlockSpec((1, ts, D_MODEL), tile)
    n_t = seq // ts
    q_spec = pl.BlockSpec((ts // QBLK, N_HEAD_PAIRS * QBLK, LANES), lambda b, t: (b * n_t + t, 0, 0))
    return pl.pallas_call(
        functools.partial(_prompt_mixer_body, ts),
        grid=(batch, n_t),
        in_specs=[q_spec, pl.BlockSpec((1, ts, 2 * KV_WIDTH), tile),
                  pl.BlockSpec((1, WINDOW, 2 * KV_WIDTH), prev),
                  wide, wide, wide, wide, wide,
                  _resident(bias.shape), _resident(sink.shape), _resident(w_s.shape),
                  _resident(b_s_t.shape), _resident(wba.shape), _resident(wbg.shape),
                  _resident(wo.shape), _resident(g_post.shape)],
        out_specs=wide,
        out_shape=jax.ShapeDtypeStruct((batch, seq, D_MODEL), F32),
        scratch_shapes=[pltpu.VMEM((ATTN_WIDTH, ts), F32), pltpu.VMEM((ts, GMLP_WIDTH), BF16)],
        compiler_params=_params(2),
        name="prompt_mixer",
    )(q, kv, kv, u, vn, ga, gb, x1, bias, sink, w_s, b_s_t, wba, wbg, wo, g_post)


def _sample_mixer_body(n_batch, n_new, q_ref, kv_ref, ck_ref, cv_ref, u_ref, vn_ref, ga_ref, gb_ref,
                       x1_ref, bias_ref, sink_ref, ws_ref, bs_ref, wba_ref, wbg_ref, wo_ref, g_ref,
                       out_ref, attn_ref, gm_ref):
    for b in range(n_batch):
        rows = slice(b * n_new, (b + 1) * n_new)
        q = q_ref[rows, :]
        kv = kv_ref[rows, :]
        k_all = jnp.concatenate([ck_ref[b], kv[:, :KV_WIDTH]], axis=0).astype(BF16)
        v_all = jnp.concatenate([cv_ref[b], kv[:, KV_WIDTH:]], axis=0).astype(BF16)
        for h in range(N_KV_HEADS):
            cols = slice(h * HEAD_DIM, (h + 1) * HEAD_DIM)
            o = _sink_attention(_stack_heads(q, h), k_all[:, cols], v_all[:, cols],
                                bias_ref[h], sink_ref[h])
            width = Q_PER_KV * HEAD_DIM
            attn_ref[rows, h * width:(h + 1) * width] = _unstack_heads(o, n_new).astype(BF16)
        for g in range(GMLP_GROUPS):
            cols = slice(g * GMLP_GROUP_DIM, (g + 1) * GMLP_GROUP_DIM)
            w = ws_ref[g, :n_new, :n_new].astype(BF16)
            sp = _dot(w, vn_ref[rows, cols].astype(BF16)) + bs_ref[:n_new, g:g + 1]
            gm_ref[rows, cols] = (u_ref[rows, cols].astype(F32) * sp).astype(BF16)

    out_ref[...] = _merge_out(x1_ref[...], attn_ref[...], gm_ref[...],
                              ga_ref[...].astype(F32), gb_ref[...].astype(F32),
                              wba_ref, wbg_ref, wo_ref, g_ref[...])


def _sample_mixer(q, kv, cache_k, cache_v, u, vn, ga, gb, x1, bias, sink, w_s, b_s_t,
                  wba, wbg, wo, g_post, *, n_batch, n_new):
    assert n_new <= CHUNK
    t = q.shape[0]
    vmem = pl.BlockSpec(memory_space=pltpu.VMEM)
    return pl.pallas_call(
        functools.partial(_sample_mixer_body, n_batch, n_new),
        in_specs=[vmem] * 17,
        out_specs=vmem,
        out_shape=jax.ShapeDtypeStruct((t, D_MODEL), F32),
        scratch_shapes=[pltpu.VMEM((t, ATTN_WIDTH), BF16), pltpu.VMEM((t, GMLP_WIDTH), BF16)],
        compiler_params=pltpu.CompilerParams(vmem_limit_bytes=VMEM_LIMIT_BYTES),
        name="sample_mixer",
    )(q, kv, cache_k, cache_v, u, vn, ga, gb, x1, bias, sink, w_s, b_s_t, wba, wbg, wo, g_post)


def _prompt_sink_rows(sinks):
    s = sinks.astype(F32).reshape(N_KV_HEADS, PAIR_GROUPS, 2, HEADS_PER_TILE)
    s = jnp.transpose(s, (0, 1, 3, 2))[..., None]
    s = jnp.broadcast_to(s, (N_KV_HEADS, PAIR_GROUPS, HEADS_PER_TILE, 2, QBLK))
    return s.reshape(N_KV_HEADS * PAIR_GROUPS * HEADS_PER_TILE, 2 * QBLK)


def _sink_rows(sinks, n_q):
    s = jnp.broadcast_to(sinks.astype(F32).reshape(N_KV_HEADS, Q_PER_KV, 1), (N_KV_HEADS, Q_PER_KV, n_q))
    return s.reshape(N_KV_HEADS, Q_PER_KV * n_q, 1)


PROMPT_TM = 512
PROMPT_TS = 512


def kernel(x_prompt, x_sample, cache_win_k, cache_win_v, rel_bias_table, norm_gains, ffn1_w_gate, ffn1_w_up, ffn1_w_down, w_in, attn_sinks, gmlp_ln_g, gmlp_ln_b, gmlp_w_s, gmlp_b_s, w_branch_attn, w_branch_gmlp, w_out, ffn2_w_gate, ffn2_w_up, ffn2_w_down):
    depth = norm_gains.shape[0]
    batch, seq, _ = x_prompt.shape
    dec_batch, dec_seq, _ = x_sample.shape
    n_cache = cache_win_k.shape[2]
    assert seq % PROMPT_TS == 0 and dec_seq <= CHUNK

    bias_p = _prompt_bias(rel_bias_table)
    bias_s = _relative_bias(rel_bias_table, dec_seq, n_cache + dec_seq, n_cache)

    xp = x_prompt.reshape(batch * seq, D_MODEL)
    xs = x_sample.reshape(dec_batch * dec_seq, D_MODEL)
    t_s = xs.shape[0]
    kp, vp, ks, vs, gs = [], [], [], [], []
    for l in range(depth):
        g = norm_gains[l].astype(F32)
        w1 = [w[l].astype(BF16) for w in (ffn1_w_gate, ffn1_w_up, ffn1_w_down)]
        w2 = [w[l].astype(BF16) for w in (ffn2_w_gate, ffn2_w_up, ffn2_w_down)]
        win = w_in[l].astype(BF16)
        wba, wbg, wo = (w[l].astype(BF16) for w in (w_branch_attn, w_branch_gmlp, w_out))
        ln = jnp.stack([gmlp_ln_g[l], gmlp_ln_b[l]]).astype(F32)
        w_s = gmlp_w_s[l].astype(F32)
        b_s_t = jnp.transpose(gmlp_b_s[l]).astype(F32)
        g_ffn1, g_post2, g_ffn2 = g[0:3], g[3:4], jnp.concatenate([g[4:6], g[5:6]])
        sink_p = _prompt_sink_rows(attn_sinks[l])
        sink_s = _sink_rows(attn_sinks[l], dec_seq)

        x1, h2 = _ffn(xp, g_ffn1, *w1, tm=PROMPT_TM, emit_next=True, name="ffn1_prompt")
        q, kv, u, vn, ga, gb = _in_proj(h2, win, ln, tm=PROMPT_TM, vn_dtype=BF16, q_blocked=True,
                                        name="in_proj_prompt")
        b3 = lambda a: a.reshape(batch, seq, a.shape[-1])
        x2 = _prompt_mixer(q, b3(kv), b3(u), b3(vn), b3(ga), b3(gb), b3(x1), bias_p, sink_p,
                           w_s, b_s_t, wba, wbg, wo, g_post2, ts=PROMPT_TS)
        (xp,) = _ffn(x2.reshape(batch * seq, D_MODEL), g_ffn2, *w2, tm=PROMPT_TM, emit_next=False,
                     name="ffn2_prompt")
        kv_win = b3(kv)[:, seq - WINDOW:, :]
        kp.append(kv_win[..., :KV_WIDTH].reshape(batch, WINDOW, N_KV_HEADS, HEAD_DIM))
        vp.append(kv_win[..., KV_WIDTH:].reshape(batch, WINDOW, N_KV_HEADS, HEAD_DIM))

        x1, h2 = _ffn(xs, g_ffn1, *w1, tm=t_s, emit_next=True, name="ffn1_sample")
        q, kv, u, vn, ga, gb = _in_proj(h2, win, ln, tm=t_s, vn_dtype=F32, q_blocked=False,
                                        name="in_proj_sample")
        ck = cache_win_k[l].reshape(dec_batch, n_cache, KV_WIDTH)
        cv = cache_win_v[l].reshape(dec_batch, n_cache, KV_WIDTH)
        x2 = _sample_mixer(q, kv, ck, cv, u, vn, ga, gb, x1, bias_s, sink_s, w_s, b_s_t,
                           wba, wbg, wo, g_post2, n_batch=dec_batch, n_new=dec_seq)
        (xs,) = _ffn(x2, g_ffn2, *w2, tm=t_s, emit_next=False, name="ffn2_sample")
        ks.append(kv[:, :KV_WIDTH].reshape(dec_batch, dec_seq, N_KV_HEADS, HEAD_DIM))
        vs.append(kv[:, KV_WIDTH:].reshape(dec_batch, dec_seq, N_KV_HEADS, HEAD_DIM))
        gs.append(vn.reshape(dec_batch, dec_seq, GMLP_WIDTH))

    return (xp.reshape(batch, seq, D_MODEL), xs.reshape(dec_batch, dec_seq, D_MODEL),
            jnp.stack(kp), jnp.stack(vp), jnp.stack(ks), jnp.stack(vs), jnp.stack(gs))
```

```python
import functools
import math

import jax
import jax.numpy as jnp
import numpy as np
from jax import lax
from jax.experimental import pallas as pl
from jax.experimental.pallas import tpu as pltpu

D_MODEL = 1024
CHUNK = 64
N_Q_HEADS = 16
N_KV_HEADS = 2
HEAD_DIM = 64
Q_PER_KV = N_Q_HEADS // N_KV_HEADS
ATTN_WIDTH = N_Q_HEADS * HEAD_DIM
KV_WIDTH = N_KV_HEADS * HEAD_DIM
WINDOW = 128
GMLP_WIDTH = 1024
GMLP_GROUPS = 4
GMLP_GROUP_DIM = GMLP_WIDTH // GMLP_GROUPS
GMLP_CHUNK = 128
NUM_BUCKETS = 32
MAX_DISTANCE = 128
D_FF = 2816
EPS = 1e-6
NEG_INF = -1e30

OFF_Q = 0
OFF_KV = ATTN_WIDTH
OFF_U = OFF_KV + 2 * KV_WIDTH
OFF_GV = OFF_U + GMLP_WIDTH
OFF_GA = OFF_GV + GMLP_WIDTH
OFF_GB = OFF_GA + D_MODEL

V7X_VMEM_BYTES = 64 * 1024 * 1024
VMEM_LIMIT_BYTES = V7X_VMEM_BYTES - 8 * 1024 * 1024
MXU_TILE = 256
LANES = 128

QBLK = 2 * CHUNK
KBLK = WINDOW + QBLK
HEADS_PER_TILE = LANES // HEAD_DIM
N_HEAD_PAIRS = N_Q_HEADS // HEADS_PER_TILE
PAIRS_PER_KV = Q_PER_KV // HEADS_PER_TILE
PAIR_GROUPS = PAIRS_PER_KV // 2

BF16 = jnp.bfloat16
F32 = jnp.float32


def _dot(a, b):
    return jnp.dot(a, b, preferred_element_type=F32)


def _rms(x, g):
    return x * lax.rsqrt(jnp.mean(x * x, axis=-1, keepdims=True) + EPS) * g


def _resident(shape):
    zeros = (0,) * len(shape)
    return pl.BlockSpec(shape, lambda *_: zeros, pipeline_mode=pl.Buffered(1))


def _params(n_axes):
    return pltpu.CompilerParams(dimension_semantics=("arbitrary",) * n_axes,
                                vmem_limit_bytes=VMEM_LIMIT_BYTES)


def _ordering_zero(*arrays):
    m = None
    for a in arrays:
        r = jnp.max(jnp.max(a.astype(F32), axis=0, keepdims=True), axis=1, keepdims=True)
        m = r if m is None else jnp.maximum(m, r)
    bits = lax.bitcast_convert_type(m, jnp.uint32)
    return lax.bitcast_convert_type((bits >> 16) >> 16, F32)


FF_CHUNK = MXU_TILE
FFN_SIDE_PIECES = 8


def _ffn_body(n_tiles, emit_next, xp_ref, xe_ref, g_ref, wg_ref, wu_ref, wd_ref, *refs):
    y_ref = refs[0]
    h_ref, acc_ref = refs[-2:]
    s = pl.program_id(0)
    slot = s % 2
    tm = y_ref.shape[0]

    def pre_norm(dst, rows):
        h_ref[dst, rows, :] = _rms(xp_ref[rows, :], g_ref[0:1, :]).astype(BF16)
        return [h_ref[dst, rows, :]]

    def finish(rows):
        y = xe_ref[rows, :] + 0.5 * _rms(acc_ref[rows, :], g_ref[1:2, :])
        y_ref[rows, :] = y
        stored = [y_ref[rows, :]]
        if emit_next:
            refs[1][rows, :] = _rms(y, g_ref[2:3, :]).astype(BF16)
            stored.append(refs[1][rows, :])
        return stored

    @pl.when(s == 0)
    def _():
        pre_norm(0, slice(None))
        acc_ref[...] = jnp.zeros_like(acc_ref)

    @pl.when(jnp.logical_and(s >= 1, s <= n_tiles))
    def _():
        h = h_ref[1 - slot]
        acc = None
        anchor = None
        piece_rows = tm // FFN_SIDE_PIECES
        assert D_FF // FF_CHUNK > FFN_SIDE_PIECES
        for ci in range(D_FF // FF_CHUNK):
            cols = slice(ci * FF_CHUNK, (ci + 1) * FF_CHUNK)
            gate = _dot(h, wg_ref[:, cols])
            up = _dot(h, wu_ref[:, cols])
            if anchor is not None:
                up = up + anchor
            act = (jax.nn.silu(gate) * up).astype(BF16)
            part = _dot(act, wd_ref[cols, :])
            acc = part if acc is None else acc + part
            anchor = None
            if ci < FFN_SIDE_PIECES:
                rows = slice(ci * piece_rows, (ci + 1) * piece_rows)
                anchor = _ordering_zero(*(finish(rows) + pre_norm(slot, rows)))
        acc_ref[...] = acc

    @pl.when(s == n_tiles + 1)
    def _():
        finish(slice(None))


def _ffn(x, gains, wg, wu, wd, *, tm, emit_next, name):
    t = x.shape[0]
    assert t % tm == 0
    n_tiles = t // tm
    head = lambda s: (jnp.minimum(s, n_tiles - 1), 0)
    tail = lambda s: (jnp.clip(s - 2, 0, n_tiles - 1), 0)
    out_shape = [jax.ShapeDtypeStruct((t, D_MODEL), F32)]
    out_specs = [pl.BlockSpec((tm, D_MODEL), tail)]
    if emit_next:
        out_shape.append(jax.ShapeDtypeStruct((t, D_MODEL), BF16))
        out_specs.append(pl.BlockSpec((tm, D_MODEL), tail))
    return pl.pallas_call(
        functools.partial(_ffn_body, n_tiles, emit_next),
        grid=(n_tiles + 2,),
        in_specs=[pl.BlockSpec((tm, D_MODEL), head), pl.BlockSpec((tm, D_MODEL), tail),
                  _resident(gains.shape), _resident(wg.shape), _resident(wu.shape), _resident(wd.shape)],
        out_specs=out_specs,
        out_shape=out_shape,
        scratch_shapes=[pltpu.VMEM((2, tm, D_MODEL), BF16), pltpu.VMEM((tm, D_MODEL), F32)],
        compiler_params=_params(1),
        name=name,
    )(x, x, gains, wg, wu, wd)


def _gelu(x):
    return 0.5 * x * (1.0 + lax.erf(x * np.sqrt(0.5).astype(np.float32)))


def _layer_norm(x, g, b):
    mu = jnp.mean(x, axis=-1, keepdims=True)
    xc = x - mu
    var = jnp.mean(xc * xc, axis=-1, keepdims=True)
    return xc * lax.rsqrt(var + EPS) * g + b


def _in_proj_body(q_blocked, h_ref, w_ref, ln_ref, q_ref, kv_ref, u_ref, vn_ref, ga_ref, gb_ref):
    h = h_ref[...]
    gv = _gelu(_dot(h, w_ref[:, OFF_GV:OFF_GA]))
    vn_ref[...] = _layer_norm(gv, ln_ref[0:1, :], ln_ref[1:2, :]).astype(vn_ref.dtype)
    u_ref[...] = _gelu(_dot(h, w_ref[:, OFF_U:OFF_GV])).astype(u_ref.dtype)
    ga_ref[...] = jax.nn.sigmoid(_dot(h, w_ref[:, OFF_GA:OFF_GB])).astype(ga_ref.dtype)
    gb_ref[...] = jax.nn.sigmoid(_dot(h, w_ref[:, OFF_GB:])).astype(gb_ref.dtype)
    q = (_dot(h, w_ref[:, OFF_Q:OFF_KV]) * (HEAD_DIM ** -0.5)).astype(q_ref.dtype)
    if q_blocked:
        for b in range(q.shape[0] // QBLK):
            for p in range(N_HEAD_PAIRS):
                q_ref[b, p * QBLK:(p + 1) * QBLK, :] = q[b * QBLK:(b + 1) * QBLK, p * LANES:(p + 1) * LANES]
    else:
        q_ref[...] = q
    kv_ref[...] = _dot(h, w_ref[:, OFF_KV:OFF_U])


def _in_proj(h, w_in, ln, *, tm, vn_dtype, q_blocked, name):
    t = h.shape[0]
    assert t % tm == 0
    row = lambda i: (i, 0)
    wide = lambda dt: jax.ShapeDtypeStruct((t, D_MODEL), dt)
    wide_spec = pl.BlockSpec((tm, D_MODEL), row)
    if q_blocked:
        assert tm % QBLK == 0
        q_shape = jax.ShapeDtypeStruct((t // QBLK, N_HEAD_PAIRS * QBLK, LANES), BF16)
        q_spec = pl.BlockSpec((tm // QBLK, N_HEAD_PAIRS * QBLK, LANES), lambda i: (i, 0, 0))
    else:
        q_shape, q_spec = wide(BF16), wide_spec
    return pl.pallas_call(
        functools.partial(_in_proj_body, q_blocked),
        grid=(t // tm,),
        in_specs=[wide_spec, _resident(w_in.shape), _resident(ln.shape)],
        out_specs=[q_spec, pl.BlockSpec((tm, 2 * KV_WIDTH), row),
                   wide_spec, wide_spec, wide_spec, wide_spec],
        out_shape=[q_shape, jax.ShapeDtypeStruct((t, 2 * KV_WIDTH), F32),
                   wide(BF16), wide(vn_dtype), wide(BF16), wide(BF16)],
        compiler_params=_params(1),
        name=name,
    )(h, w_in, ln)


def _t5_bucket(rel):
    half = NUM_BUCKETS // 2
    max_exact = half // 2
    ret = jnp.where(rel > 0, half, 0)
    n = jnp.abs(rel)
    nf = jnp.maximum(n, 1).astype(F32)
    large = max_exact + (jnp.log(nf / max_exact) / math.log(MAX_DISTANCE / max_exact)
                         * (half - max_exact)).astype(jnp.int32)
    large = jnp.minimum(large, half - 1)
    return ret + jnp.where(n < max_exact, n, large)


def _bias_body(table_ref, bucket_ref, out_ref):
    bucket = bucket_ref[...]
    hits = [bucket == b for b in range(NUM_BUCKETS)]
    for head in range(N_Q_HEADS):
        acc = jnp.zeros(bucket.shape, F32)
        for b in range(NUM_BUCKETS):
            acc = jnp.where(hits[b], table_ref[b, head], acc)
        out_ref[head] = acc


def _prompt_bias_body(table_ref, bucket_ref, out_ref):
    bucket = bucket_ref[...]
    hits = [bucket == b for b in range(NUM_BUCKETS)]
    key_chunk = lax.broadcasted_iota(jnp.int32, bucket.shape, 0) // CHUNK
    q_chunk = lax.broadcasted_iota(jnp.int32, bucket.shape, 1) // CHUNK
    in_window = jnp.logical_and(key_chunk >= q_chunk, key_chunk <= q_chunk + WINDOW // CHUNK)
    valid = [in_window, jnp.logical_and(in_window, key_chunk >= WINDOW // CHUNK)]
    for head in range(N_Q_HEADS):
        acc = jnp.zeros(bucket.shape, F32)
        for b in range(NUM_BUCKETS):
            acc = jnp.where(hits[b], table_ref[b, head], acc)
        h, rest = divmod(head, Q_PER_KV)
        pair, par = divmod(rest, HEADS_PER_TILE)
        pg, pl_ = divmod(pair, 2)
        for v in range(2):
            out_ref[v, h, pg, par * KBLK:(par + 1) * KBLK, pl_ * QBLK:(pl_ + 1) * QBLK] = (
                jnp.where(valid[v], acc, NEG_INF))


def _prompt_bias(table):
    rel = (jnp.arange(KBLK) - WINDOW)[:, None] - jnp.arange(QBLK)[None, :]
    bucket = _t5_bucket(rel).astype(jnp.int32)
    return pl.pallas_call(
        _prompt_bias_body,
        in_specs=[pl.BlockSpec(memory_space=pltpu.SMEM),
                  pl.BlockSpec(memory_space=pltpu.VMEM)],
        out_specs=pl.BlockSpec(memory_space=pltpu.VMEM),
        out_shape=jax.ShapeDtypeStruct((2, N_KV_HEADS, PAIR_GROUPS, 2 * KBLK, 2 * QBLK), F32),
        name="rel_bias_prompt",
    )(table, bucket)


def _relative_bias(table, n_q, n_keys, n_past):
    rel = (jnp.arange(n_keys) - n_past)[None, :] - jnp.arange(n_q)[:, None]
    bucket = _t5_bucket(rel).astype(jnp.int32)
    bias = pl.pallas_call(
        _bias_body,
        in_specs=[pl.BlockSpec(memory_space=pltpu.SMEM),
                  pl.BlockSpec(memory_space=pltpu.VMEM)],
        out_specs=pl.BlockSpec(memory_space=pltpu.VMEM),
        out_shape=jax.ShapeDtypeStruct((N_Q_HEADS, n_q, n_keys), F32),
        name="rel_bias_%d" % n_q,
    )(table, bucket)
    return bias.reshape(N_KV_HEADS, Q_PER_KV * n_q, n_keys)


def _sink_attention(qh, kh, vh, bias, sink, invalid=None):
    s = lax.dot_general(qh, kh, (((1,), (1,)), ((), ())), preferred_element_type=F32) + bias
    if invalid is not None:
        s = jnp.where(invalid, NEG_INF, s)
    m = jnp.maximum(jnp.max(s, axis=-1, keepdims=True), sink)
    p = jnp.exp(s - m)
    denom = jnp.sum(p, axis=-1, keepdims=True) + jnp.exp(sink - m)
    return _dot(p.astype(BF16), vh) / denom


def _stack_heads(q, kv_head):
    base = kv_head * Q_PER_KV * HEAD_DIM
    return jnp.concatenate(
        [q[:, base + g * HEAD_DIM: base + (g + 1) * HEAD_DIM] for g in range(Q_PER_KV)], axis=0)


def _unstack_heads(o, n):
    return jnp.concatenate([o[g * n:(g + 1) * n, :] for g in range(Q_PER_KV)], axis=1)


def _merge_out(x1, attn, gm, ga, gb, wba_ref, wbg_ref, wo_ref, g_post):
    merged = ga * _dot(attn, wba_ref[...]) + gb * _dot(gm, wbg_ref[...])
    return x1 + _rms(_dot(merged.astype(BF16), wo_ref[...]), g_post)


def _prompt_mixer_body(ts, q_ref, kv_ref, kvp_ref, u_ref, vn_ref, ga_ref, gb_ref, x1_ref,
                       bias_ref, sink_ref, ws_ref, bs_ref, wba_ref, wbg_ref, wo_ref, g_ref,
                       out_ref, attn_t_ref, gm_ref):
    first_variant = jnp.where(pl.program_id(1) == 0, 1, 0)
    kv_all = jnp.concatenate([kvp_ref[0], kv_ref[0]], axis=0)
    k_all = kv_all[:, :KV_WIDTH]
    v_t = jnp.transpose(kv_all[:, KV_WIDTH:]).astype(BF16)
    low = lax.broadcasted_iota(jnp.int32, k_all.shape, 1) < HEAD_DIM
    k_swapped = pltpu.roll(k_all, HEAD_DIM, axis=1)
    zero = jnp.zeros_like(k_all)
    k_par = [[jnp.where(low, k_all, zero), jnp.where(low, zero, k_swapped)],
             [jnp.where(low, k_swapped, zero), jnp.where(low, zero, k_all)]]
    k_par = [[k.astype(BF16) for k in ks] for ks in k_par]

    units = [(blk, h, pg) for blk in range(ts // QBLK) for h in range(N_KV_HEADS)
             for pg in range(PAIR_GROUPS)]

    def scores(blk, h, pg):
        keys = slice(blk * QBLK, blk * QBLK + KBLK)
        k_blk = jnp.concatenate([k_par[h][0][keys], k_par[h][1][keys]], axis=0)
        pair0 = h * PAIRS_PER_KV + pg * 2
        qa = q_ref[blk, pair0 * QBLK:(pair0 + 2) * QBLK, :]
        s = lax.dot_general(k_blk, qa, (((1,), (1,)), ((), ())), preferred_element_type=F32)
        variant = first_variant if blk == 0 else 0
        return s + bias_ref[variant, h, pg]

    def softmax(s, blk, h, pg):
        out = []
        for par in range(HEADS_PER_TILE):
            sp = s[par * KBLK:(par + 1) * KBLK]
            row = (h * PAIR_GROUPS + pg) * HEADS_PER_TILE + par
            sink = sink_ref[row:row + 1, :]
            m = jnp.maximum(jnp.max(sp, axis=0, keepdims=True), sink)
            p = jnp.exp(sp - m)
            denom = jnp.sum(p, axis=0, keepdims=True) + jnp.exp(sink - m)
            out.append((p.astype(BF16), 1.0 / denom))
        return out

    def weighted_values(probs, blk, h, pg):
        keys = slice(blk * QBLK, blk * QBLK + KBLK)
        cols = slice(blk * QBLK, (blk + 1) * QBLK)
        vh_t = v_t[h * HEAD_DIM:(h + 1) * HEAD_DIM, keys]
        pair0 = h * PAIRS_PER_KV + pg * 2
        for par, (p, inv) in enumerate(probs):
            o = _dot(vh_t, p) * inv
            for pl_ in range(2):
                head = (pair0 + pl_) * HEADS_PER_TILE + par
                attn_t_ref[head * HEAD_DIM:(head + 1) * HEAD_DIM, cols] = o[:, pl_ * QBLK:(pl_ + 1) * QBLK]

    blk_i = lax.broadcasted_iota(jnp.int32, (GMLP_CHUNK, GMLP_CHUNK), 0) // CHUNK
    blk_j = lax.broadcasted_iota(jnp.int32, (GMLP_CHUNK, GMLP_CHUNK), 1) // CHUNK
    for g in range(GMLP_GROUPS):
        w = jnp.where(blk_j <= blk_i, ws_ref[g], 0.0).astype(BF16)
        b = bs_ref[:, g:g + 1]
        cols = slice(g * GMLP_GROUP_DIM, (g + 1) * GMLP_GROUP_DIM)
        for c in range(ts // GMLP_CHUNK):
            rows = slice(c * GMLP_CHUNK, (c + 1) * GMLP_CHUNK)
            sp = _dot(w, vn_ref[0, rows, cols]) + b
            gm_ref[rows, cols] = (u_ref[0, rows, cols].astype(F32) * sp).astype(BF16)

    def gmlp_branch(c):
        cols = slice(c * MXU_TILE, (c + 1) * MXU_TILE)
        return gb_ref[0, :, cols].astype(F32) * _dot(gm_ref[...], wbg_ref[:, cols])

    n_side = D_MODEL // MXU_TILE
    side = []
    s_vals, p_vals = {}, {}
    for i in range(len(units) + 2):
        if i < len(units):
            s_vals[i] = scores(*units[i])
        if 1 <= i <= len(units):
            p_vals[i - 1] = softmax(s_vals.pop(i - 1), *units[i - 1])
        if i >= 2:
            weighted_values(p_vals.pop(i - 2), *units[i - 2])
        if i % (len(units) // n_side) == 1 and len(side) < n_side:
            side.append(gmlp_branch(len(side)))
    assert len(side) == n_side

    attn = jnp.transpose(attn_t_ref[...]).astype(BF16)
    merged = ga_ref[0].astype(F32) * _dot(attn, wba_ref[...]) + jnp.concatenate(side, axis=1)
    out_ref[0] = x1_ref[0] + _rms(_dot(merged.astype(BF16), wo_ref[...]), g_ref[...])


def _prompt_mixer(q, kv, u, vn, ga, gb, x1, bias, sink, w_s, b_s_t, wba, wbg, wo, g_post, *, ts):
    batch, seq, _ = kv.shape
    assert seq % ts == 0 and ts % QBLK == 0 and QBLK == GMLP_CHUNK == WINDOW
    tile = lambda b, t: (b, t, 0)
    prev = lambda b, t: (b, jnp.maximum(t * (ts // WINDOW) - 1, 0), 0)
    wide = pl.BlockSpec((1, ts, D_MODEL), tile)
    n_t = seq // ts
    q_spec = pl.BlockSpec((ts // QBLK, N_HEAD_PAIRS * QBLK, LANES), lambda b, t: (b * n_t + t, 0, 0))
    return pl.pallas_call(
        functools.partial(_prompt_mixer_body, ts),
        grid=(batch, n_t),
        in_specs=[q_spec, pl.BlockSpec((1, ts, 2 * KV_WIDTH), tile),
                  pl.BlockSpec((1, WINDOW, 2 * KV_WIDTH), prev),
                  wide, wide, wide, wide, wide,
                  _resident(bias.shape), _resident(sink.shape), _resident(w_s.shape),
                  _resident(b_s_t.shape), _resident(wba.shape), _resident(wbg.shape),
                  _resident(wo.shape), _resident(g_post.shape)],
        out_specs=wide,
        out_shape=jax.ShapeDtypeStruct((batch, seq, D_MODEL), F32),
        scratch_shapes=[pltpu.VMEM((ATTN_WIDTH, ts), F32), pltpu.VMEM((ts, GMLP_WIDTH), BF16)],
        compiler_params=_params(2),
        name="prompt_mixer",
    )(q, kv, kv, u, vn, ga, gb, x1, bias, sink, w_s, b_s_t, wba, wbg, wo, g_post)


def _sample_mixer_body(n_batch, n_new, q_ref, kv_ref, ck_ref, cv_ref, u_ref, vn_ref, ga_ref, gb_ref,
                       x1_ref, bias_ref, sink_ref, ws_ref, bs_ref, wba_ref, wbg_ref, wo_ref, g_ref,
                       out_ref, attn_ref, gm_ref):
    for b in range(n_batch):
        rows = slice(b * n_new, (b + 1) * n_new)
        q = q_ref[rows, :]
        kv = kv_ref[rows, :]
        k_all = jnp.concatenate([ck_ref[b], kv[:, :KV_WIDTH]], axis=0).astype(BF16)
        v_all = jnp.concatenate([cv_ref[b], kv[:, KV_WIDTH:]], axis=0).astype(BF16)
        for h in range(N_KV_HEADS):
            cols = slice(h * HEAD_DIM, (h + 1) * HEAD_DIM)
            o = _sink_attention(_stack_heads(q, h), k_all[:, cols], v_all[:, cols],
                                bias_ref[h], sink_ref[h])
            width = Q_PER_KV * HEAD_DIM
            attn_ref[rows, h * width:(h + 1) * width] = _unstack_heads(o, n_new).astype(BF16)
        for g in range(GMLP_GROUPS):
            cols = slice(g * GMLP_GROUP_DIM, (g + 1) * GMLP_GROUP_DIM)
            w = ws_ref[g, :n_new, :n_new].astype(BF16)
            sp = _dot(w, vn_ref[rows, cols].astype(BF16)) + bs_ref[:n_new, g:g + 1]
            gm_ref[rows, cols] = (u_ref[rows, cols].astype(F32) * sp).astype(BF16)

    out_ref[...] = _merge_out(x1_ref[...], attn_ref[...], gm_ref[...],
                              ga_ref[...].astype(F32), gb_ref[...].astype(F32),
                              wba_ref, wbg_ref, wo_ref, g_ref[...])


def _sample_mixer(q, kv, cache_k, cache_v, u, vn, ga, gb, x1, bias, sink, w_s, b_s_t,
                  wba, wbg, wo, g_post, *, n_batch, n_new):
    assert n_new <= CHUNK
    t = q.shape[0]
    vmem = pl.BlockSpec(memory_space=pltpu.VMEM)
    return pl.pallas_call(
        functools.partial(_sample_mixer_body, n_batch, n_new),
        in_specs=[vmem] * 17,
        out_specs=vmem,
        out_shape=jax.ShapeDtypeStruct((t, D_MODEL), F32),
        scratch_shapes=[pltpu.VMEM((t, ATTN_WIDTH), BF16), pltpu.VMEM((t, GMLP_WIDTH), BF16)],
        compiler_params=pltpu.CompilerParams(vmem_limit_bytes=VMEM_LIMIT_BYTES),
        name="sample_mixer",
    )(q, kv, cache_k, cache_v, u, vn, ga, gb, x1, bias, sink, w_s, b_s_t, wba, wbg, wo, g_post)


def _prompt_sink_rows(sinks):
    s = sinks.astype(F32).reshape(N_KV_HEADS, PAIR_GROUPS, 2, HEADS_PER_TILE)
    s = jnp.transpose(s, (0, 1, 3, 2))[..., None]
    s = jnp.broadcast_to(s, (N_KV_HEADS, PAIR_GROUPS, HEADS_PER_TILE, 2, QBLK))
    return s.reshape(N_KV_HEADS * PAIR_GROUPS * HEADS_PER_TILE, 2 * QBLK)


def _sink_rows(sinks, n_q):
    s = jnp.broadcast_to(sinks.astype(F32).reshape(N_KV_HEADS, Q_PER_KV, 1), (N_KV_HEADS, Q_PER_KV, n_q))
    return s.reshape(N_KV_HEADS, Q_PER_KV * n_q, 1)


PROMPT_TM = 512
PROMPT_TS = 512


def kernel(x_prompt, x_sample, cache_win_k, cache_win_v, rel_bias_table, norm_gains, ffn1_w_gate, ffn1_w_up, ffn1_w_down, w_in, attn_sinks, gmlp_ln_g, gmlp_ln_b, gmlp_w_s, gmlp_b_s, w_branch_attn, w_branch_gmlp, w_out, ffn2_w_gate, ffn2_w_up, ffn2_w_down):
    depth = norm_gains.shape[0]
    batch, seq, _ = x_prompt.shape
    dec_batch, dec_seq, _ = x_sample.shape
    n_cache = cache_win_k.shape[2]
    assert seq % PROMPT_TS == 0 and dec_seq <= CHUNK

    bias_p = _prompt_bias(rel_bias_table)
    bias_s = _relative_bias(rel_bias_table, dec_seq, n_cache + dec_seq, n_cache)

    xp = x_prompt.reshape(batch * seq, D_MODEL)
    xs = x_sample.reshape(dec_batch * dec_seq, D_MODEL)
    t_s = xs.shape[0]
    kp, vp, ks, vs, gs = [], [], [], [], []
    for l in range(depth):
        g = norm_gains[l].astype(F32)
        w1 = [w[l].astype(BF16) for w in (ffn1_w_gate, ffn1_w_up, ffn1_w_down)]
        w2 = [w[l].astype(BF16) for w in (ffn2_w_gate, ffn2_w_up, ffn2_w_down)]
        win = w_in[l].astype(BF16)
        wba, wbg, wo = (w[l].astype(BF16) for w in (w_branch_attn, w_branch_gmlp, w_out))
        ln = jnp.stack([gmlp_ln_g[l], gmlp_ln_b[l]]).astype(F32)
        w_s = gmlp_w_s[l].astype(F32)
        b_s_t = jnp.transpose(gmlp_b_s[l]).astype(F32)
        g_ffn1, g_post2, g_ffn2 = g[0:3], g[3:4], jnp.concatenate([g[4:6], g[5:6]])
        sink_p = _prompt_sink_rows(attn_sinks[l])
        sink_s = _sink_rows(attn_sinks[l], dec_seq)

        x1, h2 = _ffn(xp, g_ffn1, *w1, tm=PROMPT_TM, emit_next=True, name="ffn1_prompt")
        q, kv, u, vn, ga, gb = _in_proj(h2, win, ln, tm=PROMPT_TM, vn_dtype=BF16, q_blocked=True,
                                        name="in_proj_prompt")
        b3 = lambda a: a.reshape(batch, seq, a.shape[-1])
        x2 = _prompt_mixer(q, b3(kv), b3(u), b3(vn), b3(ga), b3(gb), b3(x1), bias_p, sink_p,
                           w_s, b_s_t, wba, wbg, wo, g_post2, ts=PROMPT_TS)
        (xp,) = _ffn(x2.reshape(batch * seq, D_MODEL), g_ffn2, *w2, tm=PROMPT_TM, emit_next=False,
                     name="ffn2_prompt")
        kv_win = b3(kv)[:, seq - WINDOW:, :]
        kp.append(kv_win[..., :KV_WIDTH].reshape(batch, WINDOW, N_KV_HEADS, HEAD_DIM))
        vp.append(kv_win[..., KV_WIDTH:].reshape(batch, WINDOW, N_KV_HEADS, HEAD_DIM))

        x1, h2 = _ffn(xs, g_ffn1, *w1, tm=t_s, emit_next=True, name="ffn1_sample")
        q, kv, u, vn, ga, gb = _in_proj(h2, win, ln, tm=t_s, vn_dtype=F32, q_blocked=False,
                                        name="in_proj_sample")
        ck = cache_win_k[l].reshape(dec_batch, n_cache, KV_WIDTH)
        cv = cache_win_v[l].reshape(dec_batch, n_cache, KV_WIDTH)
        x2 = _sample_mixer(q, kv, ck, cv, u, vn, ga, gb, x1, bias_s, sink_s, w_s, b_s_t,
                           wba, wbg, wo, g_post2, n_batch=dec_batch, n_new=dec_seq)
        (xs,) = _ffn(x2, g_ffn2, *w2, tm=t_s, emit_next=False, name="ffn2_sample")
        ks.append(kv[:, :KV_WIDTH].reshape(dec_batch, dec_seq, N_KV_HEADS, HEAD_DIM))
        vs.append(kv[:, KV_WIDTH:].reshape(dec_batch, dec_seq, N_KV_HEADS, HEAD_DIM))
        gs.append(vn.reshape(dec_batch, dec_seq, GMLP_WIDTH))

    return (xp.reshape(batch, seq, D_MODEL), xs.reshape(dec_batch, dec_seq, D_MODEL),
            jnp.stack(kp), jnp.stack(vp), jnp.stack(ks), jnp.stack(vs), jnp.stack(gs))
```

```python
import functools
import math

import jax
import jax.numpy as jnp
import numpy as np
from jax import lax
from jax.experimental import pallas as pl
from jax.experimental.pallas import tpu as pltpu

D_MODEL = 1024
CHUNK = 64
N_Q_HEADS = 16
N_KV_HEADS = 2
HEAD_DIM = 64
Q_PER_KV = N_Q_HEADS // N_KV_HEADS
ATTN_WIDTH = N_Q_HEADS * HEAD_DIM
KV_WIDTH = N_KV_HEADS * HEAD_DIM
WINDOW = 128
GMLP_WIDTH = 1024
GMLP_GROUPS = 4
GMLP_GROUP_DIM = GMLP_WIDTH // GMLP_GROUPS
GMLP_CHUNK = 128
NUM_BUCKETS = 32
MAX_DISTANCE = 128
D_FF = 2816
EPS = 1e-6
NEG_INF = -1e30
LOG2E = math.log2(math.e)

OFF_Q = 0
OFF_KV = ATTN_WIDTH
OFF_U = OFF_KV + 2 * KV_WIDTH
OFF_GV = OFF_U + GMLP_WIDTH
OFF_GA = OFF_GV + GMLP_WIDTH
OFF_GB = OFF_GA + D_MODEL

V7X_VMEM_BYTES = 64 * 1024 * 1024
VMEM_LIMIT_BYTES = V7X_VMEM_BYTES - 8 * 1024 * 1024
MXU_TILE = 256
LANES = 128

QBLK = 2 * CHUNK
KBLK = WINDOW + QBLK
HEADS_PER_TILE = LANES // HEAD_DIM
N_HEAD_PAIRS = N_Q_HEADS // HEADS_PER_TILE
PAIRS_PER_KV = Q_PER_KV // HEADS_PER_TILE
PAIR_GROUPS = PAIRS_PER_KV // 2

BF16 = jnp.bfloat16
F32 = jnp.float32


def _dot(a, b):
    return jnp.dot(a, b, preferred_element_type=F32)


def _rms(x, g):
    return x * lax.rsqrt(jnp.mean(x * x, axis=-1, keepdims=True) + EPS) * g


def _resident(shape):
    zeros = (0,) * len(shape)
    return pl.BlockSpec(shape, lambda *_: zeros, pipeline_mode=pl.Buffered(1))


def _params(n_axes):
    return pltpu.CompilerParams(dimension_semantics=("arbitrary",) * n_axes,
                                vmem_limit_bytes=VMEM_LIMIT_BYTES)


CAST_BLOCK_BYTES = 4 * 1024 * 1024


def _cast_body(w_ref, o_ref):
    o_ref[...] = w_ref[...].astype(o_ref.dtype)


def _to_bf16(w):
    k, n = w.shape
    rows = [r for r in range(16, k + 1, 16) if k % r == 0 and r * n * 4 <= CAST_BLOCK_BYTES]
    bk = max(rows) if rows else k
    return pl.pallas_call(
        _cast_body,
        grid=(k // bk,),
        in_specs=[pl.BlockSpec((bk, n), lambda i: (i, 0))],
        out_specs=pl.BlockSpec((bk, n), lambda i: (i, 0)),
        out_shape=jax.ShapeDtypeStruct((k, n), BF16),
        compiler_params=_params(1),
        name="weight_to_bf16",
    )(w)


def _ordering_zero(*arrays):
    m = None
    for a in arrays:
        r = jnp.max(jnp.max(a.astype(F32), axis=0, keepdims=True), axis=1, keepdims=True)
        m = r if m is None else jnp.maximum(m, r)
    bits = lax.bitcast_convert_type(m, jnp.uint32)
    return lax.bitcast_convert_type((bits >> 16) >> 16, F32)


FF_CHUNK = MXU_TILE
FFN_SIDE_PIECES = 8


def _ffn_body(n_tiles, emit_next, xp_ref, xe_ref, g_ref, wg_ref, wu_ref, wd_ref, *refs):
    y_ref = refs[0]
    h_ref, acc_ref = refs[-2:]
    s = pl.program_id(0)
    slot = s % 2
    tm = y_ref.shape[0]

    def pre_norm(dst, rows):
        h_ref[dst, rows, :] = _rms(xp_ref[rows, :], g_ref[0:1, :]).astype(BF16)
        return [h_ref[dst, rows, :]]

    def finish(rows):
        y = xe_ref[rows, :] + 0.5 * _rms(acc_ref[rows, :], g_ref[1:2, :])
        y_ref[rows, :] = y
        stored = [y_ref[rows, :]]
        if emit_next:
            refs[1][rows, :] = _rms(y, g_ref[2:3, :]).astype(BF16)
            stored.append(refs[1][rows, :])
        return stored

    @pl.when(s == 0)
    def _():
        pre_norm(0, slice(None))
        acc_ref[...] = jnp.zeros_like(acc_ref)

    @pl.when(jnp.logical_and(s >= 1, s <= n_tiles))
    def _():
        h = h_ref[1 - slot]
        acc = None
        anchor = None
        piece_rows = tm // FFN_SIDE_PIECES
        assert D_FF // FF_CHUNK > FFN_SIDE_PIECES
        for ci in range(D_FF // FF_CHUNK):
            cols = slice(ci * FF_CHUNK, (ci + 1) * FF_CHUNK)
            gate = _dot(h, wg_ref[:, cols])
            up = _dot(h, wu_ref[:, cols])
            if anchor is not None:
                up = up + anchor
            act = (jax.nn.silu(gate) * up).astype(BF16)
            part = _dot(act, wd_ref[cols, :])
            acc = part if acc is None else acc + part
            anchor = None
            if ci < FFN_SIDE_PIECES:
                rows = slice(ci * piece_rows, (ci + 1) * piece_rows)
                anchor = _ordering_zero(*(finish(rows) + pre_norm(slot, rows)))
        acc_ref[...] = acc

    @pl.when(s == n_tiles + 1)
    def _():
        finish(slice(None))


def _ffn(x, gains, wg, wu, wd, *, tm, emit_next, name):
    t = x.shape[0]
    assert t % tm == 0
    n_tiles = t // tm
    head = lambda s: (jnp.minimum(s, n_tiles - 1), 0)
    tail = lambda s: (jnp.clip(s - 2, 0, n_tiles - 1), 0)
    out_shape = [jax.ShapeDtypeStruct((t, D_MODEL), F32)]
    out_specs = [pl.BlockSpec((tm, D_MODEL), tail)]
    if emit_next:
        out_shape.append(jax.ShapeDtypeStruct((t, D_MODEL), BF16))
        out_specs.append(pl.BlockSpec((tm, D_MODEL), tail))
    return pl.pallas_call(
        functools.partial(_ffn_body, n_tiles, emit_next),
        grid=(n_tiles + 2,),
        in_specs=[pl.BlockSpec((tm, D_MODEL), head), pl.BlockSpec((tm, D_MODEL), tail),
                  _resident(gains.shape), _resident(wg.shape), _resident(wu.shape), _resident(wd.shape)],
        out_specs=out_specs,
        out_shape=out_shape,
        scratch_shapes=[pltpu.VMEM((2, tm, D_MODEL), BF16), pltpu.VMEM((tm, D_MODEL), F32)],
        compiler_params=_params(1),
        name=name,
    )(x, x, gains, wg, wu, wd)


def _gelu(x):
    return 0.5 * x * (1.0 + lax.erf(x * np.sqrt(0.5).astype(np.float32)))


def _layer_norm(x, g, b):
    mu = jnp.mean(x, axis=-1, keepdims=True)
    xc = x - mu
    var = jnp.mean(xc * xc, axis=-1, keepdims=True)
    return xc * lax.rsqrt(var + EPS) * g + b


IN_PROJ_LN_PIECES = 8


def _in_proj_body(q_blocked, h_ref, w_ref, ln_ref, q_ref, kv_ref, u_ref, vn_ref, ga_ref, gb_ref, gv_ref):
    h = h_ref[...]
    tm = h.shape[0]
    n_chunks = D_MODEL // MXU_TILE
    q_scale = HEAD_DIM ** -0.5 * (LOG2E if q_blocked else 1.0)

    def store_q(c, r):
        q = (r * q_scale).astype(q_ref.dtype)
        if q_blocked:
            for b in range(tm // QBLK):
                for j in range(MXU_TILE // LANES):
                    p = c * (MXU_TILE // LANES) + j
                    q_ref[b, p * QBLK:(p + 1) * QBLK, :] = q[b * QBLK:(b + 1) * QBLK, j * LANES:(j + 1) * LANES]
        else:
            q_ref[:, c * MXU_TILE:(c + 1) * MXU_TILE] = q

    def store_cols(ref, fn):
        def store(c, r):
            ref[:, c * MXU_TILE:(c + 1) * MXU_TILE] = fn(r).astype(ref.dtype)
        return store

    def layer_norm_piece(k):
        rows = slice(k * (tm // IN_PROJ_LN_PIECES), (k + 1) * (tm // IN_PROJ_LN_PIECES))
        vn_ref[rows, :] = _layer_norm(_gelu(gv_ref[rows, :]), ln_ref[0:1, :], ln_ref[1:2, :]).astype(vn_ref.dtype)

    def run(off, c, store):
        store(c, _dot(h, w_ref[:, off + c * MXU_TILE: off + (c + 1) * MXU_TILE]))

    store_gv = lambda c, r: gv_ref.__setitem__((slice(None), slice(c * MXU_TILE, (c + 1) * MXU_TILE)), r)
    store_u, store_ga, store_gb = (store_cols(u_ref, _gelu), store_cols(ga_ref, jax.nn.sigmoid),
                                   store_cols(gb_ref, jax.nn.sigmoid))
    for c in range(n_chunks):
        run(OFF_GV, c, store_gv)
    run(OFF_KV, 0, lambda c, r: kv_ref.__setitem__(Ellipsis, r))
    for c in range(n_chunks):
        run(OFF_Q, c, store_q)
        layer_norm_piece(2 * c)
        run(OFF_GA, c, store_ga)
        layer_norm_piece(2 * c + 1)
    for c in range(n_chunks):
        run(OFF_U, c, store_u)
        run(OFF_GB, c, store_gb)


def _in_proj(h, w_in, ln, *, tm, vn_dtype, q_blocked, name):
    t = h.shape[0]
    assert t % tm == 0
    row = lambda i: (i, 0)
    wide = lambda dt: jax.ShapeDtypeStruct((t, D_MODEL), dt)
    wide_spec = pl.BlockSpec((tm, D_MODEL), row)
    if q_blocked:
        assert tm % QBLK == 0
        q_shape = jax.ShapeDtypeStruct((t // QBLK, N_HEAD_PAIRS * QBLK, LANES), BF16)
        q_spec = pl.BlockSpec((tm // QBLK, N_HEAD_PAIRS * QBLK, LANES), lambda i: (i, 0, 0))
    else:
        q_shape, q_spec = wide(BF16), wide_spec
    return pl.pallas_call(
        functools.partial(_in_proj_body, q_blocked),
        grid=(t // tm,),
        in_specs=[wide_spec, _resident(w_in.shape), _resident(ln.shape)],
        out_specs=[q_spec, pl.BlockSpec((tm, 2 * KV_WIDTH), row),
                   wide_spec, wide_spec, wide_spec, wide_spec],
        out_shape=[q_shape, jax.ShapeDtypeStruct((t, 2 * KV_WIDTH), F32),
                   wide(BF16), wide(vn_dtype), wide(BF16), wide(BF16)],
        scratch_shapes=[pltpu.VMEM((tm, GMLP_WIDTH), F32)],
        compiler_params=_params(1),
        name=name,
    )(h, w_in, ln)


def _t5_bucket(rel):
    half = NUM_BUCKETS // 2
    max_exact = half // 2
    ret = jnp.where(rel > 0, half, 0)
    n = jnp.abs(rel)
    nf = jnp.maximum(n, 1).astype(F32)
    large = max_exact + (jnp.log(nf / max_exact) / math.log(MAX_DISTANCE / max_exact)
                         * (half - max_exact)).astype(jnp.int32)
    large = jnp.minimum(large, half - 1)
    return ret + jnp.where(n < max_exact, n, large)


def _bias_body(table_ref, bucket_ref, out_ref):
    bucket = bucket_ref[...]
    hits = [bucket == b for b in range(NUM_BUCKETS)]
    for head in range(N_Q_HEADS):
        acc = jnp.zeros(bucket.shape, F32)
        for b in range(NUM_BUCKETS):
            acc = jnp.where(hits[b], table_ref[b, head], acc)
        out_ref[head] = acc


def _prompt_bias_body(table_ref, bucket_ref, out_ref):
    bucket = bucket_ref[...]
    hits = [bucket == b for b in range(NUM_BUCKETS)]
    key_chunk = lax.broadcasted_iota(jnp.int32, bucket.shape, 0) // CHUNK
    q_chunk = lax.broadcasted_iota(jnp.int32, bucket.shape, 1) // CHUNK
    in_window = jnp.logical_and(key_chunk >= q_chunk, key_chunk <= q_chunk + WINDOW // CHUNK)
    valid = [in_window, jnp.logical_and(in_window, key_chunk >= WINDOW // CHUNK)]
    for head in range(N_Q_HEADS):
        acc = jnp.zeros(bucket.shape, F32)
        for b in range(NUM_BUCKETS):
            acc = jnp.where(hits[b], table_ref[b, head], acc)
        h, rest = divmod(head, Q_PER_KV)
        pair, par = divmod(rest, HEADS_PER_TILE)
        pg, pl_ = divmod(pair, 2)
        for v in range(2):
            out_ref[v, h, pg, par * KBLK:(par + 1) * KBLK, pl_ * QBLK:(pl_ + 1) * QBLK] = (
                jnp.where(valid[v], acc * LOG2E, NEG_INF))


def _prompt_bias(table):
    rel = (jnp.arange(KBLK) - WINDOW)[:, None] - jnp.arange(QBLK)[None, :]
    bucket = _t5_bucket(rel).astype(jnp.int32)
    return pl.pallas_call(
        _prompt_bias_body,
        in_specs=[pl.BlockSpec(memory_space=pltpu.SMEM),
                  pl.BlockSpec(memory_space=pltpu.VMEM)],
        out_specs=pl.BlockSpec(memory_space=pltpu.VMEM),
        out_shape=jax.ShapeDtypeStruct((2, N_KV_HEADS, PAIR_GROUPS, 2 * KBLK, 2 * QBLK), F32),
        name="rel_bias_prompt",
    )(table, bucket)


def _relative_bias(table, n_q, n_keys, n_past):
    rel = (jnp.arange(n_keys) - n_past)[None, :] - jnp.arange(n_q)[:, None]
    bucket = _t5_bucket(rel).astype(jnp.int32)
    bias = pl.pallas_call(
        _bias_body,
        in_specs=[pl.BlockSpec(memory_space=pltpu.SMEM),
                  pl.BlockSpec(memory_space=pltpu.VMEM)],
        out_specs=pl.BlockSpec(memory_space=pltpu.VMEM),
        out_shape=jax.ShapeDtypeStruct((N_Q_HEADS, n_q, n_keys), F32),
        name="rel_bias_%d" % n_q,
    )(table, bucket)
    return bias.reshape(N_KV_HEADS, Q_PER_KV * n_q, n_keys)


def _sink_attention(qh, kh, vh, bias, sink, invalid=None):
    s = lax.dot_general(qh, kh, (((1,), (1,)), ((), ())), preferred_element_type=F32) + bias
    if invalid is not None:
        s = jnp.where(invalid, NEG_INF, s)
    m = jnp.maximum(jnp.max(s, axis=-1, keepdims=True), sink)
    p = jnp.exp(s - m)
    denom = jnp.sum(p, axis=-1, keepdims=True) + jnp.exp(sink - m)
    return _dot(p.astype(BF16), vh) / denom


def _stack_heads(q, kv_head):
    base = kv_head * Q_PER_KV * HEAD_DIM
    return jnp.concatenate(
        [q[:, base + g * HEAD_DIM: base + (g + 1) * HEAD_DIM] for g in range(Q_PER_KV)], axis=0)


def _unstack_heads(o, n):
    return jnp.concatenate([o[g * n:(g + 1) * n, :] for g in range(Q_PER_KV)], axis=1)


def _merge_out(x1, attn, gm, ga, gb, wba_ref, wbg_ref, wo_ref, g_post):
    merged = ga * _dot(attn, wba_ref[...]) + gb * _dot(gm, wbg_ref[...])
    return x1 + _rms(_dot(merged.astype(BF16), wo_ref[...]), g_post)


def _prompt_mixer_body(ts, q_ref, kv_ref, kvp_ref, u_ref, vn_ref, ga_ref, gb_ref, x1_ref,
                       bias_ref, sink_ref, ws_ref, bs_ref, wba_ref, wbg_ref, wo_ref, g_ref,
                       out_ref, attn_t_ref, gm_ref):
    first_variant = jnp.where(pl.program_id(1) == 0, 1, 0)
    kv_all = jnp.concatenate([kvp_ref[0], kv_ref[0]], axis=0)
    k_all = kv_all[:, :KV_WIDTH]
    v_t = jnp.transpose(kv_all[:, KV_WIDTH:]).astype(BF16)
    low = lax.broadcasted_iota(jnp.int32, k_all.shape, 1) < HEAD_DIM
    k_swapped = pltpu.roll(k_all, HEAD_DIM, axis=1)
    zero = jnp.zeros_like(k_all)
    k_par = [[jnp.where(low, k_all, zero), jnp.where(low, zero, k_swapped)],
             [jnp.where(low, k_swapped, zero), jnp.where(low, zero, k_all)]]
    k_par = [[k.astype(BF16) for k in ks] for ks in k_par]

    units = [(blk, h, pg) for blk in range(ts // QBLK) for h in range(N_KV_HEADS)
             for pg in range(PAIR_GROUPS)]

    def scores(blk, h, pg):
        keys = slice(blk * QBLK, blk * QBLK + KBLK)
        k_blk = jnp.concatenate([k_par[h][0][keys], k_par[h][1][keys]], axis=0)
        pair0 = h * PAIRS_PER_KV + pg * 2
        qa = q_ref[blk, pair0 * QBLK:(pair0 + 2) * QBLK, :]
        s = lax.dot_general(k_blk, qa, (((1,), (1,)), ((), ())), preferred_element_type=F32)
        variant = first_variant if blk == 0 else 0
        return s + bias_ref[variant, h, pg]

    def softmax(s, blk, h, pg):
        out = []
        for par in range(HEADS_PER_TILE):
            sp = s[par * KBLK:(par + 1) * KBLK]
            row = (h * PAIR_GROUPS + pg) * HEADS_PER_TILE + par
            sink = sink_ref[row:row + 1, :] * LOG2E
            m = jnp.maximum(jnp.max(sp, axis=0, keepdims=True), sink)
            p = jnp.exp2(sp - m)
            denom = jnp.sum(p, axis=0, keepdims=True) + jnp.exp2(sink - m)
            out.append((p.astype(BF16), 1.0 / denom))
        return out

    def weighted_values(probs, blk, h, pg):
        keys = slice(blk * QBLK, blk * QBLK + KBLK)
        cols = slice(blk * QBLK, (blk + 1) * QBLK)
        vh_t = v_t[h * HEAD_DIM:(h + 1) * HEAD_DIM, keys]
        pair0 = h * PAIRS_PER_KV + pg * 2
        for par, (p, inv) in enumerate(probs):
            o = _dot(vh_t, p) * inv
            for pl_ in range(2):
                head = (pair0 + pl_) * HEADS_PER_TILE + par
                attn_t_ref[head * HEAD_DIM:(head + 1) * HEAD_DIM, cols] = o[:, pl_ * QBLK:(pl_ + 1) * QBLK]

    blk_i = lax.broadcasted_iota(jnp.int32, (GMLP_CHUNK, GMLP_CHUNK), 0) // CHUNK
    blk_j = lax.broadcasted_iota(jnp.int32, (GMLP_CHUNK, GMLP_CHUNK), 1) // CHUNK
    for g in range(GMLP_GROUPS):
        w = jnp.where(blk_j <= blk_i, ws_ref[g], 0.0).astype(BF16)
        b = bs_ref[:, g:g + 1]
        cols = slice(g * GMLP_GROUP_DIM, (g + 1) * GMLP_GROUP_DIM)
        for c in range(ts // GMLP_CHUNK):
            rows = slice(c * GMLP_CHUNK, (c + 1) * GMLP_CHUNK)
            sp = _dot(w, vn_ref[0, rows, cols]) + b
            gm_ref[rows, cols] = (u_ref[0, rows, cols].astype(F32) * sp).astype(BF16)

    def gmlp_branch(c):
        cols = slice(c * MXU_TILE, (c + 1) * MXU_TILE)
        return gb_ref[0, :, cols].astype(F32) * _dot(gm_ref[...], wbg_ref[:, cols])

    n_side = D_MODEL // MXU_TILE
    side = []
    s_vals, p_vals = {}, {}
    for i in range(len(units) + 2):
        if i < len(units):
            s_vals[i] = scores(*units[i])
        if 1 <= i <= len(units):
            p_vals[i - 1] = softmax(s_vals.pop(i - 1), *units[i - 1])
        if i >= 2:
            weighted_values(p_vals.pop(i - 2), *units[i - 2])
        if i % (len(units) // n_side) == 1 and len(side) < n_side:
            side.append(gmlp_branch(len(side)))
    assert len(side) == n_side

    attn = jnp.transpose(attn_t_ref[...]).astype(BF16)
    merged = ga_ref[0].astype(F32) * _dot(attn, wba_ref[...]) + jnp.concatenate(side, axis=1)
    out_ref[0] = x1_ref[0] + _rms(_dot(merged.astype(BF16), wo_ref[...]), g_ref[...])


def _prompt_mixer(q, kv, u, vn, ga, gb, x1, bias, sink, w_s, b_s_t, wba, wbg, wo, g_post, *, ts):
    batch, seq, _ = kv.shape
    assert seq % ts == 0 and ts % QBLK == 0 and QBLK == GMLP_CHUNK == WINDOW
    tile = lambda b, t: (b, t, 0)
    prev = lambda b, t: (b, jnp.maximum(t * (ts // WINDOW) - 1, 0), 0)
    wide = pl.BlockSpec((1, ts, D_MODEL), tile)
    n_t = seq // ts
    q_spec = pl.BlockSpec((ts // QBLK, N_HEAD_PAIRS * QBLK, LANES), lambda b, t: (b * n_t + t, 0, 0))
    return pl.pallas_call(
        functools.partial(_prompt_mixer_body, ts),
        grid=(batch, n_t),
        in_specs=[q_spec, pl.BlockSpec((1, ts, 2 * KV_WIDTH), tile),
                  pl.BlockSpec((1, WINDOW, 2 * KV_WIDTH), prev),
                  wide, wide, wide, wide, wide,
                  _resident(bias.shape), _resident(sink.shape), _resident(w_s.shape),
                  _resident(b_s_t.shape), _resident(wba.shape), _resident(wbg.shape),
                  _resident(wo.shape), _resident(g_post.shape)],
        out_specs=wide,
        out_shape=jax.ShapeDtypeStruct((batch, seq, D_MODEL), F32),
        scratch_shapes=[pltpu.VMEM((ATTN_WIDTH, ts), F32), pltpu.VMEM((ts, GMLP_WIDTH), BF16)],
        compiler_params=_params(2),
        name="prompt_mixer",
    )(q, kv, kv, u, vn, ga, gb, x1, bias, sink, w_s, b_s_t, wba, wbg, wo, g_post)


def _sample_mixer_body(n_batch, n_new, q_ref, kv_ref, ck_ref, cv_ref, u_ref, vn_ref, ga_ref, gb_ref,
                       x1_ref, bias_ref, sink_ref, ws_ref, bs_ref, wba_ref, wbg_ref, wo_ref, g_ref,
                       out_ref, attn_ref, gm_ref):
    for b in range(n_batch):
        rows = slice(b * n_new, (b + 1) * n_new)
        q = q_ref[rows, :]
        kv = kv_ref[rows, :]
        k_all = jnp.concatenate([ck_ref[b], kv[:, :KV_WIDTH]], axis=0).astype(BF16)
        v_all = jnp.concatenate([cv_ref[b], kv[:, KV_WIDTH:]], axis=0).astype(BF16)
        for h in range(N_KV_HEADS):
            cols = slice(h * HEAD_DIM, (h + 1) * HEAD_DIM)
            o = _sink_attention(_stack_heads(q, h), k_all[:, cols], v_all[:, cols],
                                bias_ref[h], sink_ref[h])
            width = Q_PER_KV * HEAD_DIM
            attn_ref[rows, h * width:(h + 1) * width] = _unstack_heads(o, n_new).astype(BF16)
        for g in range(GMLP_GROUPS):
            cols = slice(g * GMLP_GROUP_DIM, (g + 1) * GMLP_GROUP_DIM)
            w = ws_ref[g, :n_new, :n_new].astype(BF16)
            sp = _dot(w, vn_ref[rows, cols].astype(BF16)) + bs_ref[:n_new, g:g + 1]
            gm_ref[rows, cols] = (u_ref[rows, cols].astype(F32) * sp).astype(BF16)

    out_ref[...] = _merge_out(x1_ref[...], attn_ref[...], gm_ref[...],
                              ga_ref[...].astype(F32), gb_ref[...].astype(F32),
                              wba_ref, wbg_ref, wo_ref, g_ref[...])


def _sample_mixer(q, kv, cache_k, cache_v, u, vn, ga, gb, x1, bias, sink, w_s, b_s_t,
                  wba, wbg, wo, g_post, *, n_batch, n_new):
    assert n_new <= CHUNK
    t = q.shape[0]
    vmem = pl.BlockSpec(memory_space=pltpu.VMEM)
    return pl.pallas_call(
        functools.partial(_sample_mixer_body, n_batch, n_new),
        in_specs=[vmem] * 17,
        out_specs=vmem,
        out_shape=jax.ShapeDtypeStruct((t, D_MODEL), F32),
        scratch_shapes=[pltpu.VMEM((t, ATTN_WIDTH), BF16), pltpu.VMEM((t, GMLP_WIDTH), BF16)],
        compiler_params=pltpu.CompilerParams(vmem_limit_bytes=VMEM_LIMIT_BYTES),
        name="sample_mixer",
    )(q, kv, cache_k, cache_v, u, vn, ga, gb, x1, bias, sink, w_s, b_s_t, wba, wbg, wo, g_post)


def _prompt_sink_rows(sinks):
    s = sinks.astype(F32).reshape(N_KV_HEADS, PAIR_GROUPS, 2, HEADS_PER_TILE)
    s = jnp.transpose(s, (0, 1, 3, 2))[..., None]
    s = jnp.broadcast_to(s, (N_KV_HEADS, PAIR_GROUPS, HEADS_PER_TILE, 2, QBLK))
    return s.reshape(N_KV_HEADS * PAIR_GROUPS * HEADS_PER_TILE, 2 * QBLK)


def _sink_rows(sinks, n_q):
    s = jnp.broadcast_to(sinks.astype(F32).reshape(N_KV_HEADS, Q_PER_KV, 1), (N_KV_HEADS, Q_PER_KV, n_q))
    return s.reshape(N_KV_HEADS, Q_PER_KV * n_q, 1)


PROMPT_TM = 512
PROMPT_TS = 512


def kernel(x_prompt, x_sample, cache_win_k, cache_win_v, rel_bias_table, norm_gains, ffn1_w_gate, ffn1_w_up, ffn1_w_down, w_in, attn_sinks, gmlp_ln_g, gmlp_ln_b, gmlp_w_s, gmlp_b_s, w_branch_attn, w_branch_gmlp, w_out, ffn2_w_gate, ffn2_w_up, ffn2_w_down):
    depth = norm_gains.shape[0]
    batch, seq, _ = x_prompt.shape
    dec_batch, dec_seq, _ = x_sample.shape
    n_cache = cache_win_k.shape[2]
    assert seq % PROMPT_TS == 0 and dec_seq <= CHUNK

    bias_p = _prompt_bias(rel_bias_table)
    bias_s = _relative_bias(rel_bias_table, dec_seq, n_cache + dec_seq, n_cache)

    xp = x_prompt.reshape(batch * seq, D_MODEL)
    xs = x_sample.reshape(dec_batch * dec_seq, D_MODEL)
    t_s = xs.shape[0]
    kp, vp, ks, vs, gs = [], [], [], [], []
    for l in range(depth):
        g = norm_gains[l].astype(F32)
        w1 = [_to_bf16(w[l]) for w in (ffn1_w_gate, ffn1_w_up, ffn1_w_down)]
        w2 = [_to_bf16(w[l]) for w in (ffn2_w_gate, ffn2_w_up, ffn2_w_down)]
        win = _to_bf16(w_in[l])
        wba, wbg, wo = (_to_bf16(w[l]) for w in (w_branch_attn, w_branch_gmlp, w_out))
        ln = jnp.stack([gmlp_ln_g[l], gmlp_ln_b[l]]).astype(F32)
        w_s = gmlp_w_s[l].astype(F32)
        b_s_t = jnp.transpose(gmlp_b_s[l]).astype(F32)
        g_ffn1, g_post2, g_ffn2 = g[0:3], g[3:4], jnp.concatenate([g[4:6], g[5:6]])
        sink_p = _prompt_sink_rows(attn_sinks[l])
        sink_s = _sink_rows(attn_sinks[l], dec_seq)

        x1, h2 = _ffn(xp, g_ffn1, *w1, tm=PROMPT_TM, emit_next=True, name="ffn1_prompt")
        q, kv, u, vn, ga, gb = _in_proj(h2, win, ln, tm=PROMPT_TM, vn_dtype=BF16, q_blocked=True,
                                        name="in_proj_prompt")
        b3 = lambda a: a.reshape(batch, seq, a.shape[-1])
        x2 = _prompt_mixer(q, b3(kv), b3(u), b3(vn), b3(ga), b3(gb), b3(x1), bias_p, sink_p,
                           w_s, b_s_t, wba, wbg, wo, g_post2, ts=PROMPT_TS)
        (xp,) = _ffn(x2.reshape(batch * seq, D_MODEL), g_ffn2, *w2, tm=PROMPT_TM, emit_next=False,
                     name="ffn2_prompt")
        kv_win = b3(kv)[:, seq - WINDOW:, :]
        kp.append(kv_win[..., :KV_WIDTH].reshape(batch, WINDOW, N_KV_HEADS, HEAD_DIM))
        vp.append(kv_win[..., KV_WIDTH:].reshape(batch, WINDOW, N_KV_HEADS, HEAD_DIM))

        x1, h2 = _ffn(xs, g_ffn1, *w1, tm=t_s, emit_next=True, name="ffn1_sample")
        q, kv, u, vn, ga, gb = _in_proj(h2, win, ln, tm=t_s, vn_dtype=F32, q_blocked=False,
                                        name="in_proj_sample")
        ck = cache_win_k[l].reshape(dec_batch, n_cache, KV_WIDTH)
        cv = cache_win_v[l].reshape(dec_batch, n_cache, KV_WIDTH)
        x2 = _sample_mixer(q, kv, ck, cv, u, vn, ga, gb, x1, bias_s, sink_s, w_s, b_s_t,
                           wba, wbg, wo, g_post2, n_batch=dec_batch, n_new=dec_seq)
        (xs,) = _ffn(x2, g_ffn2, *w2, tm=t_s, emit_next=False, name="ffn2_sample")
        ks.append(kv[:, :KV_WIDTH].reshape(dec_batch, dec_seq, N_KV_HEADS, HEAD_DIM))
        vs.append(kv[:, KV_WIDTH:].reshape(dec_batch, dec_seq, N_KV_HEADS, HEAD_DIM))
        gs.append(vn.reshape(dec_batch, dec_seq, GMLP_WIDTH))

    return (xp.reshape(batch, seq, D_MODEL), xs.reshape(dec_batch, dec_seq, D_MODEL),
            jnp.stack(kp), jnp.stack(vp), jnp.stack(ks), jnp.stack(vs), jnp.stack(gs))
```

```python
import functools
import math

import jax
import jax.numpy as jnp
import numpy as np
from jax import lax
from jax.experimental import pallas as pl
from jax.experimental.pallas import tpu as pltpu

D_MODEL = 1024
CHUNK = 64
N_Q_HEADS = 16
N_KV_HEADS = 2
HEAD_DIM = 64
Q_PER_KV = N_Q_HEADS // N_KV_HEADS
ATTN_WIDTH = N_Q_HEADS * HEAD_DIM
KV_WIDTH = N_KV_HEADS * HEAD_DIM
WINDOW = 128
GMLP_WIDTH = 1024
GMLP_GROUPS = 4
GMLP_GROUP_DIM = GMLP_WIDTH // GMLP_GROUPS
GMLP_CHUNK = 128
NUM_BUCKETS = 32
MAX_DISTANCE = 128
D_FF = 2816
EPS = 1e-6
NEG_INF = -1e30
LOG2E = math.log2(math.e)

OFF_Q = 0
OFF_KV = ATTN_WIDTH
OFF_U = OFF_KV + 2 * KV_WIDTH
OFF_GV = OFF_U + GMLP_WIDTH
OFF_GA = OFF_GV + GMLP_WIDTH
OFF_GB = OFF_GA + D_MODEL

V7X_VMEM_BYTES = 64 * 1024 * 1024
VMEM_LIMIT_BYTES = V7X_VMEM_BYTES - 8 * 1024 * 1024
MXU_TILE = 256
LANES = 128

QBLK = 2 * CHUNK
KBLK = WINDOW + QBLK
HEADS_PER_TILE = LANES // HEAD_DIM
N_HEAD_PAIRS = N_Q_HEADS // HEADS_PER_TILE
PAIRS_PER_KV = Q_PER_KV // HEADS_PER_TILE
PAIR_GROUPS = PAIRS_PER_KV // 2

BF16 = jnp.bfloat16
F32 = jnp.float32


def _dot(a, b):
    return jnp.dot(a, b, preferred_element_type=F32)


def _rms(x, g):
    return x * lax.rsqrt(jnp.mean(x * x, axis=-1, keepdims=True) + EPS) * g


def _resident(shape):
    zeros = (0,) * len(shape)
    return pl.BlockSpec(shape, lambda *_: zeros, pipeline_mode=pl.Buffered(1))


def _params(n_axes):
    return pltpu.CompilerParams(dimension_semantics=("arbitrary",) * n_axes,
                                vmem_limit_bytes=VMEM_LIMIT_BYTES)


CAST_BLOCK_BYTES = 1024 * 1024


def _cast_body(*refs):
    n = len(refs) // 2
    for w_ref, o_ref in zip(refs[:n], refs[n:]):
        o_ref[...] = w_ref[...].astype(o_ref.dtype)


def _to_bf16(*ws):
    k, n = ws[0].shape
    assert all(w.shape == (k, n) for w in ws)
    rows = [r for r in range(16, k + 1, 16) if k % r == 0 and r * n * 4 <= CAST_BLOCK_BYTES]
    bk = max(rows) if rows else k
    spec = pl.BlockSpec((bk, n), lambda i: (i, 0))
    return pl.pallas_call(
        _cast_body,
        grid=(k // bk,),
        in_specs=[spec] * len(ws),
        out_specs=[spec] * len(ws),
        out_shape=[jax.ShapeDtypeStruct((k, n), BF16)] * len(ws),
        compiler_params=_params(1),
        name="weights_to_bf16",
    )(*ws)


def _ordering_zero(*arrays):
    m = None
    for a in arrays:
        r = jnp.max(jnp.max(a.astype(F32), axis=0, keepdims=True), axis=1, keepdims=True)
        m = r if m is None else jnp.maximum(m, r)
    bits = lax.bitcast_convert_type(m, jnp.uint32)
    return lax.bitcast_convert_type((bits >> 16) >> 16, F32)


FF_CHUNK = MXU_TILE
FFN_SIDE_PIECES = 8


def _ffn_body(n_tiles, emit_next, xp_ref, xe_ref, g_ref, wg_ref, wu_ref, wd_ref, *refs):
    y_ref = refs[0]
    h_ref, acc_ref = refs[-2:]
    s = pl.program_id(0)
    slot = s % 2
    tm = y_ref.shape[0]

    def pre_norm(dst, rows):
        h_ref[dst, rows, :] = _rms(xp_ref[rows, :], g_ref[0:1, :]).astype(BF16)
        return [h_ref[dst, rows, :]]

    def finish(rows):
        y = xe_ref[rows, :] + 0.5 * _rms(acc_ref[rows, :], g_ref[1:2, :])
        y_ref[rows, :] = y
        stored = [y_ref[rows, :]]
        if emit_next:
            refs[1][rows, :] = _rms(y, g_ref[2:3, :]).astype(BF16)
            stored.append(refs[1][rows, :])
        return stored

    @pl.when(s == 0)
    def _():
        pre_norm(0, slice(None))
        acc_ref[...] = jnp.zeros_like(acc_ref)

    @pl.when(jnp.logical_and(s >= 1, s <= n_tiles))
    def _():
        h = h_ref[1 - slot]
        acc = None
        anchor = None
        piece_rows = tm // FFN_SIDE_PIECES
        assert D_FF // FF_CHUNK > FFN_SIDE_PIECES
        for ci in range(D_FF // FF_CHUNK):
            cols = slice(ci * FF_CHUNK, (ci + 1) * FF_CHUNK)
            gate = _dot(h, wg_ref[:, cols])
            up = _dot(h, wu_ref[:, cols])
            if anchor is not None:
                up = up + anchor
            act = (jax.nn.silu(gate) * up).astype(BF16)
            part = _dot(act, wd_ref[cols, :])
            acc = part if acc is None else acc + part
            anchor = None
            if ci < FFN_SIDE_PIECES:
                rows = slice(ci * piece_rows, (ci + 1) * piece_rows)
                anchor = _ordering_zero(*(finish(rows) + pre_norm(slot, rows)))
        acc_ref[...] = acc

    @pl.when(s == n_tiles + 1)
    def _():
        finish(slice(None))


def _ffn(x, gains, wg, wu, wd, *, tm, emit_next, name):
    t = x.shape[0]
    assert t % tm == 0
    n_tiles = t // tm
    head = lambda s: (jnp.minimum(s, n_tiles - 1), 0)
    tail = lambda s: (jnp.clip(s - 2, 0, n_tiles - 1), 0)
    out_shape = [jax.ShapeDtypeStruct((t, D_MODEL), F32)]
    out_specs = [pl.BlockSpec((tm, D_MODEL), tail)]
    if emit_next:
        out_shape.append(jax.ShapeDtypeStruct((t, D_MODEL), BF16))
        out_specs.append(pl.BlockSpec((tm, D_MODEL), tail))
    return pl.pallas_call(
        functools.partial(_ffn_body, n_tiles, emit_next),
        grid=(n_tiles + 2,),
        in_specs=[pl.BlockSpec((tm, D_MODEL), head), pl.BlockSpec((tm, D_MODEL), tail),
                  _resident(gains.shape), _resident(wg.shape), _resident(wu.shape), _resident(wd.shape)],
        out_specs=out_specs,
        out_shape=out_shape,
        scratch_shapes=[pltpu.VMEM((2, tm, D_MODEL), BF16), pltpu.VMEM((tm, D_MODEL), F32)],
        compiler_params=_params(1),
        name=name,
    )(x, x, gains, wg, wu, wd)


def _gelu(x):
    return 0.5 * x * (1.0 + lax.erf(x * np.sqrt(0.5).astype(np.float32)))


def _layer_norm(x, g, b):
    mu = jnp.mean(x, axis=-1, keepdims=True)
    xc = x - mu
    var = jnp.mean(xc * xc, axis=-1, keepdims=True)
    return xc * lax.rsqrt(var + EPS) * g + b


IN_PROJ_LN_PIECES = 8


def _in_proj_body(q_blocked, h_ref, w_ref, ln_ref, q_ref, kv_ref, u_ref, vn_ref, ga_ref, gb_ref, gv_ref):
    h = h_ref[...]
    tm = h.shape[0]
    n_chunks = D_MODEL // MXU_TILE
    q_scale = HEAD_DIM ** -0.5 * (LOG2E if q_blocked else 1.0)

    def store_q(c, r):
        q = (r * q_scale).astype(q_ref.dtype)
        if q_blocked:
            for b in range(tm // QBLK):
                for j in range(MXU_TILE // LANES):
                    p = c * (MXU_TILE // LANES) + j
                    q_ref[b, p * QBLK:(p + 1) * QBLK, :] = q[b * QBLK:(b + 1) * QBLK, j * LANES:(j + 1) * LANES]
        else:
            q_ref[:, c * MXU_TILE:(c + 1) * MXU_TILE] = q

    def store_cols(ref, fn):
        def store(c, r):
            ref[:, c * MXU_TILE:(c + 1) * MXU_TILE] = fn(r).astype(ref.dtype)
        return store

    def layer_norm_piece(k):
        rows = slice(k * (tm // IN_PROJ_LN_PIECES), (k + 1) * (tm // IN_PROJ_LN_PIECES))
        vn_ref[rows, :] = _layer_norm(_gelu(gv_ref[rows, :]), ln_ref[0:1, :], ln_ref[1:2, :]).astype(vn_ref.dtype)

    def run(off, c, store):
        store(c, _dot(h, w_ref[:, off + c * MXU_TILE: off + (c + 1) * MXU_TILE]))

    store_gv = lambda c, r: gv_ref.__setitem__((slice(None), slice(c * MXU_TILE, (c + 1) * MXU_TILE)), r)
    store_u, store_ga, store_gb = (store_cols(u_ref, _gelu), store_cols(ga_ref, jax.nn.sigmoid),
                                   store_cols(gb_ref, jax.nn.sigmoid))
    for c in range(n_chunks):
        run(OFF_GV, c, store_gv)
    run(OFF_KV, 0, lambda c, r: kv_ref.__setitem__(Ellipsis, r))
    for c in range(n_chunks):
        run(OFF_Q, c, store_q)
        layer_norm_piece(2 * c)
        run(OFF_GA, c, store_ga)
        layer_norm_piece(2 * c + 1)
    for c in range(n_chunks):
        run(OFF_U, c, store_u)
        run(OFF_GB, c, store_gb)


def _in_proj(h, w_in, ln, *, tm, vn_dtype, q_blocked, name):
    t = h.shape[0]
    assert t % tm == 0
    row = lambda i: (i, 0)
    wide = lambda dt: jax.ShapeDtypeStruct((t, D_MODEL), dt)
    wide_spec = pl.BlockSpec((tm, D_MODEL), row)
    if q_blocked:
        assert tm % QBLK == 0
        q_shape = jax.ShapeDtypeStruct((t // QBLK, N_HEAD_PAIRS * QBLK, LANES), BF16)
        q_spec = pl.BlockSpec((tm // QBLK, N_HEAD_PAIRS * QBLK, LANES), lambda i: (i, 0, 0))
    else:
        q_shape, q_spec = wide(BF16), wide_spec
    return pl.pallas_call(
        functools.partial(_in_proj_body, q_blocked),
        grid=(t // tm,),
        in_specs=[wide_spec, _resident(w_in.shape), _resident(ln.shape)],
        out_specs=[q_spec, pl.BlockSpec((tm, 2 * KV_WIDTH), row),
                   wide_spec, wide_spec, wide_spec, wide_spec],
        out_shape=[q_shape, jax.ShapeDtypeStruct((t, 2 * KV_WIDTH), F32),
                   wide(BF16), wide(vn_dtype), wide(BF16), wide(BF16)],
        scratch_shapes=[pltpu.VMEM((tm, GMLP_WIDTH), F32)],
        compiler_params=_params(1),
        name=name,
    )(h, w_in, ln)


def _t5_bucket(rel):
    half = NUM_BUCKETS // 2
    max_exact = half // 2
    ret = jnp.where(rel > 0, half, 0)
    n = jnp.abs(rel)
    nf = jnp.maximum(n, 1).astype(F32)
    large = max_exact + (jnp.log(nf / max_exact) / math.log(MAX_DISTANCE / max_exact)
                         * (half - max_exact)).astype(jnp.int32)
    large = jnp.minimum(large, half - 1)
    return ret + jnp.where(n < max_exact, n, large)


def _bias_body(table_ref, bucket_ref, out_ref):
    bucket = bucket_ref[...]
    hits = [bucket == b for b in range(NUM_BUCKETS)]
    for head in range(N_Q_HEADS):
        acc = jnp.zeros(bucket.shape, F32)
        for b in range(NUM_BUCKETS):
            acc = jnp.where(hits[b], table_ref[b, head], acc)
        out_ref[head] = acc


def _prompt_bias_body(table_ref, bucket_ref, out_ref):
    bucket = bucket_ref[...]
    hits = [bucket == b for b in range(NUM_BUCKETS)]
    key_chunk = lax.broadcasted_iota(jnp.int32, bucket.shape, 0) // CHUNK
    q_chunk = lax.broadcasted_iota(jnp.int32, bucket.shape, 1) // CHUNK
    in_window = jnp.logical_and(key_chunk >= q_chunk, key_chunk <= q_chunk + WINDOW // CHUNK)
    valid = [in_window, jnp.logical_and(in_window, key_chunk >= WINDOW // CHUNK)]
    for head in range(N_Q_HEADS):
        acc = jnp.zeros(bucket.shape, F32)
        for b in range(NUM_BUCKETS):
            acc = jnp.where(hits[b], table_ref[b, head], acc)
        h, rest = divmod(head, Q_PER_KV)
        pair, par = divmod(rest, HEADS_PER_TILE)
        pg, pl_ = divmod(pair, 2)
        for v in range(2):
            out_ref[v, h, pg, par * KBLK:(par + 1) * KBLK, pl_ * QBLK:(pl_ + 1) * QBLK] = (
                jnp.where(valid[v], acc * LOG2E, NEG_INF))


def _prompt_bias(table):
    rel = (jnp.arange(KBLK) - WINDOW)[:, None] - jnp.arange(QBLK)[None, :]
    bucket = _t5_bucket(rel).astype(jnp.int32)
    return pl.pallas_call(
        _prompt_bias_body,
        in_specs=[pl.BlockSpec(memory_space=pltpu.SMEM),
                  pl.BlockSpec(memory_space=pltpu.VMEM)],
        out_specs=pl.BlockSpec(memory_space=pltpu.VMEM),
        out_shape=jax.ShapeDtypeStruct((2, N_KV_HEADS, PAIR_GROUPS, 2 * KBLK, 2 * QBLK), F32),
        name="rel_bias_prompt",
    )(table, bucket)


def _relative_bias(table, n_q, n_keys, n_past):
    rel = (jnp.arange(n_keys) - n_past)[None, :] - jnp.arange(n_q)[:, None]
    bucket = _t5_bucket(rel).astype(jnp.int32)
    bias = pl.pallas_call(
        _bias_body,
        in_specs=[pl.BlockSpec(memory_space=pltpu.SMEM),
                  pl.BlockSpec(memory_space=pltpu.VMEM)],
        out_specs=pl.BlockSpec(memory_space=pltpu.VMEM),
        out_shape=jax.ShapeDtypeStruct((N_Q_HEADS, n_q, n_keys), F32),
        name="rel_bias_%d" % n_q,
    )(table, bucket)
    return bias.reshape(N_KV_HEADS, Q_PER_KV * n_q, n_keys)


def _sink_attention(qh, kh, vh, bias, sink, invalid=None):
    s = lax.dot_general(qh, kh, (((1,), (1,)), ((), ())), preferred_element_type=F32) + bias
    if invalid is not None:
        s = jnp.where(invalid, NEG_INF, s)
    m = jnp.maximum(jnp.max(s, axis=-1, keepdims=True), sink)
    p = jnp.exp(s - m)
    denom = jnp.sum(p, axis=-1, keepdims=True) + jnp.exp(sink - m)
    return _dot(p.astype(BF16), vh) / denom


def _stack_heads(q, kv_head):
    base = kv_head * Q_PER_KV * HEAD_DIM
    return jnp.concatenate(
        [q[:, base + g * HEAD_DIM: base + (g + 1) * HEAD_DIM] for g in range(Q_PER_KV)], axis=0)


def _unstack_heads(o, n):
    return jnp.concatenate([o[g * n:(g + 1) * n, :] for g in range(Q_PER_KV)], axis=1)


def _merge_out(x1, attn, gm, ga, gb, wba_ref, wbg_ref, wo_ref, g_post):
    merged = ga * _dot(attn, wba_ref[...]) + gb * _dot(gm, wbg_ref[...])
    return x1 + _rms(_dot(merged.astype(BF16), wo_ref[...]), g_post)


def _prompt_mixer_body(ts, q_ref, kv_ref, kvp_ref, u_ref, vn_ref, ga_ref, gb_ref, x1_ref,
                       bias_ref, sink_ref, ws_ref, bs_ref, wba_ref, wbg_ref, wo_ref, g_ref,
                       out_ref, attn_t_ref, gm_ref):
    first_variant = jnp.where(pl.program_id(1) == 0, 1, 0)
    kv_all = jnp.concatenate([kvp_ref[0], kv_ref[0]], axis=0)
    k_all = kv_all[:, :KV_WIDTH]
    v_t = jnp.transpose(kv_all[:, KV_WIDTH:]).astype(BF16)
    low = lax.broadcasted_iota(jnp.int32, k_all.shape, 1) < HEAD_DIM
    k_swapped = pltpu.roll(k_all, HEAD_DIM, axis=1)
    zero = jnp.zeros_like(k_all)
    k_par = [[jnp.where(low, k_all, zero), jnp.where(low, zero, k_swapped)],
             [jnp.where(low, k_swapped, zero), jnp.where(low, zero, k_all)]]
    k_par = [[k.astype(BF16) for k in ks] for ks in k_par]

    units = [(blk, h, pg) for blk in range(ts // QBLK) for h in range(N_KV_HEADS)
             for pg in range(PAIR_GROUPS)]

    def scores(blk, h, pg):
        keys = slice(blk * QBLK, blk * QBLK + KBLK)
        k_blk = jnp.concatenate([k_par[h][0][keys], k_par[h][1][keys]], axis=0)
        pair0 = h * PAIRS_PER_KV + pg * 2
        qa = q_ref[blk, pair0 * QBLK:(pair0 + 2) * QBLK, :]
        s = lax.dot_general(k_blk, qa, (((1,), (1,)), ((), ())), preferred_element_type=F32)
        variant = first_variant if blk == 0 else 0
        return s + bias_ref[variant, h, pg]

    def softmax(s, blk, h, pg):
        out = []
        for par in range(HEADS_PER_TILE):
            sp = s[par * KBLK:(par + 1) * KBLK]
            row = (h * PAIR_GROUPS + pg) * HEADS_PER_TILE + par
            sink = sink_ref[row:row + 1, :] * LOG2E
            m = jnp.maximum(jnp.max(sp, axis=0, keepdims=True), sink)
            p = jnp.exp2(sp - m)
            denom = jnp.sum(p, axis=0, keepdims=True) + jnp.exp2(sink - m)
            out.append((p.astype(BF16), 1.0 / denom))
        return out

    def weighted_values(probs, blk, h, pg):
        keys = slice(blk * QBLK, blk * QBLK + KBLK)
        cols = slice(blk * QBLK, (blk + 1) * QBLK)
        vh_t = v_t[h * HEAD_DIM:(h + 1) * HEAD_DIM, keys]
        pair0 = h * PAIRS_PER_KV + pg * 2
        for par, (p, inv) in enumerate(probs):
            o = _dot(vh_t, p) * inv
            for pl_ in range(2):
                head = (pair0 + pl_) * HEADS_PER_TILE + par
                attn_t_ref[head * HEAD_DIM:(head + 1) * HEAD_DIM, cols] = o[:, pl_ * QBLK:(pl_ + 1) * QBLK]

    blk_i = lax.broadcasted_iota(jnp.int32, (GMLP_CHUNK, GMLP_CHUNK), 0) // CHUNK
    blk_j = lax.broadcasted_iota(jnp.int32, (GMLP_CHUNK, GMLP_CHUNK), 1) // CHUNK
    for g in range(GMLP_GROUPS):
        w = jnp.where(blk_j <= blk_i, ws_ref[g], 0.0).astype(BF16)
        b = bs_ref[:, g:g + 1]
        cols = slice(g * GMLP_GROUP_DIM, (g + 1) * GMLP_GROUP_DIM)
        for c in range(ts // GMLP_CHUNK):
            rows = slice(c * GMLP_CHUNK, (c + 1) * GMLP_CHUNK)
            sp = _dot(w, vn_ref[0, rows, cols]) + b
            gm_ref[rows, cols] = (u_ref[0, rows, cols].astype(F32) * sp).astype(BF16)

    def gmlp_branch(c):
        cols = slice(c * MXU_TILE, (c + 1) * MXU_TILE)
        return gb_ref[0, :, cols].astype(F32) * _dot(gm_ref[...], wbg_ref[:, cols])

    n_side = D_MODEL // MXU_TILE
    side = []
    s_vals, p_vals = {}, {}
    for i in range(len(units) + 2):
        if i < len(units):
            s_vals[i] = scores(*units[i])
        if 1 <= i <= len(units):
            p_vals[i - 1] = softmax(s_vals.pop(i - 1), *units[i - 1])
        if i >= 2:
            weighted_values(p_vals.pop(i - 2), *units[i - 2])
        if i % (len(units) // n_side) == 1 and len(side) < n_side:
            side.append(gmlp_branch(len(side)))
    assert len(side) == n_side

    attn = jnp.transpose(attn_t_ref[...]).astype(BF16)
    merged = ga_ref[0].astype(F32) * _dot(attn, wba_ref[...]) + jnp.concatenate(side, axis=1)
    out_ref[0] = x1_ref[0] + _rms(_dot(merged.astype(BF16), wo_ref[...]), g_ref[...])


def _prompt_mixer(q, kv, u, vn, ga, gb, x1, bias, sink, w_s, b_s_t, wba, wbg, wo, g_post, *, ts):
    batch, seq, _ = kv.shape
    assert seq % ts == 0 and ts % QBLK == 0 and QBLK == GMLP_CHUNK == WINDOW
    tile = lambda b, t: (b, t, 0)
    prev = lambda b, t: (b, jnp.maximum(t * (ts // WINDOW) - 1, 0), 0)
    wide = pl.BlockSpec((1, ts, D_MODEL), tile)
    n_t = seq // ts
    q_spec = pl.BlockSpec((ts // QBLK, N_HEAD_PAIRS * QBLK, LANES), lambda b, t: (b * n_t + t, 0, 0))
    return pl.pallas_call(
        functools.partial(_prompt_mixer_body, ts),
        grid=(batch, n_t),
        in_specs=[q_spec, pl.BlockSpec((1, ts, 2 * KV_WIDTH), tile),
                  pl.BlockSpec((1, WINDOW, 2 * KV_WIDTH), prev),
                  wide, wide, wide, wide, wide,
                  _resident(bias.shape), _resident(sink.shape), _resident(w_s.shape),
                  _resident(b_s_t.shape), _resident(wba.shape), _resident(wbg.shape),
                  _resident(wo.shape), _resident(g_post.shape)],
        out_specs=wide,
        out_shape=jax.ShapeDtypeStruct((batch, seq, D_MODEL), F32),
        scratch_shapes=[pltpu.VMEM((ATTN_WIDTH, ts), F32), pltpu.VMEM((ts, GMLP_WIDTH), BF16)],
        compiler_params=_params(2),
        name="prompt_mixer",
    )(q, kv, kv, u, vn, ga, gb, x1, bias, sink, w_s, b_s_t, wba, wbg, wo, g_post)


def _sample_mixer_body(n_batch, n_new, q_ref, kv_ref, ck_ref, cv_ref, u_ref, vn_ref, ga_ref, gb_ref,
                       x1_ref, bias_ref, sink_ref, ws_ref, bs_ref, wba_ref, wbg_ref, wo_ref, g_ref,
                       out_ref, attn_ref, gm_ref):
    for b in range(n_batch):
        rows = slice(b * n_new, (b + 1) * n_new)
        q = q_ref[rows, :]
        kv = kv_ref[rows, :]
        k_all = jnp.concatenate([ck_ref[b], kv[:, :KV_WIDTH]], axis=0).astype(BF16)
        v_all = jnp.concatenate([cv_ref[b], kv[:, KV_WIDTH:]], axis=0).astype(BF16)
        for h in range(N_KV_HEADS):
            cols = slice(h * HEAD_DIM, (h + 1) * HEAD_DIM)
            o = _sink_attention(_stack_heads(q, h), k_all[:, cols], v_all[:, cols],
                                bias_ref[h], sink_ref[h])
            width = Q_PER_KV * HEAD_DIM
            attn_ref[rows, h * width:(h + 1) * width] = _unstack_heads(o, n_new).astype(BF16)
        for g in range(GMLP_GROUPS):
            cols = slice(g * GMLP_GROUP_DIM, (g + 1) * GMLP_GROUP_DIM)
            w = ws_ref[g, :n_new, :n_new].astype(BF16)
            sp = _dot(w, vn_ref[rows, cols].astype(BF16)) + bs_ref[:n_new, g:g + 1]
            gm_ref[rows, cols] = (u_ref[rows, cols].astype(F32) * sp).astype(BF16)

    out_ref[...] = _merge_out(x1_ref[...], attn_ref[...], gm_ref[...],
                              ga_ref[...].astype(F32), gb_ref[...].astype(F32),
                              wba_ref, wbg_ref, wo_ref, g_ref[...])


def _sample_mixer(q, kv, cache_k, cache_v, u, vn, ga, gb, x1, bias, sink, w_s, b_s_t,
                  wba, wbg, wo, g_post, *, n_batch, n_new):
    assert n_new <= CHUNK
    t = q.shape[0]
    vmem = pl.BlockSpec(memory_space=pltpu.VMEM)
    return pl.pallas_call(
        functools.partial(_sample_mixer_body, n_batch, n_new),
        in_specs=[vmem] * 17,
        out_specs=vmem,
        out_shape=jax.ShapeDtypeStruct((t, D_MODEL), F32),
        scratch_shapes=[pltpu.VMEM((t, ATTN_WIDTH), BF16), pltpu.VMEM((t, GMLP_WIDTH), BF16)],
        compiler_params=pltpu.CompilerParams(vmem_limit_bytes=VMEM_LIMIT_BYTES),
        name="sample_mixer",
    )(q, kv, cache_k, cache_v, u, vn, ga, gb, x1, bias, sink, w_s, b_s_t, wba, wbg, wo, g_post)


def _prompt_sink_rows(sinks):
    s = sinks.astype(F32).reshape(N_KV_HEADS, PAIR_GROUPS, 2, HEADS_PER_TILE)
    s = jnp.transpose(s, (0, 1, 3, 2))[..., None]
    s = jnp.broadcast_to(s, (N_KV_HEADS, PAIR_GROUPS, HEADS_PER_TILE, 2, QBLK))
    return s.reshape(N_KV_HEADS * PAIR_GROUPS * HEADS_PER_TILE, 2 * QBLK)


def _sink_rows(sinks, n_q):
    s = jnp.broadcast_to(sinks.astype(F32).reshape(N_KV_HEADS, Q_PER_KV, 1), (N_KV_HEADS, Q_PER_KV, n_q))
    return s.reshape(N_KV_HEADS, Q_PER_KV * n_q, 1)


PROMPT_TM = 512
PROMPT_TS = 512


def kernel(x_prompt, x_sample, cache_win_k, cache_win_v, rel_bias_table, norm_gains, ffn1_w_gate, ffn1_w_up, ffn1_w_down, w_in, attn_sinks, gmlp_ln_g, gmlp_ln_b, gmlp_w_s, gmlp_b_s, w_branch_attn, w_branch_gmlp, w_out, ffn2_w_gate, ffn2_w_up, ffn2_w_down):
    depth = norm_gains.shape[0]
    batch, seq, _ = x_prompt.shape
    dec_batch, dec_seq, _ = x_sample.shape
    n_cache = cache_win_k.shape[2]
    assert seq % PROMPT_TS == 0 and dec_seq <= CHUNK

    bias_p = _prompt_bias(rel_bias_table)
    bias_s = _relative_bias(rel_bias_table, dec_seq, n_cache + dec_seq, n_cache)

    xp = x_prompt.reshape(batch * seq, D_MODEL)
    xs = x_sample.reshape(dec_batch * dec_seq, D_MODEL)
    t_s = xs.shape[0]
    kp, vp, ks, vs, gs = [], [], [], [], []
    for l in range(depth):
        g = norm_gains[l].astype(F32)
        wg1, wu1, wg2, wu2 = _to_bf16(ffn1_w_gate[l], ffn1_w_up[l], ffn2_w_gate[l], ffn2_w_up[l])
        wd1, wd2 = _to_bf16(ffn1_w_down[l], ffn2_w_down[l])
        w1, w2 = (wg1, wu1, wd1), (wg2, wu2, wd2)
        (win,) = _to_bf16(w_in[l])
        wba, wbg, wo = _to_bf16(w_branch_attn[l], w_branch_gmlp[l], w_out[l])
        ln = jnp.stack([gmlp_ln_g[l], gmlp_ln_b[l]]).astype(F32)
        w_s = gmlp_w_s[l].astype(F32)
        b_s_t = jnp.transpose(gmlp_b_s[l]).astype(F32)
        g_ffn1, g_post2, g_ffn2 = g[0:3], g[3:4], jnp.concatenate([g[4:6], g[5:6]])
        sink_p = _prompt_sink_rows(attn_sinks[l])
        sink_s = _sink_rows(attn_sinks[l], dec_seq)

        x1, h2 = _ffn(xp, g_ffn1, *w1, tm=PROMPT_TM, emit_next=True, name="ffn1_prompt")
        q, kv, u, vn, ga, gb = _in_proj(h2, win, ln, tm=2 * PROMPT_TM, vn_dtype=BF16, q_blocked=True,
                                        name="in_proj_prompt")
        b3 = lambda a: a.reshape(batch, seq, a.shape[-1])
        x2 = _prompt_mixer(q, b3(kv), b3(u), b3(vn), b3(ga), b3(gb), b3(x1), bias_p, sink_p,
                           w_s, b_s_t, wba, wbg, wo, g_post2, ts=PROMPT_TS)
        (xp,) = _ffn(x2.reshape(batch * seq, D_MODEL), g_ffn2, *w2, tm=PROMPT_TM, emit_next=False,
                     name="ffn2_prompt")
        kv_win = b3(kv)[:, seq - WINDOW:, :]
        kp.append(kv_win[..., :KV_WIDTH].reshape(batch, WINDOW, N_KV_HEADS, HEAD_DIM))
        vp.append(kv_win[..., KV_WIDTH:].reshape(batch, WINDOW, N_KV_HEADS, HEAD_DIM))

        x1, h2 = _ffn(xs, g_ffn1, *w1, tm=t_s, emit_next=True, name="ffn1_sample")
        q, kv, u, vn, ga, gb = _in_proj(h2, win, ln, tm=t_s, vn_dtype=F32, q_blocked=False,
                                        name="in_proj_sample")
        ck = cache_win_k[l].reshape(dec_batch, n_cache, KV_WIDTH)
        cv = cache_win_v[l].reshape(dec_batch, n_cache, KV_WIDTH)
        x2 = _sample_mixer(q, kv, ck, cv, u, vn, ga, gb, x1, bias_s, sink_s, w_s, b_s_t,
                           wba, wbg, wo, g_post2, n_batch=dec_batch, n_new=dec_seq)
        (xs,) = _ffn(x2, g_ffn2, *w2, tm=t_s, emit_next=False, name="ffn2_sample")
        ks.append(kv[:, :KV_WIDTH].reshape(dec_batch, dec_seq, N_KV_HEADS, HEAD_DIM))
        vs.append(kv[:, KV_WIDTH:].reshape(dec_batch, dec_seq, N_KV_HEADS, HEAD_DIM))
        gs.append(vn.reshape(dec_batch, dec_seq, GMLP_WIDTH))

    return (xp.reshape(batch, seq, D_MODEL), xs.reshape(dec_batch, dec_seq, D_MODEL),
            jnp.stack(kp), jnp.stack(vp), jnp.stack(ks), jnp.stack(vs), jnp.stack(gs))
```

```python
import functools
import math

import jax
import jax.numpy as jnp
import numpy as np
from jax import lax
from jax.experimental import pallas as pl
from jax.experimental.pallas import tpu as pltpu

D_MODEL = 1024
CHUNK = 64
N_Q_HEADS = 16
N_KV_HEADS = 2
HEAD_DIM = 64
Q_PER_KV = N_Q_HEADS // N_KV_HEADS
ATTN_WIDTH = N_Q_HEADS * HEAD_DIM
KV_WIDTH = N_KV_HEADS * HEAD_DIM
WINDOW = 128
GMLP_WIDTH = 1024
GMLP_GROUPS = 4
GMLP_GROUP_DIM = GMLP_WIDTH // GMLP_GROUPS
GMLP_CHUNK = 128
NUM_BUCKETS = 32
MAX_DISTANCE = 128
D_FF = 2816
EPS = 1e-6
NEG_INF = -1e30
LOG2E = math.log2(math.e)

OFF_Q = 0
OFF_KV = ATTN_WIDTH
OFF_U = OFF_KV + 2 * KV_WIDTH
OFF_GV = OFF_U + GMLP_WIDTH
OFF_GA = OFF_GV + GMLP_WIDTH
OFF_GB = OFF_GA + D_MODEL

V7X_VMEM_BYTES = 64 * 1024 * 1024
VMEM_LIMIT_BYTES = V7X_VMEM_BYTES - 8 * 1024 * 1024
MXU_TILE = 256
LANES = 128

QBLK = 2 * CHUNK
KBLK = WINDOW + QBLK
HEADS_PER_TILE = LANES // HEAD_DIM
N_HEAD_PAIRS = N_Q_HEADS // HEADS_PER_TILE
PAIRS_PER_KV = Q_PER_KV // HEADS_PER_TILE
GROUP_PAIRS = 4
PAIR_GROUPS = PAIRS_PER_KV // GROUP_PAIRS
SIDE_COLS = MXU_TILE

BF16 = jnp.bfloat16
F32 = jnp.float32


def _dot(a, b):
    return jnp.dot(a, b, preferred_element_type=F32)


def _rms(x, g):
    return x * lax.rsqrt(jnp.mean(x * x, axis=-1, keepdims=True) + EPS) * g


def _resident(shape):
    zeros = (0,) * len(shape)
    return pl.BlockSpec(shape, lambda *_: zeros, pipeline_mode=pl.Buffered(1))


def _params(n_axes):
    return pltpu.CompilerParams(dimension_semantics=("arbitrary",) * n_axes,
                                vmem_limit_bytes=VMEM_LIMIT_BYTES)


CAST_BLOCK_BYTES = 2 * 1024 * 1024


def _cast_body(*refs):
    n = len(refs) // 2
    for w_ref, o_ref in zip(refs[:n], refs[n:]):
        o_ref[...] = w_ref[...].astype(o_ref.dtype)


def _to_bf16(*ws):
    k, n = ws[0].shape
    assert all(w.shape == (k, n) for w in ws)
    rows = [r for r in range(16, k + 1, 16) if k % r == 0 and r * n * 4 <= CAST_BLOCK_BYTES]
    bk = max(rows) if rows else k
    spec = pl.BlockSpec((bk, n), lambda i: (i, 0))
    return pl.pallas_call(
        _cast_body,
        grid=(k // bk,),
        in_specs=[spec] * len(ws),
        out_specs=[spec] * len(ws),
        out_shape=[jax.ShapeDtypeStruct((k, n), BF16)] * len(ws),
        compiler_params=_params(1),
        name="weights_to_bf16",
    )(*ws)


def _ordering_zero(*arrays):
    m = None
    for a in arrays:
        r = jnp.max(jnp.max(a.astype(F32), axis=0, keepdims=True), axis=1, keepdims=True)
        m = r if m is None else jnp.maximum(m, r)
    bits = lax.bitcast_convert_type(m, jnp.uint32)
    return lax.bitcast_convert_type((bits >> 16) >> 16, F32)


FF_CHUNK = MXU_TILE
FFN_SIDE_PIECES = 8


def _ffn_body(n_tiles, emit_next, xp_ref, xe_ref, g_ref, wg_ref, wu_ref, wd_ref, *refs):
    y_ref = refs[0]
    h_ref, acc_ref = refs[-2:]
    s = pl.program_id(0)
    slot = s % 2
    tm = y_ref.shape[0]

    def pre_norm(dst, rows):
        h_ref[dst, rows, :] = _rms(xp_ref[rows, :], g_ref[0:1, :]).astype(BF16)
        return [h_ref[dst, rows, :]]

    def finish(rows):
        y = xe_ref[rows, :] + 0.5 * _rms(acc_ref[rows, :], g_ref[1:2, :])
        y_ref[rows, :] = y
        stored = [y_ref[rows, :]]
        if emit_next:
            refs[1][rows, :] = _rms(y, g_ref[2:3, :]).astype(BF16)
            stored.append(refs[1][rows, :])
        return stored

    @pl.when(s == 0)
    def _():
        pre_norm(0, slice(None))
        acc_ref[...] = jnp.zeros_like(acc_ref)

    @pl.when(jnp.logical_and(s >= 1, s <= n_tiles))
    def _():
        h = h_ref[1 - slot]
        acc = None
        anchor = None
        piece_rows = tm // FFN_SIDE_PIECES
        assert D_FF // FF_CHUNK > FFN_SIDE_PIECES
        for ci in range(D_FF // FF_CHUNK):
            cols = slice(ci * FF_CHUNK, (ci + 1) * FF_CHUNK)
            gate = _dot(h, wg_ref[:, cols])
            up = _dot(h, wu_ref[:, cols])
            if anchor is not None:
                up = up + anchor
            act = (jax.nn.silu(gate) * up).astype(BF16)
            part = _dot(act, wd_ref[cols, :])
            acc = part if acc is None else acc + part
            anchor = None
            if ci < FFN_SIDE_PIECES:
                rows = slice(ci * piece_rows, (ci + 1) * piece_rows)
                anchor = _ordering_zero(*(finish(rows) + pre_norm(slot, rows)))
        acc_ref[...] = acc

    @pl.when(s == n_tiles + 1)
    def _():
        finish(slice(None))


def _ffn(x, gains, wg, wu, wd, *, tm, emit_next, name):
    t = x.shape[0]
    assert t % tm == 0
    n_tiles = t // tm
    head = lambda s: (jnp.minimum(s, n_tiles - 1), 0)
    tail = lambda s: (jnp.clip(s - 2, 0, n_tiles - 1), 0)
    out_shape = [jax.ShapeDtypeStruct((t, D_MODEL), F32)]
    out_specs = [pl.BlockSpec((tm, D_MODEL), tail)]
    if emit_next:
        out_shape.append(jax.ShapeDtypeStruct((t, D_MODEL), BF16))
        out_specs.append(pl.BlockSpec((tm, D_MODEL), tail))
    return pl.pallas_call(
        functools.partial(_ffn_body, n_tiles, emit_next),
        grid=(n_tiles + 2,),
        in_specs=[pl.BlockSpec((tm, D_MODEL), head), pl.BlockSpec((tm, D_MODEL), tail),
                  _resident(gains.shape), _resident(wg.shape), _resident(wu.shape), _resident(wd.shape)],
        out_specs=out_specs,
        out_shape=out_shape,
        scratch_shapes=[pltpu.VMEM((2, tm, D_MODEL), BF16), pltpu.VMEM((tm, D_MODEL), F32)],
        compiler_params=_params(1),
        name=name,
    )(x, x, gains, wg, wu, wd)


def _gelu(x):
    return 0.5 * x * (1.0 + lax.erf(x * np.sqrt(0.5).astype(np.float32)))


def _layer_norm(x, g, b):
    mu = jnp.mean(x, axis=-1, keepdims=True)
    xc = x - mu
    var = jnp.mean(xc * xc, axis=-1, keepdims=True)
    return xc * lax.rsqrt(var + EPS) * g + b


IN_PROJ_LN_PIECES = 8


def _in_proj_body(q_blocked, h_ref, w_ref, ln_ref, q_ref, kv_ref, u_ref, vn_ref, ga_ref, gb_ref, gv_ref):
    h = h_ref[...]
    tm = h.shape[0]
    n_chunks = D_MODEL // MXU_TILE
    q_scale = HEAD_DIM ** -0.5 * (LOG2E if q_blocked else 1.0)

    def store_q(c, r):
        q = (r * q_scale).astype(q_ref.dtype)
        if q_blocked:
            for b in range(tm // QBLK):
                for j in range(MXU_TILE // LANES):
                    p = c * (MXU_TILE // LANES) + j
                    q_ref[b, p * QBLK:(p + 1) * QBLK, :] = q[b * QBLK:(b + 1) * QBLK, j * LANES:(j + 1) * LANES]
        else:
            q_ref[:, c * MXU_TILE:(c + 1) * MXU_TILE] = q

    def store_cols(ref, fn):
        def store(c, r):
            ref[:, c * MXU_TILE:(c + 1) * MXU_TILE] = fn(r).astype(ref.dtype)
        return store

    def layer_norm_piece(k):
        rows = slice(k * (tm // IN_PROJ_LN_PIECES), (k + 1) * (tm // IN_PROJ_LN_PIECES))
        vn_ref[rows, :] = _layer_norm(_gelu(gv_ref[rows, :]), ln_ref[0:1, :], ln_ref[1:2, :]).astype(vn_ref.dtype)

    def run(off, c, store):
        store(c, _dot(h, w_ref[:, off + c * MXU_TILE: off + (c + 1) * MXU_TILE]))

    store_gv = lambda c, r: gv_ref.__setitem__((slice(None), slice(c * MXU_TILE, (c + 1) * MXU_TILE)), r)
    store_u, store_ga, store_gb = (store_cols(u_ref, _gelu), store_cols(ga_ref, jax.nn.sigmoid),
                                   store_cols(gb_ref, jax.nn.sigmoid))
    for c in range(n_chunks):
        run(OFF_GV, c, store_gv)
    run(OFF_KV, 0, lambda c, r: kv_ref.__setitem__(Ellipsis, r))
    for c in range(n_chunks):
        run(OFF_Q, c, store_q)
        layer_norm_piece(2 * c)
        run(OFF_GA, c, store_ga)
        layer_norm_piece(2 * c + 1)
    for c in range(n_chunks):
        run(OFF_U, c, store_u)
        run(OFF_GB, c, store_gb)


def _in_proj(h, w_in, ln, *, tm, vn_dtype, q_blocked, name):
    t = h.shape[0]
    assert t % tm == 0
    row = lambda i: (i, 0)
    wide = lambda dt: jax.ShapeDtypeStruct((t, D_MODEL), dt)
    wide_spec = pl.BlockSpec((tm, D_MODEL), row)
    if q_blocked:
        assert tm % QBLK == 0
        q_shape = jax.ShapeDtypeStruct((t // QBLK, N_HEAD_PAIRS * QBLK, LANES), BF16)
        q_spec = pl.BlockSpec((tm // QBLK, N_HEAD_PAIRS * QBLK, LANES), lambda i: (i, 0, 0))
    else:
        q_shape, q_spec = wide(BF16), wide_spec
    return pl.pallas_call(
        functools.partial(_in_proj_body, q_blocked),
        grid=(t // tm,),
        in_specs=[wide_spec, _resident(w_in.shape), _resident(ln.shape)],
        out_specs=[q_spec, pl.BlockSpec((tm, 2 * KV_WIDTH), row),
                   wide_spec, wide_spec, wide_spec, wide_spec],
        out_shape=[q_shape, jax.ShapeDtypeStruct((t, 2 * KV_WIDTH), F32),
                   wide(BF16), wide(vn_dtype), wide(BF16), wide(BF16)],
        scratch_shapes=[pltpu.VMEM((tm, GMLP_WIDTH), F32)],
        compiler_params=_params(1),
        name=name,
    )(h, w_in, ln)


def _t5_bucket(rel):
    half = NUM_BUCKETS // 2
    max_exact = half // 2
    ret = jnp.where(rel > 0, half, 0)
    n = jnp.abs(rel)
    nf = jnp.maximum(n, 1).astype(F32)
    large = max_exact + (jnp.log(nf / max_exact) / math.log(MAX_DISTANCE / max_exact)
                         * (half - max_exact)).astype(jnp.int32)
    large = jnp.minimum(large, half - 1)
    return ret + jnp.where(n < max_exact, n, large)


def _bias_body(table_ref, bucket_ref, out_ref):
    bucket = bucket_ref[...]
    hits = [bucket == b for b in range(NUM_BUCKETS)]
    for head in range(N_Q_HEADS):
        acc = jnp.zeros(bucket.shape, F32)
        for b in range(NUM_BUCKETS):
            acc = jnp.where(hits[b], table_ref[b, head], acc)
        out_ref[head] = acc


def _prompt_bias_body(table_ref, bucket_ref, out_ref):
    bucket = bucket_ref[...]
    hits = [bucket == b for b in range(NUM_BUCKETS)]
    key_chunk = lax.broadcasted_iota(jnp.int32, bucket.shape, 0) // CHUNK
    q_chunk = lax.broadcasted_iota(jnp.int32, bucket.shape, 1) // CHUNK
    in_window = jnp.logical_and(key_chunk >= q_chunk, key_chunk <= q_chunk + WINDOW // CHUNK)
    valid = [in_window, jnp.logical_and(in_window, key_chunk >= WINDOW // CHUNK)]
    for head in range(N_Q_HEADS):
        acc = jnp.zeros(bucket.shape, F32)
        for b in range(NUM_BUCKETS):
            acc = jnp.where(hits[b], table_ref[b, head], acc)
        h, rest = divmod(head, Q_PER_KV)
        pair, par = divmod(rest, HEADS_PER_TILE)
        pg, pl_ = divmod(pair, GROUP_PAIRS)
        for v in range(2):
            out_ref[v, h, pg, par * KBLK:(par + 1) * KBLK, pl_ * QBLK:(pl_ + 1) * QBLK] = (
                jnp.where(valid[v], acc * LOG2E, NEG_INF))


def _prompt_bias(table):
    rel = (jnp.arange(KBLK) - WINDOW)[:, None] - jnp.arange(QBLK)[None, :]
    bucket = _t5_bucket(rel).astype(jnp.int32)
    return pl.pallas_call(
        _prompt_bias_body,
        in_specs=[pl.BlockSpec(memory_space=pltpu.SMEM),
                  pl.BlockSpec(memory_space=pltpu.VMEM)],
        out_specs=pl.BlockSpec(memory_space=pltpu.VMEM),
        out_shape=jax.ShapeDtypeStruct((2, N_KV_HEADS, PAIR_GROUPS, 2 * KBLK, GROUP_PAIRS * QBLK), F32),
        name="rel_bias_prompt",
    )(table, bucket)


def _relative_bias(table, n_q, n_keys, n_past):
    rel = (jnp.arange(n_keys) - n_past)[None, :] - jnp.arange(n_q)[:, None]
    bucket = _t5_bucket(rel).astype(jnp.int32)
    bias = pl.pallas_call(
        _bias_body,
        in_specs=[pl.BlockSpec(memory_space=pltpu.SMEM),
                  pl.BlockSpec(memory_space=pltpu.VMEM)],
        out_specs=pl.BlockSpec(memory_space=pltpu.VMEM),
        out_shape=jax.ShapeDtypeStruct((N_Q_HEADS, n_q, n_keys), F32),
        name="rel_bias_%d" % n_q,
    )(table, bucket)
    return bias.reshape(N_KV_HEADS, Q_PER_KV * n_q, n_keys)


def _sink_attention(qh, kh, vh, bias, sink, invalid=None):
    s = lax.dot_general(qh, kh, (((1,), (1,)), ((), ())), preferred_element_type=F32) + bias
    if invalid is not None:
        s = jnp.where(invalid, NEG_INF, s)
    m = jnp.maximum(jnp.max(s, axis=-1, keepdims=True), sink)
    p = jnp.exp(s - m)
    denom = jnp.sum(p, axis=-1, keepdims=True) + jnp.exp(sink - m)
    return _dot(p.astype(BF16), vh) / denom


def _stack_heads(q, kv_head):
    base = kv_head * Q_PER_KV * HEAD_DIM
    return jnp.concatenate(
        [q[:, base + g * HEAD_DIM: base + (g + 1) * HEAD_DIM] for g in range(Q_PER_KV)], axis=0)


def _unstack_heads(o, n):
    return jnp.concatenate([o[g * n:(g + 1) * n, :] for g in range(Q_PER_KV)], axis=1)


def _merge_out(x1, attn, gm, ga, gb, wba_ref, wbg_ref, wo_ref, g_post):
    merged = ga * _dot(attn, wba_ref[...]) + gb * _dot(gm, wbg_ref[...])
    return x1 + _rms(_dot(merged.astype(BF16), wo_ref[...]), g_post)


def _prompt_mixer_body(ts, q_ref, kv_ref, kvp_ref, u_ref, vn_ref, ga_ref, gb_ref, x1_ref,
                       bias_ref, sink_ref, ws_ref, bs_ref, wba_ref, wbg_ref, wo_ref, g_ref,
                       out_ref, attn_t_ref, gm_ref):
    first_variant = jnp.where(pl.program_id(1) == 0, 1, 0)
    kv_all = jnp.concatenate([kvp_ref[0], kv_ref[0]], axis=0)
    k_all = kv_all[:, :KV_WIDTH]
    v_t = jnp.transpose(kv_all[:, KV_WIDTH:]).astype(BF16)
    low = lax.broadcasted_iota(jnp.int32, k_all.shape, 1) < HEAD_DIM
    k_swapped = pltpu.roll(k_all, HEAD_DIM, axis=1)
    zero = jnp.zeros_like(k_all)
    k_par = [[jnp.where(low, k_all, zero), jnp.where(low, zero, k_swapped)],
             [jnp.where(low, k_swapped, zero), jnp.where(low, zero, k_all)]]
    k_par = [[k.astype(BF16) for k in ks] for ks in k_par]

    units = [(blk, h, pg) for blk in range(ts // QBLK) for h in range(N_KV_HEADS)
             for pg in range(PAIR_GROUPS)]

    def scores(blk, h, pg):
        keys = slice(blk * QBLK, blk * QBLK + KBLK)
        k_blk = jnp.concatenate([k_par[h][0][keys], k_par[h][1][keys]], axis=0)
        pair0 = h * PAIRS_PER_KV + pg * GROUP_PAIRS
        qa = q_ref[blk, pair0 * QBLK:(pair0 + GROUP_PAIRS) * QBLK, :]
        s = lax.dot_general(k_blk, qa, (((1,), (1,)), ((), ())), preferred_element_type=F32)
        variant = first_variant if blk == 0 else 0
        return s + bias_ref[variant, h, pg]

    def softmax(s, blk, h, pg):
        out = []
        for par in range(HEADS_PER_TILE):
            sp = s[par * KBLK:(par + 1) * KBLK]
            row = (h * PAIR_GROUPS + pg) * HEADS_PER_TILE + par
            sink = sink_ref[row:row + 1, :] * LOG2E
            m = jnp.maximum(jnp.max(sp, axis=0, keepdims=True), sink)
            p = jnp.exp2(sp - m)
            denom = jnp.sum(p, axis=0, keepdims=True) + jnp.exp2(sink - m)
            out.append((p.astype(BF16), 1.0 / denom))
        return out

    def weighted_values(probs, blk, h, pg):
        keys = slice(blk * QBLK, blk * QBLK + KBLK)
        cols = slice(blk * QBLK, (blk + 1) * QBLK)
        vh_t = v_t[h * HEAD_DIM:(h + 1) * HEAD_DIM, keys]
        pair0 = h * PAIRS_PER_KV + pg * GROUP_PAIRS
        for par, (p, inv) in enumerate(probs):
            o = _dot(vh_t, p) * inv
            for pl_ in range(GROUP_PAIRS):
                head = (pair0 + pl_) * HEADS_PER_TILE + par
                attn_t_ref[head * HEAD_DIM:(head + 1) * HEAD_DIM, cols] = o[:, pl_ * QBLK:(pl_ + 1) * QBLK]

    blk_i = lax.broadcasted_iota(jnp.int32, (GMLP_CHUNK, GMLP_CHUNK), 0) // CHUNK
    blk_j = lax.broadcasted_iota(jnp.int32, (GMLP_CHUNK, GMLP_CHUNK), 1) // CHUNK
    for g in range(GMLP_GROUPS):
        w = jnp.where(blk_j <= blk_i, ws_ref[g], 0.0).astype(BF16)
        b = bs_ref[:, g:g + 1]
        cols = slice(g * GMLP_GROUP_DIM, (g + 1) * GMLP_GROUP_DIM)
        for c in range(ts // GMLP_CHUNK):
            rows = slice(c * GMLP_CHUNK, (c + 1) * GMLP_CHUNK)
            sp = _dot(w, vn_ref[0, rows, cols]) + b
            gm_ref[rows, cols] = (u_ref[0, rows, cols].astype(F32) * sp).astype(BF16)

    def gmlp_branch(c):
        cols = slice(c * SIDE_COLS, (c + 1) * SIDE_COLS)
        return gb_ref[0, :, cols].astype(F32) * _dot(gm_ref[...], wbg_ref[:, cols])

    n_side = D_MODEL // SIDE_COLS
    side = []
    s_vals, p_vals = {}, {}
    for i in range(len(units) + 2):
        if i < len(units):
            s_vals[i] = scores(*units[i])
        if 1 <= i <= len(units):
            p_vals[i - 1] = softmax(s_vals.pop(i - 1), *units[i - 1])
        if i >= 2:
            weighted_values(p_vals.pop(i - 2), *units[i - 2])
        if i % (len(units) // n_side) == 1 and len(side) < n_side:
            side.append(gmlp_branch(len(side)))
    assert len(side) == n_side

    attn = jnp.transpose(attn_t_ref[...]).astype(BF16)
    merged = ga_ref[0].astype(F32) * _dot(attn, wba_ref[...]) + jnp.concatenate(side, axis=1)
    out_ref[0] = x1_ref[0] + _rms(_dot(merged.astype(BF16), wo_ref[...]), g_ref[...])


def _prompt_mixer(q, kv, u, vn, ga, gb, x1, bias, sink, w_s, b_s_t, wba, wbg, wo, g_post, *, ts):
    batch, seq, _ = kv.shape
    assert seq % ts == 0 and ts % QBLK == 0 and QBLK == GMLP_CHUNK == WINDOW
    tile = lambda b, t: (b, t, 0)
    prev = lambda b, t: (b, jnp.maximum(t * (ts // WINDOW) - 1, 0), 0)
    wide = pl.BlockSpec((1, ts, D_MODEL), tile)
    n_t = seq // ts
    q_spec = pl.BlockSpec((ts // QBLK, N_HEAD_PAIRS * QBLK, LANES), lambda b, t: (b * n_t + t, 0, 0))
    return pl.pallas_call(
        functools.partial(_prompt_mixer_body, ts),
        grid=(batch, n_t),
        in_specs=[q_spec, pl.BlockSpec((1, ts, 2 * KV_WIDTH), tile),
                  pl.BlockSpec((1, WINDOW, 2 * KV_WIDTH), prev),
                  wide, wide, wide, wide, wide,
                  _resident(bias.shape), _resident(sink.shape), _resident(w_s.shape),
                  _resident(b_s_t.shape), _resident(wba.shape), _resident(wbg.shape),
                  _resident(wo.shape), _resident(g_post.shape)],
        out_specs=wide,
        out_shape=jax.ShapeDtypeStruct((batch, seq, D_MODEL), F32),
        scratch_shapes=[pltpu.VMEM((ATTN_WIDTH, ts), F32), pltpu.VMEM((ts, GMLP_WIDTH), BF16)],
        compiler_params=_params(2),
        name="prompt_mixer",
    )(q, kv, kv, u, vn, ga, gb, x1, bias, sink, w_s, b_s_t, wba, wbg, wo, g_post)


def _sample_mixer_body(n_batch, n_new, q_ref, kv_ref, ck_ref, cv_ref, u_ref, vn_ref, ga_ref, gb_ref,
                       x1_ref, bias_ref, sink_ref, ws_ref, bs_ref, wba_ref, wbg_ref, wo_ref, g_ref,
                       out_ref, attn_ref, gm_ref):
    for b in range(n_batch):
        rows = slice(b * n_new, (b + 1) * n_new)
        q = q_ref[rows, :]
        kv = kv_ref[rows, :]
        k_all = jnp.concatenate([ck_ref[b], kv[:, :KV_WIDTH]], axis=0).astype(BF16)
        v_all = jnp.concatenate([cv_ref[b], kv[:, KV_WIDTH:]], axis=0).astype(BF16)
        for h in range(N_KV_HEADS):
            cols = slice(h * HEAD_DIM, (h + 1) * HEAD_DIM)
            o = _sink_attention(_stack_heads(q, h), k_all[:, cols], v_all[:, cols],
                                bias_ref[h], sink_ref[h])
            width = Q_PER_KV * HEAD_DIM
            attn_ref[rows, h * width:(h + 1) * width] = _unstack_heads(o, n_new).astype(BF16)
        for g in range(GMLP_GROUPS):
            cols = slice(g * GMLP_GROUP_DIM, (g + 1) * GMLP_GROUP_DIM)
            w = ws_ref[g, :n_new, :n_new].astype(BF16)
            sp = _dot(w, vn_ref[rows, cols].astype(BF16)) + bs_ref[:n_new, g:g + 1]
            gm_ref[rows, cols] = (u_ref[rows, cols].astype(F32) * sp).astype(BF16)

    out_ref[...] = _merge_out(x1_ref[...], attn_ref[...], gm_ref[...],
                              ga_ref[...].astype(F32), gb_ref[...].astype(F32),
                              wba_ref, wbg_ref, wo_ref, g_ref[...])


def _sample_mixer(q, kv, cache_k, cache_v, u, vn, ga, gb, x1, bias, sink, w_s, b_s_t,
                  wba, wbg, wo, g_post, *, n_batch, n_new):
    assert n_new <= CHUNK
    t = q.shape[0]
    vmem = pl.BlockSpec(memory_space=pltpu.VMEM)
    return pl.pallas_call(
        functools.partial(_sample_mixer_body, n_batch, n_new),
        in_specs=[vmem] * 17,
        out_specs=vmem,
        out_shape=jax.ShapeDtypeStruct((t, D_MODEL), F32),
        scratch_shapes=[pltpu.VMEM((t, ATTN_WIDTH), BF16), pltpu.VMEM((t, GMLP_WIDTH), BF16)],
        compiler_params=pltpu.CompilerParams(vmem_limit_bytes=VMEM_LIMIT_BYTES),
        name="sample_mixer",
    )(q, kv, cache_k, cache_v, u, vn, ga, gb, x1, bias, sink, w_s, b_s_t, wba, wbg, wo, g_post)


def _prompt_sink_rows(sinks):
    s = sinks.astype(F32).reshape(N_KV_HEADS, PAIR_GROUPS, GROUP_PAIRS, HEADS_PER_TILE)
    s = jnp.transpose(s, (0, 1, 3, 2))[..., None]
    s = jnp.broadcast_to(s, (N_KV_HEADS, PAIR_GROUPS, HEADS_PER_TILE, GROUP_PAIRS, QBLK))
    return s.reshape(N_KV_HEADS * PAIR_GROUPS * HEADS_PER_TILE, GROUP_PAIRS * QBLK)


def _sink_rows(sinks, n_q):
    s = jnp.broadcast_to(sinks.astype(F32).reshape(N_KV_HEADS, Q_PER_KV, 1), (N_KV_HEADS, Q_PER_KV, n_q))
    return s.reshape(N_KV_HEADS, Q_PER_KV * n_q, 1)


PROMPT_TM = 512
PROMPT_TS = 512


def kernel(x_prompt, x_sample, cache_win_k, cache_win_v, rel_bias_table, norm_gains, ffn1_w_gate, ffn1_w_up, ffn1_w_down, w_in, attn_sinks, gmlp_ln_g, gmlp_ln_b, gmlp_w_s, gmlp_b_s, w_branch_attn, w_branch_gmlp, w_out, ffn2_w_gate, ffn2_w_up, ffn2_w_down):
    depth = norm_gains.shape[0]
    batch, seq, _ = x_prompt.shape
    dec_batch, dec_seq, _ = x_sample.shape
    n_cache = cache_win_k.shape[2]
    assert seq % PROMPT_TS == 0 and dec_seq <= CHUNK

    bias_p = _prompt_bias(rel_bias_table)
    bias_s = _relative_bias(rel_bias_table, dec_seq, n_cache + dec_seq, n_cache)

    xp = x_prompt.reshape(batch * seq, D_MODEL)
    xs = x_sample.reshape(dec_batch * dec_seq, D_MODEL)
    t_s = xs.shape[0]
    kp, vp, ks, vs, gs = [], [], [], [], []
    for l in range(depth):
        g = norm_gains[l].astype(F32)
        wg1, wu1, wg2, wu2 = _to_bf16(ffn1_w_gate[l], ffn1_w_up[l], ffn2_w_gate[l], ffn2_w_up[l])
        wd1, wd2 = _to_bf16(ffn1_w_down[l], ffn2_w_down[l])
        w1, w2 = (wg1, wu1, wd1), (wg2, wu2, wd2)
        (win,) = _to_bf16(w_in[l])
        wba, wbg, wo = _to_bf16(w_branch_attn[l], w_branch_gmlp[l], w_out[l])
        ln = jnp.stack([gmlp_ln_g[l], gmlp_ln_b[l]]).astype(F32)
        w_s = gmlp_w_s[l].astype(F32)
        b_s_t = jnp.transpose(gmlp_b_s[l]).astype(F32)
        g_ffn1, g_post2, g_ffn2 = g[0:3], g[3:4], jnp.concatenate([g[4:6], g[5:6]])
        sink_p = _prompt_sink_rows(attn_sinks[l])
        sink_s = _sink_rows(attn_sinks[l], dec_seq)

        x1, h2 = _ffn(xp, g_ffn1, *w1, tm=PROMPT_TM, emit_next=True, name="ffn1_prompt")
        q, kv, u, vn, ga, gb = _in_proj(h2, win, ln, tm=2 * PROMPT_TM, vn_dtype=BF16, q_blocked=True,
                                        name="in_proj_prompt")
        b3 = lambda a: a.reshape(batch, seq, a.shape[-1])
        x2 = _prompt_mixer(q, b3(kv), b3(u), b3(vn), b3(ga), b3(gb), b3(x1), bias_p, sink_p,
                           w_s, b_s_t, wba, wbg, wo, g_post2, ts=PROMPT_TS)
        (xp,) = _ffn(x2.reshape(batch * seq, D_MODEL), g_ffn2, *w2, tm=PROMPT_TM, emit_next=False,
                     name="ffn2_prompt")
        kv_win = b3(kv)[:, seq - WINDOW:, :]
        kp.append(kv_win[..., :KV_WIDTH].reshape(batch, WINDOW, N_KV_HEADS, HEAD_DIM))
        vp.append(kv_win[..., KV_WIDTH:].reshape(batch, WINDOW, N_KV_HEADS, HEAD_DIM))

        x1, h2 = _ffn(xs, g_ffn1, *w1, tm=t_s, emit_next=True, name="ffn1_sample")
        q, kv, u, vn, ga, gb = _in_proj(h2, win, ln, tm=t_s, vn_dtype=F32, q_blocked=False,
                                        name="in_proj_sample")
        ck = cache_win_k[l].reshape(dec_batch, n_cache, KV_WIDTH)
        cv = cache_win_v[l].reshape(dec_batch, n_cache, KV_WIDTH)
        x2 = _sample_mixer(q, kv, ck, cv, u, vn, ga, gb, x1, bias_s, sink_s, w_s, b_s_t,
                           wba, wbg, wo, g_post2, n_batch=dec_batch, n_new=dec_seq)
        (xs,) = _ffn(x2, g_ffn2, *w2, tm=t_s, emit_next=False, name="ffn2_sample")
        ks.append(kv[:, :KV_WIDTH].reshape(dec_batch, dec_seq, N_KV_HEADS, HEAD_DIM))
        vs.append(kv[:, KV_WIDTH:].reshape(dec_batch, dec_seq, N_KV_HEADS, HEAD_DIM))
        gs.append(vn.reshape(dec_batch, dec_seq, GMLP_WIDTH))

    return (xp.reshape(batch, seq, D_MODEL), xs.reshape(dec_batch, dec_seq, D_MODEL),
            jnp.stack(kp), jnp.stack(vp), jnp.stack(ks), jnp.stack(vs), jnp.stack(gs))
```

```python
import functools
import math

import jax
import jax.numpy as jnp
import numpy as np
from jax import lax
from jax.experimental import pallas as pl
from jax.experimental.pallas import tpu as pltpu

D_MODEL = 1024
CHUNK = 64
N_Q_HEADS = 16
N_KV_HEADS = 2
HEAD_DIM = 64
Q_PER_KV = N_Q_HEADS // N_KV_HEADS
ATTN_WIDTH = N_Q_HEADS * HEAD_DIM
KV_WIDTH = N_KV_HEADS * HEAD_DIM
WINDOW = 128
GMLP_WIDTH = 1024
GMLP_GROUPS = 4
GMLP_GROUP_DIM = GMLP_WIDTH // GMLP_GROUPS
GMLP_CHUNK = 128
NUM_BUCKETS = 32
MAX_DISTANCE = 128
D_FF = 2816
EPS = 1e-6
NEG_INF = -1e30
LOG2E = math.log2(math.e)

OFF_Q = 0
OFF_KV = ATTN_WIDTH
OFF_U = OFF_KV + 2 * KV_WIDTH
OFF_GV = OFF_U + GMLP_WIDTH
OFF_GA = OFF_GV + GMLP_WIDTH
OFF_GB = OFF_GA + D_MODEL

V7X_VMEM_BYTES = 64 * 1024 * 1024
VMEM_LIMIT_BYTES = V7X_VMEM_BYTES - 8 * 1024 * 1024
MXU_TILE = 256
LANES = 128

QBLK = 2 * CHUNK
KBLK = WINDOW + QBLK
HEADS_PER_TILE = LANES // HEAD_DIM
N_HEAD_PAIRS = N_Q_HEADS // HEADS_PER_TILE
PAIRS_PER_KV = Q_PER_KV // HEADS_PER_TILE
GROUP_PAIRS = 4
PAIR_GROUPS = PAIRS_PER_KV // GROUP_PAIRS
SIDE_COLS = MXU_TILE

BF16 = jnp.bfloat16
F32 = jnp.float32


def _dot(a, b):
    return jnp.dot(a, b, preferred_element_type=F32)


def _rms(x, g):
    return x * lax.rsqrt(jnp.mean(x * x, axis=-1, keepdims=True) + EPS) * g


def _resident(shape):
    zeros = (0,) * len(shape)
    return pl.BlockSpec(shape, lambda *_: zeros, pipeline_mode=pl.Buffered(1))


def _params(n_axes):
    return pltpu.CompilerParams(dimension_semantics=("arbitrary",) * n_axes,
                                vmem_limit_bytes=VMEM_LIMIT_BYTES)


CAST_BLOCK_BYTES = 2 * 1024 * 1024


def _cast_body(*refs):
    n = len(refs) // 2
    for w_ref, o_ref in zip(refs[:n], refs[n:]):
        o_ref[...] = w_ref[...].astype(o_ref.dtype)


def _to_bf16(*ws):
    k, n = ws[0].shape
    assert all(w.shape == (k, n) for w in ws)
    rows = [r for r in range(16, k + 1, 16) if k % r == 0 and r * n * 4 <= CAST_BLOCK_BYTES]
    bk = max(rows) if rows else k
    spec = pl.BlockSpec((bk, n), lambda i: (i, 0))
    return pl.pallas_call(
        _cast_body,
        grid=(k // bk,),
        in_specs=[spec] * len(ws),
        out_specs=[spec] * len(ws),
        out_shape=[jax.ShapeDtypeStruct((k, n), BF16)] * len(ws),
        compiler_params=_params(1),
        name="weights_to_bf16",
    )(*ws)


def _ordering_zero(*arrays):
    m = None
    for a in arrays:
        r = jnp.max(jnp.max(a.astype(F32), axis=0, keepdims=True), axis=1, keepdims=True)
        m = r if m is None else jnp.maximum(m, r)
    bits = lax.bitcast_convert_type(m, jnp.uint32)
    return lax.bitcast_convert_type((bits >> 16) >> 16, F32)


FF_CHUNK = MXU_TILE
FFN_SIDE_PIECES = 8


def _ffn_body(n_tiles, emit_next, xp_ref, xe_ref, g_ref, wg_ref, wu_ref, wd_ref, *refs):
    y_ref = refs[0]
    h_ref, acc_ref, act_ref = refs[-3:]
    s = pl.program_id(0)
    slot = s % 2
    tm = y_ref.shape[0]

    def pre_norm(dst, rows):
        h_ref[dst, rows, :] = _rms(xp_ref[rows, :], g_ref[0:1, :]).astype(BF16)
        return [h_ref[dst, rows, :]]

    def finish(rows):
        y = xe_ref[rows, :] + 0.5 * _rms(acc_ref[rows, :], g_ref[1:2, :])
        y_ref[rows, :] = y
        stored = [y_ref[rows, :]]
        if emit_next:
            refs[1][rows, :] = _rms(y, g_ref[2:3, :]).astype(BF16)
            stored.append(refs[1][rows, :])
        return stored

    @pl.when(s == 0)
    def _():
        pre_norm(0, slice(None))
        acc_ref[...] = jnp.zeros_like(acc_ref)

    @pl.when(jnp.logical_and(s >= 1, s <= n_tiles))
    def _():
        h = h_ref[1 - slot]
        acc = None
        anchor = None
        piece_rows = tm // FFN_SIDE_PIECES
        assert D_FF // FF_CHUNK > FFN_SIDE_PIECES
        for ci in range(D_FF // FF_CHUNK):
            cols = slice(ci * FF_CHUNK, (ci + 1) * FF_CHUNK)
            gate = _dot(h, wg_ref[:, cols])
            up = _dot(h, wu_ref[:, cols])
            if anchor is not None:
                up = up + anchor
            act_ref[:, cols] = (jax.nn.silu(gate) * up).astype(BF16)
            anchor = None
            if ci < FFN_SIDE_PIECES:
                rows = slice(ci * piece_rows, (ci + 1) * piece_rows)
                anchor = _ordering_zero(*(finish(rows) + pre_norm(slot, rows)))
        acc_ref[...] = _dot(act_ref[...], wd_ref[...])

    @pl.when(s == n_tiles + 1)
    def _():
        finish(slice(None))


def _ffn(x, gains, wg, wu, wd, *, tm, emit_next, name):
    t = x.shape[0]
    assert t % tm == 0
    n_tiles = t // tm
    head = lambda s: (jnp.minimum(s, n_tiles - 1), 0)
    tail = lambda s: (jnp.clip(s - 2, 0, n_tiles - 1), 0)
    out_shape = [jax.ShapeDtypeStruct((t, D_MODEL), F32)]
    out_specs = [pl.BlockSpec((tm, D_MODEL), tail)]
    if emit_next:
        out_shape.append(jax.ShapeDtypeStruct((t, D_MODEL), BF16))
        out_specs.append(pl.BlockSpec((tm, D_MODEL), tail))
    return pl.pallas_call(
        functools.partial(_ffn_body, n_tiles, emit_next),
        grid=(n_tiles + 2,),
        in_specs=[pl.BlockSpec((tm, D_MODEL), head), pl.BlockSpec((tm, D_MODEL), tail),
                  _resident(gains.shape), _resident(wg.shape), _resident(wu.shape), _resident(wd.shape)],
        out_specs=out_specs,
        out_shape=out_shape,
        scratch_shapes=[pltpu.VMEM((2, tm, D_MODEL), BF16), pltpu.VMEM((tm, D_MODEL), F32),
                        pltpu.VMEM((tm, D_FF), BF16)],
        compiler_params=_params(1),
        name=name,
    )(x, x, gains, wg, wu, wd)


def _gelu(x):
    return 0.5 * x * (1.0 + lax.erf(x * np.sqrt(0.5).astype(np.float32)))


def _layer_norm(x, g, b):
    mu = jnp.mean(x, axis=-1, keepdims=True)
    xc = x - mu
    var = jnp.mean(xc * xc, axis=-1, keepdims=True)
    return xc * lax.rsqrt(var + EPS) * g + b


IN_PROJ_LN_PIECES = 8


def _in_proj_body(q_blocked, h_ref, w_ref, ln_ref, q_ref, kv_ref, u_ref, vn_ref, ga_ref, gb_ref, gv_ref):
    h = h_ref[...]
    tm = h.shape[0]
    n_chunks = D_MODEL // MXU_TILE
    q_scale = HEAD_DIM ** -0.5 * (LOG2E if q_blocked else 1.0)

    def store_q(c, r):
        q = (r * q_scale).astype(q_ref.dtype)
        if q_blocked:
            for b in range(tm // QBLK):
                for j in range(MXU_TILE // LANES):
                    p = c * (MXU_TILE // LANES) + j
                    q_ref[b, p * QBLK:(p + 1) * QBLK, :] = q[b * QBLK:(b + 1) * QBLK, j * LANES:(j + 1) * LANES]
        else:
            q_ref[:, c * MXU_TILE:(c + 1) * MXU_TILE] = q

    def store_cols(ref, fn):
        def store(c, r):
            ref[:, c * MXU_TILE:(c + 1) * MXU_TILE] = fn(r).astype(ref.dtype)
        return store

    def layer_norm_piece(k):
        rows = slice(k * (tm // IN_PROJ_LN_PIECES), (k + 1) * (tm // IN_PROJ_LN_PIECES))
        vn_ref[rows, :] = _layer_norm(_gelu(gv_ref[rows, :]), ln_ref[0:1, :], ln_ref[1:2, :]).astype(vn_ref.dtype)

    def run(off, c, store):
        store(c, _dot(h, w_ref[:, off + c * MXU_TILE: off + (c + 1) * MXU_TILE]))

    store_gv = lambda c, r: gv_ref.__setitem__((slice(None), slice(c * MXU_TILE, (c + 1) * MXU_TILE)), r)
    store_u, store_ga, store_gb = (store_cols(u_ref, _gelu), store_cols(ga_ref, jax.nn.sigmoid),
                                   store_cols(gb_ref, jax.nn.sigmoid))
    for c in range(n_chunks):
        run(OFF_GV, c, store_gv)
    run(OFF_KV, 0, lambda c, r: kv_ref.__setitem__(Ellipsis, r))
    for c in range(n_chunks):
        run(OFF_Q, c, store_q)
        layer_norm_piece(2 * c)
        run(OFF_GA, c, store_ga)
        layer_norm_piece(2 * c + 1)
    for c in range(n_chunks):
        run(OFF_U, c, store_u)
        run(OFF_GB, c, store_gb)


def _in_proj(h, w_in, ln, *, tm, vn_dtype, q_blocked, name):
    t = h.shape[0]
    assert t % tm == 0
    row = lambda i: (i, 0)
    wide = lambda dt: jax.ShapeDtypeStruct((t, D_MODEL), dt)
    wide_spec = pl.BlockSpec((tm, D_MODEL), row)
    if q_blocked:
        assert tm % QBLK == 0
        q_shape = jax.ShapeDtypeStruct((t // QBLK, N_HEAD_PAIRS * QBLK, LANES), BF16)
        q_spec = pl.BlockSpec((tm // QBLK, N_HEAD_PAIRS * QBLK, LANES), lambda i: (i, 0, 0))
    else:
        q_shape, q_spec = wide(BF16), wide_spec
    return pl.pallas_call(
        functools.partial(_in_proj_body, q_blocked),
        grid=(t // tm,),
        in_specs=[wide_spec, _resident(w_in.shape), _resident(ln.shape)],
        out_specs=[q_spec, pl.BlockSpec((tm, 2 * KV_WIDTH), row),
                   wide_spec, wide_spec, wide_spec, wide_spec],
        out_shape=[q_shape, jax.ShapeDtypeStruct((t, 2 * KV_WIDTH), F32),
                   wide(BF16), wide(vn_dtype), wide(BF16), wide(BF16)],
        scratch_shapes=[pltpu.VMEM((tm, GMLP_WIDTH), F32)],
        compiler_params=_params(1),
        name=name,
    )(h, w_in, ln)


def _t5_bucket(rel):
    half = NUM_BUCKETS // 2
    max_exact = half // 2
    ret = jnp.where(rel > 0, half, 0)
    n = jnp.abs(rel)
    nf = jnp.maximum(n, 1).astype(F32)
    large = max_exact + (jnp.log(nf / max_exact) / math.log(MAX_DISTANCE / max_exact)
                         * (half - max_exact)).astype(jnp.int32)
    large = jnp.minimum(large, half - 1)
    return ret + jnp.where(n < max_exact, n, large)


def _bias_body(table_ref, bucket_ref, out_ref):
    bucket = bucket_ref[...]
    hits = [bucket == b for b in range(NUM_BUCKETS)]
    for head in range(N_Q_HEADS):
        acc = jnp.zeros(bucket.shape, F32)
        for b in range(NUM_BUCKETS):
            acc = jnp.where(hits[b], table_ref[b, head], acc)
        out_ref[head] = acc


def _prompt_bias_body(table_ref, bucket_ref, out_ref):
    bucket = bucket_ref[...]
    hits = [bucket == b for b in range(NUM_BUCKETS)]
    key_chunk = lax.broadcasted_iota(jnp.int32, bucket.shape, 0) // CHUNK
    q_chunk = lax.broadcasted_iota(jnp.int32, bucket.shape, 1) // CHUNK
    in_window = jnp.logical_and(key_chunk >= q_chunk, key_chunk <= q_chunk + WINDOW // CHUNK)
    valid = [in_window, jnp.logical_and(in_window, key_chunk >= WINDOW // CHUNK)]
    for head in range(N_Q_HEADS):
        acc = jnp.zeros(bucket.shape, F32)
        for b in range(NUM_BUCKETS):
            acc = jnp.where(hits[b], table_ref[b, head], acc)
        h, rest = divmod(head, Q_PER_KV)
        pair, par = divmod(rest, HEADS_PER_TILE)
        pg, pl_ = divmod(pair, GROUP_PAIRS)
        for v in range(2):
            out_ref[v, h, pg, par * KBLK:(par + 1) * KBLK, pl_ * QBLK:(pl_ + 1) * QBLK] = (
                jnp.where(valid[v], acc * LOG2E, NEG_INF))


def _prompt_bias(table):
    rel = (jnp.arange(KBLK) - WINDOW)[:, None] - jnp.arange(QBLK)[None, :]
    bucket = _t5_bucket(rel).astype(jnp.int32)
    return pl.pallas_call(
        _prompt_bias_body,
        in_specs=[pl.BlockSpec(memory_space=pltpu.SMEM),
                  pl.BlockSpec(memory_space=pltpu.VMEM)],
        out_specs=pl.BlockSpec(memory_space=pltpu.VMEM),
        out_shape=jax.ShapeDtypeStruct((2, N_KV_HEADS, PAIR_GROUPS, 2 * KBLK, GROUP_PAIRS * QBLK), F32),
        name="rel_bias_prompt",
    )(table, bucket)


def _relative_bias(table, n_q, n_keys, n_past):
    rel = (jnp.arange(n_keys) - n_past)[None, :] - jnp.arange(n_q)[:, None]
    bucket = _t5_bucket(rel).astype(jnp.int32)
    bias = pl.pallas_call(
        _bias_body,
        in_specs=[pl.BlockSpec(memory_space=pltpu.SMEM),
                  pl.BlockSpec(memory_space=pltpu.VMEM)],
        out_specs=pl.BlockSpec(memory_space=pltpu.VMEM),
        out_shape=jax.ShapeDtypeStruct((N_Q_HEADS, n_q, n_keys), F32),
        name="rel_bias_%d" % n_q,
    )(table, bucket)
    return bias.reshape(N_KV_HEADS, Q_PER_KV * n_q, n_keys)


def _sink_attention(qh, kh, vh, bias, sink, invalid=None):
    s = lax.dot_general(qh, kh, (((1,), (1,)), ((), ())), preferred_element_type=F32) + bias
    if invalid is not None:
        s = jnp.where(invalid, NEG_INF, s)
    m = jnp.maximum(jnp.max(s, axis=-1, keepdims=True), sink)
    p = jnp.exp(s - m)
    denom = jnp.sum(p, axis=-1, keepdims=True) + jnp.exp(sink - m)
    return _dot(p.astype(BF16), vh) / denom


def _stack_heads(q, kv_head):
    base = kv_head * Q_PER_KV * HEAD_DIM
    return jnp.concatenate(
        [q[:, base + g * HEAD_DIM: base + (g + 1) * HEAD_DIM] for g in range(Q_PER_KV)], axis=0)


def _unstack_heads(o, n):
    return jnp.concatenate([o[g * n:(g + 1) * n, :] for g in range(Q_PER_KV)], axis=1)


def _merge_out(x1, attn, gm, ga, gb, wba_ref, wbg_ref, wo_ref, g_post):
    merged = ga * _dot(attn, wba_ref[...]) + gb * _dot(gm, wbg_ref[...])
    return x1 + _rms(_dot(merged.astype(BF16), wo_ref[...]), g_post)


def _prompt_mixer_body(ts, q_ref, kv_ref, kvp_ref, u_ref, vn_ref, ga_ref, gb_ref, x1_ref,
                       bias_ref, sink_ref, ws_ref, bs_ref, wba_ref, wbg_ref, wo_ref, g_ref,
                       out_ref, attn_t_ref, gm_ref):
    first_variant = jnp.where(pl.program_id(1) == 0, 1, 0)
    kv_all = jnp.concatenate([kvp_ref[0], kv_ref[0]], axis=0)
    k_all = kv_all[:, :KV_WIDTH]
    v_t = jnp.transpose(kv_all[:, KV_WIDTH:]).astype(BF16)
    low = lax.broadcasted_iota(jnp.int32, k_all.shape, 1) < HEAD_DIM
    k_swapped = pltpu.roll(k_all, HEAD_DIM, axis=1)
    zero = jnp.zeros_like(k_all)
    k_par = [[jnp.where(low, k_all, zero), jnp.where(low, zero, k_swapped)],
             [jnp.where(low, k_swapped, zero), jnp.where(low, zero, k_all)]]
    k_par = [[k.astype(BF16) for k in ks] for ks in k_par]

    units = [(blk, h, pg) for blk in range(ts // QBLK) for h in range(N_KV_HEADS)
             for pg in range(PAIR_GROUPS)]

    def scores(blk, h, pg):
        keys = slice(blk * QBLK, blk * QBLK + KBLK)
        k_blk = jnp.concatenate([k_par[h][0][keys], k_par[h][1][keys]], axis=0)
        pair0 = h * PAIRS_PER_KV + pg * GROUP_PAIRS
        qa = q_ref[blk, pair0 * QBLK:(pair0 + GROUP_PAIRS) * QBLK, :]
        s = lax.dot_general(k_blk, qa, (((1,), (1,)), ((), ())), preferred_element_type=F32)
        variant = first_variant if blk == 0 else 0
        return s + bias_ref[variant, h, pg]

    def softmax(s, blk, h, pg):
        out = []
        for par in range(HEADS_PER_TILE):
            sp = s[par * KBLK:(par + 1) * KBLK]
            row = (h * PAIR_GROUPS + pg) * HEADS_PER_TILE + par
            sink = sink_ref[row:row + 1, :] * LOG2E
            m = jnp.maximum(jnp.max(sp, axis=0, keepdims=True), sink)
            p = jnp.exp2(sp - m)
            denom = jnp.sum(p, axis=0, keepdims=True) + jnp.exp2(sink - m)
            out.append((p.astype(BF16), 1.0 / denom))
        return out

    def weighted_values(probs, blk, h, pg):
        keys = slice(blk * QBLK, blk * QBLK + KBLK)
        cols = slice(blk * QBLK, (blk + 1) * QBLK)
        vh_t = v_t[h * HEAD_DIM:(h + 1) * HEAD_DIM, keys]
        pair0 = h * PAIRS_PER_KV + pg * GROUP_PAIRS
        for par, (p, inv) in enumerate(probs):
            o = _dot(vh_t, p) * inv
            for pl_ in range(GROUP_PAIRS):
                head = (pair0 + pl_) * HEADS_PER_TILE + par
                attn_t_ref[head * HEAD_DIM:(head + 1) * HEAD_DIM, cols] = o[:, pl_ * QBLK:(pl_ + 1) * QBLK]

    blk_i = lax.broadcasted_iota(jnp.int32, (GMLP_CHUNK, GMLP_CHUNK), 0) // CHUNK
    blk_j = lax.broadcasted_iota(jnp.int32, (GMLP_CHUNK, GMLP_CHUNK), 1) // CHUNK
    for g in range(GMLP_GROUPS):
        w = jnp.where(blk_j <= blk_i, ws_ref[g], 0.0).astype(BF16)
        b = bs_ref[:, g:g + 1]
        cols = slice(g * GMLP_GROUP_DIM, (g + 1) * GMLP_GROUP_DIM)
        for c in range(ts // GMLP_CHUNK):
            rows = slice(c * GMLP_CHUNK, (c + 1) * GMLP_CHUNK)
            sp = _dot(w, vn_ref[0, rows, cols]) + b
            gm_ref[rows, cols] = (u_ref[0, rows, cols].astype(F32) * sp).astype(BF16)

    def gmlp_branch(c):
        cols = slice(c * SIDE_COLS, (c + 1) * SIDE_COLS)
        return gb_ref[0, :, cols].astype(F32) * _dot(gm_ref[...], wbg_ref[:, cols])

    n_side = D_MODEL // SIDE_COLS
    side = []
    s_vals, p_vals = {}, {}
    for i in range(len(units) + 2):
        if i < len(units):
            s_vals[i] = scores(*units[i])
        if 1 <= i <= len(units):
            p_vals[i - 1] = softmax(s_vals.pop(i - 1), *units[i - 1])
        if i >= 2:
            weighted_values(p_vals.pop(i - 2), *units[i - 2])
        if i % (len(units) // n_side) == 1 and len(side) < n_side:
            side.append(gmlp_branch(len(side)))
    assert len(side) == n_side

    attn = jnp.transpose(attn_t_ref[...]).astype(BF16)
    merged = ga_ref[0].astype(F32) * _dot(attn, wba_ref[...]) + jnp.concatenate(side, axis=1)
    out_ref[0] = x1_ref[0] + _rms(_dot(merged.astype(BF16), wo_ref[...]), g_ref[...])


def _prompt_mixer(q, kv, u, vn, ga, gb, x1, bias, sink, w_s, b_s_t, wba, wbg, wo, g_post, *, ts):
    batch, seq, _ = kv.shape
    assert seq % ts == 0 and ts % QBLK == 0 and QBLK == GMLP_CHUNK == WINDOW
    tile = lambda b, t: (b, t, 0)
    prev = lambda b, t: (b, jnp.maximum(t * (ts // WINDOW) - 1, 0), 0)
    wide = pl.BlockSpec((1, ts, D_MODEL), tile)
    n_t = seq // ts
    q_spec = pl.BlockSpec((ts // QBLK, N_HEAD_PAIRS * QBLK, LANES), lambda b, t: (b * n_t + t, 0, 0))
    return pl.pallas_call(
        functools.partial(_prompt_mixer_body, ts),
        grid=(batch, n_t),
        in_specs=[q_spec, pl.BlockSpec((1, ts, 2 * KV_WIDTH), tile),
                  pl.BlockSpec((1, WINDOW, 2 * KV_WIDTH), prev),
                  wide, wide, wide, wide, wide,
                  _resident(bias.shape), _resident(sink.shape), _resident(w_s.shape),
                  _resident(b_s_t.shape), _resident(wba.shape), _resident(wbg.shape),
                  _resident(wo.shape), _resident(g_post.shape)],
        out_specs=wide,
        out_shape=jax.ShapeDtypeStruct((batch, seq, D_MODEL), F32),
        scratch_shapes=[pltpu.VMEM((ATTN_WIDTH, ts), F32), pltpu.VMEM((ts, GMLP_WIDTH), BF16)],
        compiler_params=_params(2),
        name="prompt_mixer",
    )(q, kv, kv, u, vn, ga, gb, x1, bias, sink, w_s, b_s_t, wba, wbg, wo, g_post)


def _sample_mixer_body(n_batch, n_new, q_ref, kv_ref, ck_ref, cv_ref, u_ref, vn_ref, ga_ref, gb_ref,
                       x1_ref, bias_ref, sink_ref, ws_ref, bs_ref, wba_ref, wbg_ref, wo_ref, g_ref,
                       out_ref, attn_ref, gm_ref):
    for b in range(n_batch):
        rows = slice(b * n_new, (b + 1) * n_new)
        q = q_ref[rows, :]
        kv = kv_ref[rows, :]
        k_all = jnp.concatenate([ck_ref[b], kv[:, :KV_WIDTH]], axis=0).astype(BF16)
        v_all = jnp.concatenate([cv_ref[b], kv[:, KV_WIDTH:]], axis=0).astype(BF16)
        for h in range(N_KV_HEADS):
            cols = slice(h * HEAD_DIM, (h + 1) * HEAD_DIM)
            o = _sink_attention(_stack_heads(q, h), k_all[:, cols], v_all[:, cols],
                                bias_ref[h], sink_ref[h])
            width = Q_PER_KV * HEAD_DIM
            attn_ref[rows, h * width:(h + 1) * width] = _unstack_heads(o, n_new).astype(BF16)
        for g in range(GMLP_GROUPS):
            cols = slice(g * GMLP_GROUP_DIM, (g + 1) * GMLP_GROUP_DIM)
            w = ws_ref[g, :n_new, :n_new].astype(BF16)
            sp = _dot(w, vn_ref[rows, cols].astype(BF16)) + bs_ref[:n_new, g:g + 1]
            gm_ref[rows, cols] = (u_ref[rows, cols].astype(F32) * sp).astype(BF16)

    out_ref[...] = _merge_out(x1_ref[...], attn_ref[...], gm_ref[...],
                              ga_ref[...].astype(F32), gb_ref[...].astype(F32),
                              wba_ref, wbg_ref, wo_ref, g_ref[...])


def _sample_mixer(q, kv, cache_k, cache_v, u, vn, ga, gb, x1, bias, sink, w_s, b_s_t,
                  wba, wbg, wo, g_post, *, n_batch, n_new):
    assert n_new <= CHUNK
    t = q.shape[0]
    vmem = pl.BlockSpec(memory_space=pltpu.VMEM)
    return pl.pallas_call(
        functools.partial(_sample_mixer_body, n_batch, n_new),
        in_specs=[vmem] * 17,
        out_specs=vmem,
        out_shape=jax.ShapeDtypeStruct((t, D_MODEL), F32),
        scratch_shapes=[pltpu.VMEM((t, ATTN_WIDTH), BF16), pltpu.VMEM((t, GMLP_WIDTH), BF16)],
        compiler_params=pltpu.CompilerParams(vmem_limit_bytes=VMEM_LIMIT_BYTES),
        name="sample_mixer",
    )(q, kv, cache_k, cache_v, u, vn, ga, gb, x1, bias, sink, w_s, b_s_t, wba, wbg, wo, g_post)


def _prompt_sink_rows(sinks):
    s = sinks.astype(F32).reshape(N_KV_HEADS, PAIR_GROUPS, GROUP_PAIRS, HEADS_PER_TILE)
    s = jnp.transpose(s, (0, 1, 3, 2))[..., None]
    s = jnp.broadcast_to(s, (N_KV_HEADS, PAIR_GROUPS, HEADS_PER_TILE, GROUP_PAIRS, QBLK))
    return s.reshape(N_KV_HEADS * PAIR_GROUPS * HEADS_PER_TILE, GROUP_PAIRS * QBLK)


def _sink_rows(sinks, n_q):
    s = jnp.broadcast_to(sinks.astype(F32).reshape(N_KV_HEADS, Q_PER_KV, 1), (N_KV_HEADS, Q_PER_KV, n_q))
    return s.reshape(N_KV_HEADS, Q_PER_KV * n_q, 1)


PROMPT_TM = 512
PROMPT_TS = 512


def kernel(x_prompt, x_sample, cache_win_k, cache_win_v, rel_bias_table, norm_gains, ffn1_w_gate, ffn1_w_up, ffn1_w_down, w_in, attn_sinks, gmlp_ln_g, gmlp_ln_b, gmlp_w_s, gmlp_b_s, w_branch_attn, w_branch_gmlp, w_out, ffn2_w_gate, ffn2_w_up, ffn2_w_down):
    depth = norm_gains.shape[0]
    batch, seq, _ = x_prompt.shape
    dec_batch, dec_seq, _ = x_sample.shape
    n_cache = cache_win_k.shape[2]
    assert seq % PROMPT_TS == 0 and dec_seq <= CHUNK

    bias_p = _prompt_bias(rel_bias_table)
    bias_s = _relative_bias(rel_bias_table, dec_seq, n_cache + dec_seq, n_cache)

    xp = x_prompt.reshape(batch * seq, D_MODEL)
    xs = x_sample.reshape(dec_batch * dec_seq, D_MODEL)
    t_s = xs.shape[0]
    kp, vp, ks, vs, gs = [], [], [], [], []
    for l in range(depth):
        g = norm_gains[l].astype(F32)
        wg1, wu1, wg2, wu2 = _to_bf16(ffn1_w_gate[l], ffn1_w_up[l], ffn2_w_gate[l], ffn2_w_up[l])
        wd1, wd2 = _to_bf16(ffn1_w_down[l], ffn2_w_down[l])
        w1, w2 = (wg1, wu1, wd1), (wg2, wu2, wd2)
        (win,) = _to_bf16(w_in[l])
        wba, wbg, wo = _to_bf16(w_branch_attn[l], w_branch_gmlp[l], w_out[l])
        ln = jnp.stack([gmlp_ln_g[l], gmlp_ln_b[l]]).astype(F32)
        w_s = gmlp_w_s[l].astype(F32)
        b_s_t = jnp.transpose(gmlp_b_s[l]).astype(F32)
        g_ffn1, g_post2, g_ffn2 = g[0:3], g[3:4], jnp.concatenate([g[4:6], g[5:6]])
        sink_p = _prompt_sink_rows(attn_sinks[l])
        sink_s = _sink_rows(attn_sinks[l], dec_seq)

        x1, h2 = _ffn(xp, g_ffn1, *w1, tm=PROMPT_TM, emit_next=True, name="ffn1_prompt")
        q, kv, u, vn, ga, gb = _in_proj(h2, win, ln, tm=2 * PROMPT_TM, vn_dtype=BF16, q_blocked=True,
                                        name="in_proj_prompt")
        b3 = lambda a: a.reshape(batch, seq, a.shape[-1])
        x2 = _prompt_mixer(q, b3(kv), b3(u), b3(vn), b3(ga), b3(gb), b3(x1), bias_p, sink_p,
                           w_s, b_s_t, wba, wbg, wo, g_post2, ts=PROMPT_TS)
        (xp,) = _ffn(x2.reshape(batch * seq, D_MODEL), g_ffn2, *w2, tm=PROMPT_TM, emit_next=False,
                     name="ffn2_prompt")
        kv_win = b3(kv)[:, seq - WINDOW:, :]
        kp.append(kv_win[..., :KV_WIDTH].reshape(batch, WINDOW, N_KV_HEADS, HEAD_DIM))
        vp.append(kv_win[..., KV_WIDTH:].reshape(batch, WINDOW, N_KV_HEADS, HEAD_DIM))

        x1, h2 = _ffn(xs, g_ffn1, *w1, tm=t_s, emit_next=True, name="ffn1_sample")
        q, kv, u, vn, ga, gb = _in_proj(h2, win, ln, tm=t_s, vn_dtype=F32, q_blocked=False,
                                        name="in_proj_sample")
        ck = cache_win_k[l].reshape(dec_batch, n_cache, KV_WIDTH)
        cv = cache_win_v[l].reshape(dec_batch, n_cache, KV_WIDTH)
        x2 = _sample_mixer(q, kv, ck, cv, u, vn, ga, gb, x1, bias_s, sink_s, w_s, b_s_t,
                           wba, wbg, wo, g_post2, n_batch=dec_batch, n_new=dec_seq)
        (xs,) = _ffn(x2, g_ffn2, *w2, tm=t_s, emit_next=False, name="ffn2_sample")
        ks.append(kv[:, :KV_WIDTH].reshape(dec_batch, dec_seq, N_KV_HEADS, HEAD_DIM))
        vs.append(kv[:, KV_WIDTH:].reshape(dec_batch, dec_seq, N_KV_HEADS, HEAD_DIM))
        gs.append(vn.reshape(dec_batch, dec_seq, GMLP_WIDTH))

    return (xp.reshape(batch, seq, D_MODEL), xs.reshape(dec_batch, dec_seq, D_MODEL),
            jnp.stack(kp), jnp.stack(vp), jnp.stack(ks), jnp.stack(vs), jnp.stack(gs))
```

```python
import functools
import math

import jax
import jax.numpy as jnp
import numpy as np
from jax import lax
from jax.experimental import pallas as pl
from jax.experimental.pallas import tpu as pltpu

D_MODEL = 1024
CHUNK = 64
N_Q_HEADS = 16
N_KV_HEADS = 2
HEAD_DIM = 64
Q_PER_KV = N_Q_HEADS // N_KV_HEADS
ATTN_WIDTH = N_Q_HEADS * HEAD_DIM
KV_WIDTH = N_KV_HEADS * HEAD_DIM
WINDOW = 128
GMLP_WIDTH = 1024
GMLP_GROUPS = 4
GMLP_GROUP_DIM = GMLP_WIDTH // GMLP_GROUPS
GMLP_CHUNK = 128
NUM_BUCKETS = 32
MAX_DISTANCE = 128
D_FF = 2816
EPS = 1e-6
NEG_INF = -1e30
LOG2E = math.log2(math.e)

OFF_Q = 0
OFF_KV = ATTN_WIDTH
OFF_U = OFF_KV + 2 * KV_WIDTH
OFF_GV = OFF_U + GMLP_WIDTH
OFF_GA = OFF_GV + GMLP_WIDTH
OFF_GB = OFF_GA + D_MODEL

V7X_VMEM_BYTES = 64 * 1024 * 1024
VMEM_LIMIT_BYTES = V7X_VMEM_BYTES - 8 * 1024 * 1024
MXU_TILE = 256
LANES = 128

QBLK = 2 * CHUNK
KBLK = WINDOW + QBLK
HEADS_PER_TILE = LANES // HEAD_DIM
N_HEAD_PAIRS = N_Q_HEADS // HEADS_PER_TILE
PAIRS_PER_KV = Q_PER_KV // HEADS_PER_TILE
GROUP_PAIRS = 4
PAIR_GROUPS = PAIRS_PER_KV // GROUP_PAIRS
SIDE_COLS = MXU_TILE

BF16 = jnp.bfloat16
F32 = jnp.float32


def _dot(a, b):
    return jnp.dot(a, b, preferred_element_type=F32)


def _rms(x, g):
    return x * lax.rsqrt(jnp.mean(x * x, axis=-1, keepdims=True) + EPS) * g


def _resident(shape):
    zeros = (0,) * len(shape)
    return pl.BlockSpec(shape, lambda *_: zeros, pipeline_mode=pl.Buffered(1))


def _params(n_axes):
    return pltpu.CompilerParams(dimension_semantics=("arbitrary",) * n_axes,
                                vmem_limit_bytes=VMEM_LIMIT_BYTES)


CAST_BLOCK_BYTES = 2 * 1024 * 1024


def _cast_body(*refs):
    n = len(refs) // 2
    for w_ref, o_ref in zip(refs[:n], refs[n:]):
        o_ref[...] = w_ref[...].astype(o_ref.dtype)


def _to_bf16(*ws):
    k, n = ws[0].shape
    assert all(w.shape == (k, n) for w in ws)
    rows = [r for r in range(16, k + 1, 16) if k % r == 0 and r * n * 4 <= CAST_BLOCK_BYTES]
    bk = max(rows) if rows else k
    spec = pl.BlockSpec((bk, n), lambda i: (i, 0))
    return pl.pallas_call(
        _cast_body,
        grid=(k // bk,),
        in_specs=[spec] * len(ws),
        out_specs=[spec] * len(ws),
        out_shape=[jax.ShapeDtypeStruct((k, n), BF16)] * len(ws),
        compiler_params=_params(1),
        name="weights_to_bf16",
    )(*ws)


def _ordering_zero(*arrays):
    m = None
    for a in arrays:
        r = jnp.max(jnp.max(a.astype(F32), axis=0, keepdims=True), axis=1, keepdims=True)
        m = r if m is None else jnp.maximum(m, r)
    bits = lax.bitcast_convert_type(m, jnp.uint32)
    return lax.bitcast_convert_type((bits >> 16) >> 16, F32)


FF_CHUNK = MXU_TILE
FFN_SIDE_PIECES = 8


def _ffn_body(n_tiles, emit_next, xp_ref, xe_ref, g_ref, wg_ref, wu_ref, wd_ref, *refs):
    y_ref = refs[0]
    h_ref, acc_ref, act_ref = refs[-3:]
    s = pl.program_id(0)
    slot = s % 2
    tm = y_ref.shape[0]

    def pre_norm(dst, rows):
        h_ref[dst, rows, :] = _rms(xp_ref[rows, :], g_ref[0:1, :]).astype(BF16)
        return [h_ref[dst, rows, :]]

    def finish(rows):
        y = xe_ref[rows, :] + 0.5 * _rms(acc_ref[rows, :], g_ref[1:2, :])
        y_ref[rows, :] = y
        stored = [y_ref[rows, :]]
        if emit_next:
            refs[1][rows, :] = _rms(y, g_ref[2:3, :]).astype(BF16)
            stored.append(refs[1][rows, :])
        return stored

    @pl.when(s == 0)
    def _():
        pre_norm(0, slice(None))
        acc_ref[...] = jnp.zeros_like(acc_ref)

    @pl.when(jnp.logical_and(s >= 1, s <= n_tiles))
    def _():
        h = h_ref[1 - slot]
        acc = None
        anchor = None
        piece_rows = tm // FFN_SIDE_PIECES
        assert D_FF // FF_CHUNK > FFN_SIDE_PIECES
        for ci in range(D_FF // FF_CHUNK):
            cols = slice(ci * FF_CHUNK, (ci + 1) * FF_CHUNK)
            gate = _dot(h, wg_ref[:, cols])
            up = _dot(h, wu_ref[:, cols])
            if anchor is not None:
                up = up + anchor
            act_ref[:, cols] = (jax.nn.silu(gate) * up).astype(BF16)
            anchor = None
            if ci < FFN_SIDE_PIECES:
                rows = slice(ci * piece_rows, (ci + 1) * piece_rows)
                anchor = _ordering_zero(*(finish(rows) + pre_norm(slot, rows)))
        acc_ref[...] = _dot(act_ref[...], wd_ref[...])

    @pl.when(s == n_tiles + 1)
    def _():
        finish(slice(None))


def _ffn(x, gains, wg, wu, wd, *, tm, emit_next, name):
    t = x.shape[0]
    assert t % tm == 0
    n_tiles = t // tm
    head = lambda s: (jnp.minimum(s, n_tiles - 1), 0)
    tail = lambda s: (jnp.clip(s - 2, 0, n_tiles - 1), 0)
    out_shape = [jax.ShapeDtypeStruct((t, D_MODEL), F32)]
    out_specs = [pl.BlockSpec((tm, D_MODEL), tail)]
    if emit_next:
        out_shape.append(jax.ShapeDtypeStruct((t, D_MODEL), BF16))
        out_specs.append(pl.BlockSpec((tm, D_MODEL), tail))
    return pl.pallas_call(
        functools.partial(_ffn_body, n_tiles, emit_next),
        grid=(n_tiles + 2,),
        in_specs=[pl.BlockSpec((tm, D_MODEL), head), pl.BlockSpec((tm, D_MODEL), tail),
                  _resident(gains.shape), _resident(wg.shape), _resident(wu.shape), _resident(wd.shape)],
        out_specs=out_specs,
        out_shape=out_shape,
        scratch_shapes=[pltpu.VMEM((2, tm, D_MODEL), BF16), pltpu.VMEM((tm, D_MODEL), F32),
                        pltpu.VMEM((tm, D_FF), BF16)],
        compiler_params=_params(1),
        name=name,
    )(x, x, gains, wg, wu, wd)


def _ffn_stream_body(emit_next, x_ref, g_ref, wg_ref, wu_ref, wd_ref, *refs):
    n_out = 2 if emit_next else 1
    y_ref = refs[0]
    wg16_ref, wu16_ref, wd16_ref = refs[n_out:n_out + 3]
    h_ref, acc_ref = refs[-2:]
    c = pl.program_id(0)

    @pl.when(c == 0)
    def _():
        h_ref[...] = _rms(x_ref[...], g_ref[0:1, :]).astype(BF16)
        acc_ref[...] = jnp.zeros_like(acc_ref)

    wg, wu, wd = (r[...].astype(BF16) for r in (wg_ref, wu_ref, wd_ref))
    wg16_ref[...], wu16_ref[...], wd16_ref[...] = wg, wu, wd
    h = h_ref[...]
    act = (jax.nn.silu(_dot(h, wg)) * _dot(h, wu)).astype(BF16)
    acc_ref[...] += _dot(act, wd)

    @pl.when(c == pl.num_programs(0) - 1)
    def _():
        y = x_ref[...] + 0.5 * _rms(acc_ref[...], g_ref[1:2, :])
        y_ref[...] = y
        if emit_next:
            refs[1][...] = _rms(y, g_ref[2:3, :]).astype(BF16)


def _ffn_stream(x, gains, wg, wu, wd, *, emit_next, name):
    t = x.shape[0]
    whole = lambda shape: pl.BlockSpec(shape, lambda c: (0, 0))
    col = pl.BlockSpec((D_MODEL, FF_CHUNK), lambda c: (0, c))
    row = pl.BlockSpec((FF_CHUNK, D_MODEL), lambda c: (c, 0))
    out_shape = [jax.ShapeDtypeStruct((t, D_MODEL), F32)]
    if emit_next:
        out_shape.append(jax.ShapeDtypeStruct((t, D_MODEL), BF16))
    out_specs = [whole((t, D_MODEL))] * len(out_shape) + [col, col, row]
    out_shape += [jax.ShapeDtypeStruct(w.shape, BF16) for w in (wg, wu, wd)]
    return pl.pallas_call(
        functools.partial(_ffn_stream_body, emit_next),
        grid=(D_FF // FF_CHUNK,),
        in_specs=[whole((t, D_MODEL)), whole(gains.shape), col, col, row],
        out_specs=out_specs,
        out_shape=out_shape,
        scratch_shapes=[pltpu.VMEM((t, D_MODEL), BF16), pltpu.VMEM((t, D_MODEL), F32)],
        compiler_params=_params(1),
        name=name,
    )(x, gains, wg, wu, wd)


def _gelu(x):
    return 0.5 * x * (1.0 + lax.erf(x * np.sqrt(0.5).astype(np.float32)))


def _layer_norm(x, g, b):
    mu = jnp.mean(x, axis=-1, keepdims=True)
    xc = x - mu
    var = jnp.mean(xc * xc, axis=-1, keepdims=True)
    return xc * lax.rsqrt(var + EPS) * g + b


IN_PROJ_LN_PIECES = 8


def _in_proj_body(q_blocked, h_ref, w_ref, ln_ref, q_ref, kv_ref, u_ref, vn_ref, ga_ref, gb_ref, gv_ref):
    h = h_ref[...]
    tm = h.shape[0]
    n_chunks = D_MODEL // MXU_TILE
    q_scale = HEAD_DIM ** -0.5 * (LOG2E if q_blocked else 1.0)

    def store_q(c, r):
        q = (r * q_scale).astype(q_ref.dtype)
        if q_blocked:
            for b in range(tm // QBLK):
                for j in range(MXU_TILE // LANES):
                    p = c * (MXU_TILE // LANES) + j
                    q_ref[b, p * QBLK:(p + 1) * QBLK, :] = q[b * QBLK:(b + 1) * QBLK, j * LANES:(j + 1) * LANES]
        else:
            q_ref[:, c * MXU_TILE:(c + 1) * MXU_TILE] = q

    def store_cols(ref, fn):
        def store(c, r):
            ref[:, c * MXU_TILE:(c + 1) * MXU_TILE] = fn(r).astype(ref.dtype)
        return store

    def layer_norm_piece(k):
        rows = slice(k * (tm // IN_PROJ_LN_PIECES), (k + 1) * (tm // IN_PROJ_LN_PIECES))
        vn_ref[rows, :] = _layer_norm(_gelu(gv_ref[rows, :]), ln_ref[0:1, :], ln_ref[1:2, :]).astype(vn_ref.dtype)

    def run(off, c, store):
        store(c, _dot(h, w_ref[:, off + c * MXU_TILE: off + (c + 1) * MXU_TILE]))

    store_gv = lambda c, r: gv_ref.__setitem__((slice(None), slice(c * MXU_TILE, (c + 1) * MXU_TILE)), r)
    store_u, store_ga, store_gb = (store_cols(u_ref, _gelu), store_cols(ga_ref, jax.nn.sigmoid),
                                   store_cols(gb_ref, jax.nn.sigmoid))
    for c in range(n_chunks):
        run(OFF_GV, c, store_gv)
    run(OFF_KV, 0, lambda c, r: kv_ref.__setitem__(Ellipsis, r))
    for c in range(n_chunks):
        run(OFF_Q, c, store_q)
        layer_norm_piece(2 * c)
        run(OFF_GA, c, store_ga)
        layer_norm_piece(2 * c + 1)
    for c in range(n_chunks):
        run(OFF_U, c, store_u)
        run(OFF_GB, c, store_gb)


def _in_proj(h, w_in, ln, *, tm, vn_dtype, q_blocked, name):
    t = h.shape[0]
    assert t % tm == 0
    row = lambda i: (i, 0)
    wide = lambda dt: jax.ShapeDtypeStruct((t, D_MODEL), dt)
    wide_spec = pl.BlockSpec((tm, D_MODEL), row)
    if q_blocked:
        assert tm % QBLK == 0
        q_shape = jax.ShapeDtypeStruct((t // QBLK, N_HEAD_PAIRS * QBLK, LANES), BF16)
        q_spec = pl.BlockSpec((tm // QBLK, N_HEAD_PAIRS * QBLK, LANES), lambda i: (i, 0, 0))
    else:
        q_shape, q_spec = wide(BF16), wide_spec
    return pl.pallas_call(
        functools.partial(_in_proj_body, q_blocked),
        grid=(t // tm,),
        in_specs=[wide_spec, _resident(w_in.shape), _resident(ln.shape)],
        out_specs=[q_spec, pl.BlockSpec((tm, 2 * KV_WIDTH), row),
                   wide_spec, wide_spec, wide_spec, wide_spec],
        out_shape=[q_shape, jax.ShapeDtypeStruct((t, 2 * KV_WIDTH), F32),
                   wide(BF16), wide(vn_dtype), wide(BF16), wide(BF16)],
        scratch_shapes=[pltpu.VMEM((tm, GMLP_WIDTH), F32)],
        compiler_params=_params(1),
        name=name,
    )(h, w_in, ln)


def _t5_bucket(rel):
    half = NUM_BUCKETS // 2
    max_exact = half // 2
    ret = jnp.where(rel > 0, half, 0)
    n = jnp.abs(rel)
    nf = jnp.maximum(n, 1).astype(F32)
    large = max_exact + (jnp.log(nf / max_exact) / math.log(MAX_DISTANCE / max_exact)
                         * (half - max_exact)).astype(jnp.int32)
    large = jnp.minimum(large, half - 1)
    return ret + jnp.where(n < max_exact, n, large)


def _bias_body(table_ref, bucket_ref, out_ref):
    bucket = bucket_ref[...]
    hits = [bucket == b for b in range(NUM_BUCKETS)]
    for head in range(N_Q_HEADS):
        acc = jnp.zeros(bucket.shape, F32)
        for b in range(NUM_BUCKETS):
            acc = jnp.where(hits[b], table_ref[b, head], acc)
        out_ref[head] = acc


def _prompt_bias_body(table_ref, bucket_ref, out_ref):
    bucket = bucket_ref[...]
    hits = [bucket == b for b in range(NUM_BUCKETS)]
    key_chunk = lax.broadcasted_iota(jnp.int32, bucket.shape, 0) // CHUNK
    q_chunk = lax.broadcasted_iota(jnp.int32, bucket.shape, 1) // CHUNK
    in_window = jnp.logical_and(key_chunk >= q_chunk, key_chunk <= q_chunk + WINDOW // CHUNK)
    valid = [in_window, jnp.logical_and(in_window, key_chunk >= WINDOW // CHUNK)]
    for head in range(N_Q_HEADS):
        acc = jnp.zeros(bucket.shape, F32)
        for b in range(NUM_BUCKETS):
            acc = jnp.where(hits[b], table_ref[b, head], acc)
        h, rest = divmod(head, Q_PER_KV)
        pair, par = divmod(rest, HEADS_PER_TILE)
        pg, pl_ = divmod(pair, GROUP_PAIRS)
        for v in range(2):
            out_ref[v, h, pg, par * KBLK:(par + 1) * KBLK, pl_ * QBLK:(pl_ + 1) * QBLK] = (
                jnp.where(valid[v], acc * LOG2E, NEG_INF))


def _prompt_bias(table):
    rel = (jnp.arange(KBLK) - WINDOW)[:, None] - jnp.arange(QBLK)[None, :]
    bucket = _t5_bucket(rel).astype(jnp.int32)
    return pl.pallas_call(
        _prompt_bias_body,
        in_specs=[pl.BlockSpec(memory_space=pltpu.SMEM),
                  pl.BlockSpec(memory_space=pltpu.VMEM)],
        out_specs=pl.BlockSpec(memory_space=pltpu.VMEM),
        out_shape=jax.ShapeDtypeStruct((2, N_KV_HEADS, PAIR_GROUPS, 2 * KBLK, GROUP_PAIRS * QBLK), F32),
        name="rel_bias_prompt",
    )(table, bucket)


def _relative_bias(table, n_q, n_keys, n_past):
    rel = (jnp.arange(n_keys) - n_past)[None, :] - jnp.arange(n_q)[:, None]
    bucket = _t5_bucket(rel).astype(jnp.int32)
    bias = pl.pallas_call(
        _bias_body,
        in_specs=[pl.BlockSpec(memory_space=pltpu.SMEM),
                  pl.BlockSpec(memory_space=pltpu.VMEM)],
        out_specs=pl.BlockSpec(memory_space=pltpu.VMEM),
        out_shape=jax.ShapeDtypeStruct((N_Q_HEADS, n_q, n_keys), F32),
        name="rel_bias_%d" % n_q,
    )(table, bucket)
    return bias.reshape(N_KV_HEADS, Q_PER_KV * n_q, n_keys)


def _sink_attention(qh, kh, vh, bias, sink, invalid=None):
    s = lax.dot_general(qh, kh, (((1,), (1,)), ((), ())), preferred_element_type=F32) + bias
    if invalid is not None:
        s = jnp.where(invalid, NEG_INF, s)
    m = jnp.maximum(jnp.max(s, axis=-1, keepdims=True), sink)
    p = jnp.exp(s - m)
    denom = jnp.sum(p, axis=-1, keepdims=True) + jnp.exp(sink - m)
    return _dot(p.astype(BF16), vh) / denom


def _stack_heads(q, kv_head):
    base = kv_head * Q_PER_KV * HEAD_DIM
    return jnp.concatenate(
        [q[:, base + g * HEAD_DIM: base + (g + 1) * HEAD_DIM] for g in range(Q_PER_KV)], axis=0)


def _unstack_heads(o, n):
    return jnp.concatenate([o[g * n:(g + 1) * n, :] for g in range(Q_PER_KV)], axis=1)


def _merge_out(x1, attn, gm, ga, gb, wba_ref, wbg_ref, wo_ref, g_post):
    merged = ga * _dot(attn, wba_ref[...]) + gb * _dot(gm, wbg_ref[...])
    return x1 + _rms(_dot(merged.astype(BF16), wo_ref[...]), g_post)


def _prompt_mixer_body(ts, q_ref, kv_ref, kvp_ref, u_ref, vn_ref, ga_ref, gb_ref, x1_ref,
                       bias_ref, sink_ref, ws_ref, bs_ref, wba_ref, wbg_ref, wo_ref, g_ref,
                       out_ref, attn_t_ref, gm_ref):
    first_variant = jnp.where(pl.program_id(1) == 0, 1, 0)
    kv_all = jnp.concatenate([kvp_ref[0], kv_ref[0]], axis=0)
    k_all = kv_all[:, :KV_WIDTH]
    v_t = jnp.transpose(kv_all[:, KV_WIDTH:]).astype(BF16)
    low = lax.broadcasted_iota(jnp.int32, k_all.shape, 1) < HEAD_DIM
    k_swapped = pltpu.roll(k_all, HEAD_DIM, axis=1)
    zero = jnp.zeros_like(k_all)
    k_par = [[jnp.where(low, k_all, zero), jnp.where(low, zero, k_swapped)],
             [jnp.where(low, k_swapped, zero), jnp.where(low, zero, k_all)]]
    k_par = [[k.astype(BF16) for k in ks] for ks in k_par]

    units = [(blk, h, pg) for blk in range(ts // QBLK) for h in range(N_KV_HEADS)
             for pg in range(PAIR_GROUPS)]

    def scores(blk, h, pg):
        keys = slice(blk * QBLK, blk * QBLK + KBLK)
        k_blk = jnp.concatenate([k_par[h][0][keys], k_par[h][1][keys]], axis=0)
        pair0 = h * PAIRS_PER_KV + pg * GROUP_PAIRS
        qa = q_ref[blk, pair0 * QBLK:(pair0 + GROUP_PAIRS) * QBLK, :]
        s = lax.dot_general(k_blk, qa, (((1,), (1,)), ((), ())), preferred_element_type=F32)
        variant = first_variant if blk == 0 else 0
        return s + bias_ref[variant, h, pg]

    def softmax(s, blk, h, pg):
        out = []
        for par in range(HEADS_PER_TILE):
            sp = s[par * KBLK:(par + 1) * KBLK]
            row = (h * PAIR_GROUPS + pg) * HEADS_PER_TILE + par
            sink = sink_ref[row:row + 1, :] * LOG2E
            m = jnp.maximum(jnp.max(sp, axis=0, keepdims=True), sink)
            p = jnp.exp2(sp - m)
            denom = jnp.sum(p, axis=0, keepdims=True) + jnp.exp2(sink - m)
            out.append((p.astype(BF16), 1.0 / denom))
        return out

    def weighted_values(probs, blk, h, pg):
        keys = slice(blk * QBLK, blk * QBLK + KBLK)
        cols = slice(blk * QBLK, (blk + 1) * QBLK)
        vh_t = v_t[h * HEAD_DIM:(h + 1) * HEAD_DIM, keys]
        pair0 = h * PAIRS_PER_KV + pg * GROUP_PAIRS
        for par, (p, inv) in enumerate(probs):
            o = _dot(vh_t, p) * inv
            for pl_ in range(GROUP_PAIRS):
                head = (pair0 + pl_) * HEADS_PER_TILE + par
                attn_t_ref[head * HEAD_DIM:(head + 1) * HEAD_DIM, cols] = o[:, pl_ * QBLK:(pl_ + 1) * QBLK]

    blk_i = lax.broadcasted_iota(jnp.int32, (GMLP_CHUNK, GMLP_CHUNK), 0) // CHUNK
    blk_j = lax.broadcasted_iota(jnp.int32, (GMLP_CHUNK, GMLP_CHUNK), 1) // CHUNK
    for g in range(GMLP_GROUPS):
        w = jnp.where(blk_j <= blk_i, ws_ref[g], 0.0).astype(BF16)
        b = bs_ref[:, g:g + 1]
        cols = slice(g * GMLP_GROUP_DIM, (g + 1) * GMLP_GROUP_DIM)
        for c in range(ts // GMLP_CHUNK):
            rows = slice(c * GMLP_CHUNK, (c + 1) * GMLP_CHUNK)
            sp = _dot(w, vn_ref[0, rows, cols]) + b
            gm_ref[rows, cols] = (u_ref[0, rows, cols].astype(F32) * sp).astype(BF16)

    def gmlp_branch(c):
        cols = slice(c * SIDE_COLS, (c + 1) * SIDE_COLS)
        return gb_ref[0, :, cols].astype(F32) * _dot(gm_ref[...], wbg_ref[:, cols])

    n_side = D_MODEL // SIDE_COLS
    side = []
    s_vals, p_vals = {}, {}
    for i in range(len(units) + 2):
        if i < len(units):
            s_vals[i] = scores(*units[i])
        if 1 <= i <= len(units):
            p_vals[i - 1] = softmax(s_vals.pop(i - 1), *units[i - 1])
        if i >= 2:
            weighted_values(p_vals.pop(i - 2), *units[i - 2])
        if i % (len(units) // n_side) == 1 and len(side) < n_side:
            side.append(gmlp_branch(len(side)))
    assert len(side) == n_side

    attn = jnp.transpose(attn_t_ref[...]).astype(BF16)
    merged = ga_ref[0].astype(F32) * _dot(attn, wba_ref[...]) + jnp.concatenate(side, axis=1)
    out_ref[0] = x1_ref[0] + _rms(_dot(merged.astype(BF16), wo_ref[...]), g_ref[...])


def _prompt_mixer(q, kv, u, vn, ga, gb, x1, bias, sink, w_s, b_s_t, wba, wbg, wo, g_post, *, ts):
    batch, seq, _ = kv.shape
    assert seq % ts == 0 and ts % QBLK == 0 and QBLK == GMLP_CHUNK == WINDOW
    tile = lambda b, t: (b, t, 0)
    prev = lambda b, t: (b, jnp.maximum(t * (ts // WINDOW) - 1, 0), 0)
    wide = pl.BlockSpec((1, ts, D_MODEL), tile)
    n_t = seq // ts
    q_spec = pl.BlockSpec((ts // QBLK, N_HEAD_PAIRS * QBLK, LANES), lambda b, t: (b * n_t + t, 0, 0))
    return pl.pallas_call(
        functools.partial(_prompt_mixer_body, ts),
        grid=(batch, n_t),
        in_specs=[q_spec, pl.BlockSpec((1, ts, 2 * KV_WIDTH), tile),
                  pl.BlockSpec((1, WINDOW, 2 * KV_WIDTH), prev),
                  wide, wide, wide, wide, wide,
                  _resident(bias.shape), _resident(sink.shape), _resident(w_s.shape),
                  _resident(b_s_t.shape), _resident(wba.shape), _resident(wbg.shape),
                  _resident(wo.shape), _resident(g_post.shape)],
        out_specs=wide,
        out_shape=jax.ShapeDtypeStruct((batch, seq, D_MODEL), F32),
        scratch_shapes=[pltpu.VMEM((ATTN_WIDTH, ts), F32), pltpu.VMEM((ts, GMLP_WIDTH), BF16)],
        compiler_params=_params(2),
        name="prompt_mixer",
    )(q, kv, kv, u, vn, ga, gb, x1, bias, sink, w_s, b_s_t, wba, wbg, wo, g_post)


def _sample_mixer_body(n_batch, n_new, q_ref, kv_ref, ck_ref, cv_ref, u_ref, vn_ref, ga_ref, gb_ref,
                       x1_ref, bias_ref, sink_ref, ws_ref, bs_ref, wba_ref, wbg_ref, wo_ref, g_ref,
                       out_ref, attn_ref, gm_ref):
    for b in range(n_batch):
        rows = slice(b * n_new, (b + 1) * n_new)
        q = q_ref[rows, :]
        kv = kv_ref[rows, :]
        k_all = jnp.concatenate([ck_ref[b], kv[:, :KV_WIDTH]], axis=0).astype(BF16)
        v_all = jnp.concatenate([cv_ref[b], kv[:, KV_WIDTH:]], axis=0).astype(BF16)
        for h in range(N_KV_HEADS):
            cols = slice(h * HEAD_DIM, (h + 1) * HEAD_DIM)
            o = _sink_attention(_stack_heads(q, h), k_all[:, cols], v_all[:, cols],
                                bias_ref[h], sink_ref[h])
            width = Q_PER_KV * HEAD_DIM
            attn_ref[rows, h * width:(h + 1) * width] = _unstack_heads(o, n_new).astype(BF16)
        for g in range(GMLP_GROUPS):
            cols = slice(g * GMLP_GROUP_DIM, (g + 1) * GMLP_GROUP_DIM)
            w = ws_ref[g, :n_new, :n_new].astype(BF16)
            sp = _dot(w, vn_ref[rows, cols].astype(BF16)) + bs_ref[:n_new, g:g + 1]
            gm_ref[rows, cols] = (u_ref[rows, cols].astype(F32) * sp).astype(BF16)

    out_ref[...] = _merge_out(x1_ref[...], attn_ref[...], gm_ref[...],
                              ga_ref[...].astype(F32), gb_ref[...].astype(F32),
                              wba_ref, wbg_ref, wo_ref, g_ref[...])


def _sample_mixer(q, kv, cache_k, cache_v, u, vn, ga, gb, x1, bias, sink, w_s, b_s_t,
                  wba, wbg, wo, g_post, *, n_batch, n_new):
    assert n_new <= CHUNK
    t = q.shape[0]
    vmem = pl.BlockSpec(memory_space=pltpu.VMEM)
    return pl.pallas_call(
        functools.partial(_sample_mixer_body, n_batch, n_new),
        in_specs=[vmem] * 17,
        out_specs=vmem,
        out_shape=jax.ShapeDtypeStruct((t, D_MODEL), F32),
        scratch_shapes=[pltpu.VMEM((t, ATTN_WIDTH), BF16), pltpu.VMEM((t, GMLP_WIDTH), BF16)],
        compiler_params=pltpu.CompilerParams(vmem_limit_bytes=VMEM_LIMIT_BYTES),
        name="sample_mixer",
    )(q, kv, cache_k, cache_v, u, vn, ga, gb, x1, bias, sink, w_s, b_s_t, wba, wbg, wo, g_post)


def _prompt_sink_rows(sinks):
    s = sinks.astype(F32).reshape(N_KV_HEADS, PAIR_GROUPS, GROUP_PAIRS, HEADS_PER_TILE)
    s = jnp.transpose(s, (0, 1, 3, 2))[..., None]
    s = jnp.broadcast_to(s, (N_KV_HEADS, PAIR_GROUPS, HEADS_PER_TILE, GROUP_PAIRS, QBLK))
    return s.reshape(N_KV_HEADS * PAIR_GROUPS * HEADS_PER_TILE, GROUP_PAIRS * QBLK)


def _sink_rows(sinks, n_q):
    s = jnp.broadcast_to(sinks.astype(F32).reshape(N_KV_HEADS, Q_PER_KV, 1), (N_KV_HEADS, Q_PER_KV, n_q))
    return s.reshape(N_KV_HEADS, Q_PER_KV * n_q, 1)


PROMPT_TM = 512
PROMPT_TS = 512


def kernel(x_prompt, x_sample, cache_win_k, cache_win_v, rel_bias_table, norm_gains, ffn1_w_gate, ffn1_w_up, ffn1_w_down, w_in, attn_sinks, gmlp_ln_g, gmlp_ln_b, gmlp_w_s, gmlp_b_s, w_branch_attn, w_branch_gmlp, w_out, ffn2_w_gate, ffn2_w_up, ffn2_w_down):
    depth = norm_gains.shape[0]
    batch, seq, _ = x_prompt.shape
    dec_batch, dec_seq, _ = x_sample.shape
    n_cache = cache_win_k.shape[2]
    assert seq % PROMPT_TS == 0 and dec_seq <= CHUNK

    bias_p = _prompt_bias(rel_bias_table)
    bias_s = _relative_bias(rel_bias_table, dec_seq, n_cache + dec_seq, n_cache)

    xp = x_prompt.reshape(batch * seq, D_MODEL)
    xs = x_sample.reshape(dec_batch * dec_seq, D_MODEL)
    t_s = xs.shape[0]
    kp, vp, ks, vs, gs = [], [], [], [], []
    for l in range(depth):
        g = norm_gains[l].astype(F32)
        (win,) = _to_bf16(w_in[l])
        wba, wbg, wo = _to_bf16(w_branch_attn[l], w_branch_gmlp[l], w_out[l])
        ln = jnp.stack([gmlp_ln_g[l], gmlp_ln_b[l]]).astype(F32)
        w_s = gmlp_w_s[l].astype(F32)
        b_s_t = jnp.transpose(gmlp_b_s[l]).astype(F32)
        g_ffn1, g_post2, g_ffn2 = g[0:3], g[3:4], jnp.concatenate([g[4:6], g[5:6]])
        sink_p = _prompt_sink_rows(attn_sinks[l])
        sink_s = _sink_rows(attn_sinks[l], dec_seq)

        x1, h2, *w1 = _ffn_stream(xs, g_ffn1, ffn1_w_gate[l], ffn1_w_up[l], ffn1_w_down[l],
                                  emit_next=True, name="ffn1_sample")
        q, kv, u, vn, ga, gb = _in_proj(h2, win, ln, tm=t_s, vn_dtype=F32, q_blocked=False,
                                        name="in_proj_sample")
        ck = cache_win_k[l].reshape(dec_batch, n_cache, KV_WIDTH)
        cv = cache_win_v[l].reshape(dec_batch, n_cache, KV_WIDTH)
        x2 = _sample_mixer(q, kv, ck, cv, u, vn, ga, gb, x1, bias_s, sink_s, w_s, b_s_t,
                           wba, wbg, wo, g_post2, n_batch=dec_batch, n_new=dec_seq)
        xs, *w2 = _ffn_stream(x2, g_ffn2, ffn2_w_gate[l], ffn2_w_up[l], ffn2_w_down[l],
                              emit_next=False, name="ffn2_sample")
        ks.append(kv[:, :KV_WIDTH].reshape(dec_batch, dec_seq, N_KV_HEADS, HEAD_DIM))
        vs.append(kv[:, KV_WIDTH:].reshape(dec_batch, dec_seq, N_KV_HEADS, HEAD_DIM))
        gs.append(vn.reshape(dec_batch, dec_seq, GMLP_WIDTH))

        x1, h2 = _ffn(xp, g_ffn1, *w1, tm=PROMPT_TM, emit_next=True, name="ffn1_prompt")
        q, kv, u, vn, ga, gb = _in_proj(h2, win, ln, tm=2 * PROMPT_TM, vn_dtype=BF16, q_blocked=True,
                                        name="in_proj_prompt")
        b3 = lambda a: a.reshape(batch, seq, a.shape[-1])
        x2 = _prompt_mixer(q, b3(kv), b3(u), b3(vn), b3(ga), b3(gb), b3(x1), bias_p, sink_p,
                           w_s, b_s_t, wba, wbg, wo, g_post2, ts=PROMPT_TS)
        (xp,) = _ffn(x2.reshape(batch * seq, D_MODEL), g_ffn2, *w2, tm=PROMPT_TM, emit_next=False,
                     name="ffn2_prompt")
        kv_win = b3(kv)[:, seq - WINDOW:, :]
        kp.append(kv_win[..., :KV_WIDTH].reshape(batch, WINDOW, N_KV_HEADS, HEAD_DIM))
        vp.append(kv_win[..., KV_WIDTH:].reshape(batch, WINDOW, N_KV_HEADS, HEAD_DIM))

    return (xp.reshape(batch, seq, D_MODEL), xs.reshape(dec_batch, dec_seq, D_MODEL),
            jnp.stack(kp), jnp.stack(vp), jnp.stack(ks), jnp.stack(vs), jnp.stack(gs))
```

```python
import functools
import math

import jax
import jax.numpy as jnp
import numpy as np
from jax import lax
from jax.experimental import pallas as pl
from jax.experimental.pallas import tpu as pltpu

D_MODEL = 1024
CHUNK = 64
N_Q_HEADS = 16
N_KV_HEADS = 2
HEAD_DIM = 64
Q_PER_KV = N_Q_HEADS // N_KV_HEADS
ATTN_WIDTH = N_Q_HEADS * HEAD_DIM
KV_WIDTH = N_KV_HEADS * HEAD_DIM
WINDOW = 128
GMLP_WIDTH = 1024
GMLP_GROUPS = 4
GMLP_GROUP_DIM = GMLP_WIDTH // GMLP_GROUPS
GMLP_CHUNK = 128
NUM_BUCKETS = 32
MAX_DISTANCE = 128
D_FF = 2816
EPS = 1e-6
NEG_INF = -1e30
LOG2E = math.log2(math.e)

OFF_Q = 0
OFF_KV = ATTN_WIDTH
OFF_U = OFF_KV + 2 * KV_WIDTH
OFF_GV = OFF_U + GMLP_WIDTH
OFF_GA = OFF_GV + GMLP_WIDTH
OFF_GB = OFF_GA + D_MODEL

V7X_VMEM_BYTES = 64 * 1024 * 1024
VMEM_LIMIT_BYTES = V7X_VMEM_BYTES - 8 * 1024 * 1024
MXU_TILE = 256
LANES = 128

QBLK = 2 * CHUNK
KBLK = WINDOW + QBLK
HEADS_PER_TILE = LANES // HEAD_DIM
N_HEAD_PAIRS = N_Q_HEADS // HEADS_PER_TILE
PAIRS_PER_KV = Q_PER_KV // HEADS_PER_TILE
GROUP_PAIRS = 4
PAIR_GROUPS = PAIRS_PER_KV // GROUP_PAIRS
SIDE_COLS = MXU_TILE

BF16 = jnp.bfloat16
F32 = jnp.float32


def _dot(a, b):
    return jnp.dot(a, b, preferred_element_type=F32)


def _rms(x, g):
    return x * lax.rsqrt(jnp.mean(x * x, axis=-1, keepdims=True) + EPS) * g


def _resident(shape):
    zeros = (0,) * len(shape)
    return pl.BlockSpec(shape, lambda *_: zeros, pipeline_mode=pl.Buffered(1))


def _params(n_axes):
    return pltpu.CompilerParams(dimension_semantics=("arbitrary",) * n_axes,
                                vmem_limit_bytes=VMEM_LIMIT_BYTES)


CAST_BLOCK_BYTES = 2 * 1024 * 1024


def _cast_body(*refs):
    n = len(refs) // 2
    for w_ref, o_ref in zip(refs[:n], refs[n:]):
        o_ref[...] = w_ref[...].astype(o_ref.dtype)


def _to_bf16(*ws):
    k = ws[0].shape[0]
    assert all(w.ndim == 2 and w.shape[0] == k for w in ws)
    n_max = max(w.shape[1] for w in ws)
    rows = [r for r in range(16, k + 1, 16) if k % r == 0 and r * n_max * 4 <= CAST_BLOCK_BYTES]
    bk = max(rows) if rows else k
    specs = [pl.BlockSpec((bk, w.shape[1]), lambda i: (i, 0)) for w in ws]
    return pl.pallas_call(
        _cast_body,
        grid=(k // bk,),
        in_specs=specs,
        out_specs=specs,
        out_shape=[jax.ShapeDtypeStruct(w.shape, BF16) for w in ws],
        compiler_params=_params(1),
        name="weights_to_bf16",
    )(*ws)


def _ordering_zero(*arrays):
    m = None
    for a in arrays:
        r = jnp.max(jnp.max(a.astype(F32), axis=0, keepdims=True), axis=1, keepdims=True)
        m = r if m is None else jnp.maximum(m, r)
    bits = lax.bitcast_convert_type(m, jnp.uint32)
    return lax.bitcast_convert_type((bits >> 16) >> 16, F32)


FF_CHUNK = MXU_TILE
FFN_SIDE_PIECES = 8


def _ffn_body(n_tiles, emit_next, xp_ref, xe_ref, g_ref, wg_ref, wu_ref, wd_ref, *refs):
    y_ref = refs[0]
    h_ref, acc_ref, act_ref = refs[-3:]
    s = pl.program_id(0)
    slot = s % 2
    tm = y_ref.shape[0]

    def pre_norm(dst, rows):
        h_ref[dst, rows, :] = _rms(xp_ref[rows, :], g_ref[0:1, :]).astype(BF16)
        return [h_ref[dst, rows, :]]

    def finish(rows):
        y = xe_ref[rows, :] + 0.5 * _rms(acc_ref[rows, :], g_ref[1:2, :])
        y_ref[rows, :] = y
        stored = [y_ref[rows, :]]
        if emit_next:
            refs[1][rows, :] = _rms(y, g_ref[2:3, :]).astype(BF16)
            stored.append(refs[1][rows, :])
        return stored

    @pl.when(s == 0)
    def _():
        pre_norm(0, slice(None))
        acc_ref[...] = jnp.zeros_like(acc_ref)

    @pl.when(jnp.logical_and(s >= 1, s <= n_tiles))
    def _():
        h = h_ref[1 - slot]
        acc = None
        anchor = None
        piece_rows = tm // FFN_SIDE_PIECES
        assert D_FF // FF_CHUNK > FFN_SIDE_PIECES
        for ci in range(D_FF // FF_CHUNK):
            cols = slice(ci * FF_CHUNK, (ci + 1) * FF_CHUNK)
            gate = _dot(h, wg_ref[:, cols])
            up = _dot(h, wu_ref[:, cols])
            if anchor is not None:
                up = up + anchor
            act_ref[:, cols] = (jax.nn.silu(gate) * up).astype(BF16)
            anchor = None
            if ci < FFN_SIDE_PIECES:
                rows = slice(ci * piece_rows, (ci + 1) * piece_rows)
                anchor = _ordering_zero(*(finish(rows) + pre_norm(slot, rows)))
        acc_ref[...] = _dot(act_ref[...], wd_ref[...])

    @pl.when(s == n_tiles + 1)
    def _():
        finish(slice(None))


def _ffn(x, gains, wg, wu, wd, *, tm, emit_next, name):
    t = x.shape[0]
    assert t % tm == 0
    n_tiles = t // tm
    head = lambda s: (jnp.minimum(s, n_tiles - 1), 0)
    tail = lambda s: (jnp.clip(s - 2, 0, n_tiles - 1), 0)
    out_shape = [jax.ShapeDtypeStruct((t, D_MODEL), F32)]
    out_specs = [pl.BlockSpec((tm, D_MODEL), tail)]
    if emit_next:
        out_shape.append(jax.ShapeDtypeStruct((t, D_MODEL), BF16))
        out_specs.append(pl.BlockSpec((tm, D_MODEL), tail))
    return pl.pallas_call(
        functools.partial(_ffn_body, n_tiles, emit_next),
        grid=(n_tiles + 2,),
        in_specs=[pl.BlockSpec((tm, D_MODEL), head), pl.BlockSpec((tm, D_MODEL), tail),
                  _resident(gains.shape), _resident(wg.shape), _resident(wu.shape), _resident(wd.shape)],
        out_specs=out_specs,
        out_shape=out_shape,
        scratch_shapes=[pltpu.VMEM((2, tm, D_MODEL), BF16), pltpu.VMEM((tm, D_MODEL), F32),
                        pltpu.VMEM((tm, D_FF), BF16)],
        compiler_params=_params(1),
        name=name,
    )(x, x, gains, wg, wu, wd)


def _ffn_stream_body(emit_next, x_ref, g_ref, wg_ref, wu_ref, wd_ref, *refs):
    n_out = 2 if emit_next else 1
    y_ref = refs[0]
    wg16_ref, wu16_ref, wd16_ref = refs[n_out:n_out + 3]
    h_ref, acc_ref = refs[-2:]
    c = pl.program_id(0)

    @pl.when(c == 0)
    def _():
        h_ref[...] = _rms(x_ref[...], g_ref[0:1, :]).astype(BF16)
        acc_ref[...] = jnp.zeros_like(acc_ref)

    wg, wu, wd = (r[...].astype(BF16) for r in (wg_ref, wu_ref, wd_ref))
    wg16_ref[...], wu16_ref[...], wd16_ref[...] = wg, wu, wd
    h = h_ref[...]
    act = (jax.nn.silu(_dot(h, wg)) * _dot(h, wu)).astype(BF16)
    acc_ref[...] += _dot(act, wd)

    @pl.when(c == pl.num_programs(0) - 1)
    def _():
        y = x_ref[...] + 0.5 * _rms(acc_ref[...], g_ref[1:2, :])
        y_ref[...] = y
        if emit_next:
            refs[1][...] = _rms(y, g_ref[2:3, :]).astype(BF16)


def _ffn_stream(x, gains, wg, wu, wd, *, emit_next, name):
    t = x.shape[0]
    whole = lambda shape: pl.BlockSpec(shape, lambda c: (0, 0))
    col = pl.BlockSpec((D_MODEL, FF_CHUNK), lambda c: (0, c))
    row = pl.BlockSpec((FF_CHUNK, D_MODEL), lambda c: (c, 0))
    out_shape = [jax.ShapeDtypeStruct((t, D_MODEL), F32)]
    if emit_next:
        out_shape.append(jax.ShapeDtypeStruct((t, D_MODEL), BF16))
    out_specs = [whole((t, D_MODEL))] * len(out_shape) + [col, col, row]
    out_shape += [jax.ShapeDtypeStruct(w.shape, BF16) for w in (wg, wu, wd)]
    return pl.pallas_call(
        functools.partial(_ffn_stream_body, emit_next),
        grid=(D_FF // FF_CHUNK,),
        in_specs=[whole((t, D_MODEL)), whole(gains.shape), col, col, row],
        out_specs=out_specs,
        out_shape=out_shape,
        scratch_shapes=[pltpu.VMEM((t, D_MODEL), BF16), pltpu.VMEM((t, D_MODEL), F32)],
        compiler_params=_params(1),
        name=name,
    )(x, gains, wg, wu, wd)


def _gelu(x):
    return 0.5 * x * (1.0 + lax.erf(x * np.sqrt(0.5).astype(np.float32)))


def _layer_norm(x, g, b):
    mu = jnp.mean(x, axis=-1, keepdims=True)
    xc = x - mu
    var = jnp.mean(xc * xc, axis=-1, keepdims=True)
    return xc * lax.rsqrt(var + EPS) * g + b


IN_PROJ_LN_PIECES = 8


def _in_proj_body(q_blocked, h_ref, w_ref, ln_ref, q_ref, kv_ref, u_ref, vn_ref, ga_ref, gb_ref, gv_ref):
    h = h_ref[...]
    tm = h.shape[0]
    n_chunks = D_MODEL // MXU_TILE
    q_scale = HEAD_DIM ** -0.5 * (LOG2E if q_blocked else 1.0)

    def store_q(c, r):
        q = (r * q_scale).astype(q_ref.dtype)
        if q_blocked:
            for b in range(tm // QBLK):
                for j in range(MXU_TILE // LANES):
                    p = c * (MXU_TILE // LANES) + j
                    q_ref[b, p * QBLK:(p + 1) * QBLK, :] = q[b * QBLK:(b + 1) * QBLK, j * LANES:(j + 1) * LANES]
        else:
            q_ref[:, c * MXU_TILE:(c + 1) * MXU_TILE] = q

    def store_cols(ref, fn):
        def store(c, r):
            ref[:, c * MXU_TILE:(c + 1) * MXU_TILE] = fn(r).astype(ref.dtype)
        return store

    def layer_norm_piece(k):
        rows = slice(k * (tm // IN_PROJ_LN_PIECES), (k + 1) * (tm // IN_PROJ_LN_PIECES))
        vn_ref[rows, :] = _layer_norm(_gelu(gv_ref[rows, :]), ln_ref[0:1, :], ln_ref[1:2, :]).astype(vn_ref.dtype)

    def run(off, c, store):
        store(c, _dot(h, w_ref[:, off + c * MXU_TILE: off + (c + 1) * MXU_TILE]))

    store_gv = lambda c, r: gv_ref.__setitem__((slice(None), slice(c * MXU_TILE, (c + 1) * MXU_TILE)), r)
    store_u, store_ga, store_gb = (store_cols(u_ref, _gelu), store_cols(ga_ref, jax.nn.sigmoid),
                                   store_cols(gb_ref, jax.nn.sigmoid))
    for c in range(n_chunks):
        run(OFF_GV, c, store_gv)
    run(OFF_KV, 0, lambda c, r: kv_ref.__setitem__(Ellipsis, r))
    for c in range(n_chunks):
        run(OFF_Q, c, store_q)
        layer_norm_piece(2 * c)
        run(OFF_GA, c, store_ga)
        layer_norm_piece(2 * c + 1)
    for c in range(n_chunks):
        run(OFF_U, c, store_u)
        run(OFF_GB, c, store_gb)


def _in_proj(h, w_in, ln, *, tm, vn_dtype, q_blocked, name):
    t = h.shape[0]
    assert t % tm == 0
    row = lambda i: (i, 0)
    wide = lambda dt: jax.ShapeDtypeStruct((t, D_MODEL), dt)
    wide_spec = pl.BlockSpec((tm, D_MODEL), row)
    if q_blocked:
        assert tm % QBLK == 0
        q_shape = jax.ShapeDtypeStruct((t // QBLK, N_HEAD_PAIRS * QBLK, LANES), BF16)
        q_spec = pl.BlockSpec((tm // QBLK, N_HEAD_PAIRS * QBLK, LANES), lambda i: (i, 0, 0))
    else:
        q_shape, q_spec = wide(BF16), wide_spec
    return pl.pallas_call(
        functools.partial(_in_proj_body, q_blocked),
        grid=(t // tm,),
        in_specs=[wide_spec, _resident(w_in.shape), _resident(ln.shape)],
        out_specs=[q_spec, pl.BlockSpec((tm, 2 * KV_WIDTH), row),
                   wide_spec, wide_spec, wide_spec, wide_spec],
        out_shape=[q_shape, jax.ShapeDtypeStruct((t, 2 * KV_WIDTH), F32),
                   wide(BF16), wide(vn_dtype), wide(BF16), wide(BF16)],
        scratch_shapes=[pltpu.VMEM((tm, GMLP_WIDTH), F32)],
        compiler_params=_params(1),
        name=name,
    )(h, w_in, ln)


def _t5_bucket(rel):
    half = NUM_BUCKETS // 2
    max_exact = half // 2
    ret = jnp.where(rel > 0, half, 0)
    n = jnp.abs(rel)
    nf = jnp.maximum(n, 1).astype(F32)
    large = max_exact + (jnp.log(nf / max_exact) / math.log(MAX_DISTANCE / max_exact)
                         * (half - max_exact)).astype(jnp.int32)
    large = jnp.minimum(large, half - 1)
    return ret + jnp.where(n < max_exact, n, large)


def _bias_body(table_ref, bucket_ref, out_ref):
    bucket = bucket_ref[...]
    hits = [bucket == b for b in range(NUM_BUCKETS)]
    for head in range(N_Q_HEADS):
        acc = jnp.zeros(bucket.shape, F32)
        for b in range(NUM_BUCKETS):
            acc = jnp.where(hits[b], table_ref[b, head], acc)
        out_ref[head] = acc


def _prompt_bias_body(table_ref, bucket_ref, out_ref):
    bucket = bucket_ref[...]
    hits = [bucket == b for b in range(NUM_BUCKETS)]
    key_chunk = lax.broadcasted_iota(jnp.int32, bucket.shape, 0) // CHUNK
    q_chunk = lax.broadcasted_iota(jnp.int32, bucket.shape, 1) // CHUNK
    in_window = jnp.logical_and(key_chunk >= q_chunk, key_chunk <= q_chunk + WINDOW // CHUNK)
    valid = [in_window, jnp.logical_and(in_window, key_chunk >= WINDOW // CHUNK)]
    for head in range(N_Q_HEADS):
        acc = jnp.zeros(bucket.shape, F32)
        for b in range(NUM_BUCKETS):
            acc = jnp.where(hits[b], table_ref[b, head], acc)
        h, rest = divmod(head, Q_PER_KV)
        pair, par = divmod(rest, HEADS_PER_TILE)
        pg, pl_ = divmod(pair, GROUP_PAIRS)
        for v in range(2):
            out_ref[v, h, pg, par * KBLK:(par + 1) * KBLK, pl_ * QBLK:(pl_ + 1) * QBLK] = (
                jnp.where(valid[v], acc * LOG2E, NEG_INF))


def _prompt_bias(table):
    rel = (jnp.arange(KBLK) - WINDOW)[:, None] - jnp.arange(QBLK)[None, :]
    bucket = _t5_bucket(rel).astype(jnp.int32)
    return pl.pallas_call(
        _prompt_bias_body,
        in_specs=[pl.BlockSpec(memory_space=pltpu.SMEM),
                  pl.BlockSpec(memory_space=pltpu.VMEM)],
        out_specs=pl.BlockSpec(memory_space=pltpu.VMEM),
        out_shape=jax.ShapeDtypeStruct((2, N_KV_HEADS, PAIR_GROUPS, 2 * KBLK, GROUP_PAIRS * QBLK), F32),
        name="rel_bias_prompt",
    )(table, bucket)


def _relative_bias(table, n_q, n_keys, n_past):
    rel = (jnp.arange(n_keys) - n_past)[None, :] - jnp.arange(n_q)[:, None]
    bucket = _t5_bucket(rel).astype(jnp.int32)
    bias = pl.pallas_call(
        _bias_body,
        in_specs=[pl.BlockSpec(memory_space=pltpu.SMEM),
                  pl.BlockSpec(memory_space=pltpu.VMEM)],
        out_specs=pl.BlockSpec(memory_space=pltpu.VMEM),
        out_shape=jax.ShapeDtypeStruct((N_Q_HEADS, n_q, n_keys), F32),
        name="rel_bias_%d" % n_q,
    )(table, bucket)
    return bias.reshape(N_KV_HEADS, Q_PER_KV * n_q, n_keys)


def _sink_attention(qh, kh, vh, bias, sink, invalid=None):
    s = lax.dot_general(qh, kh, (((1,), (1,)), ((), ())), preferred_element_type=F32) + bias
    if invalid is not None:
        s = jnp.where(invalid, NEG_INF, s)
    m = jnp.maximum(jnp.max(s, axis=-1, keepdims=True), sink)
    p = jnp.exp(s - m)
    denom = jnp.sum(p, axis=-1, keepdims=True) + jnp.exp(sink - m)
    return _dot(p.astype(BF16), vh) / denom


def _stack_heads(q, kv_head):
    base = kv_head * Q_PER_KV * HEAD_DIM
    return jnp.concatenate(
        [q[:, base + g * HEAD_DIM: base + (g + 1) * HEAD_DIM] for g in range(Q_PER_KV)], axis=0)


def _unstack_heads(o, n):
    return jnp.concatenate([o[g * n:(g + 1) * n, :] for g in range(Q_PER_KV)], axis=1)


def _merge_out(x1, attn, gm, ga, gb, wba_ref, wbg_ref, wo_ref, g_post):
    merged = ga * _dot(attn, wba_ref[...]) + gb * _dot(gm, wbg_ref[...])
    return x1 + _rms(_dot(merged.astype(BF16), wo_ref[...]), g_post)


def _prompt_mixer_body(ts, q_ref, kv_ref, kvp_ref, u_ref, vn_ref, ga_ref, gb_ref, x1_ref,
                       bias_ref, sink_ref, ws_ref, bs_ref, wba_ref, wbg_ref, wo_ref, g_ref,
                       out_ref, attn_t_ref, gm_ref):
    first_variant = jnp.where(pl.program_id(1) == 0, 1, 0)
    kv_all = jnp.concatenate([kvp_ref[0], kv_ref[0]], axis=0)
    k_all = kv_all[:, :KV_WIDTH]
    v_t = jnp.transpose(kv_all[:, KV_WIDTH:]).astype(BF16)
    low = lax.broadcasted_iota(jnp.int32, k_all.shape, 1) < HEAD_DIM
    k_swapped = pltpu.roll(k_all, HEAD_DIM, axis=1)
    zero = jnp.zeros_like(k_all)
    k_par = [[jnp.where(low, k_all, zero), jnp.where(low, zero, k_swapped)],
             [jnp.where(low, k_swapped, zero), jnp.where(low, zero, k_all)]]
    k_par = [[k.astype(BF16) for k in ks] for ks in k_par]

    units = [(blk, h, pg) for blk in range(ts // QBLK) for h in range(N_KV_HEADS)
             for pg in range(PAIR_GROUPS)]

    def scores(blk, h, pg):
        keys = slice(blk * QBLK, blk * QBLK + KBLK)
        k_blk = jnp.concatenate([k_par[h][0][keys], k_par[h][1][keys]], axis=0)
        pair0 = h * PAIRS_PER_KV + pg * GROUP_PAIRS
        qa = q_ref[blk, pair0 * QBLK:(pair0 + GROUP_PAIRS) * QBLK, :]
        s = lax.dot_general(k_blk, qa, (((1,), (1,)), ((), ())), preferred_element_type=F32)
        variant = first_variant if blk == 0 else 0
        return s + bias_ref[variant, h, pg]

    def softmax(s, blk, h, pg):
        out = []
        for par in range(HEADS_PER_TILE):
            sp = s[par * KBLK:(par + 1) * KBLK]
            row = (h * PAIR_GROUPS + pg) * HEADS_PER_TILE + par
            sink = sink_ref[row:row + 1, :] * LOG2E
            m = jnp.maximum(jnp.max(sp, axis=0, keepdims=True), sink)
            p = jnp.exp2(sp - m)
            denom = jnp.sum(p, axis=0, keepdims=True) + jnp.exp2(sink - m)
            out.append((p.astype(BF16), 1.0 / denom))
        return out

    def weighted_values(probs, blk, h, pg):
        keys = slice(blk * QBLK, blk * QBLK + KBLK)
        cols = slice(blk * QBLK, (blk + 1) * QBLK)
        vh_t = v_t[h * HEAD_DIM:(h + 1) * HEAD_DIM, keys]
        pair0 = h * PAIRS_PER_KV + pg * GROUP_PAIRS
        for par, (p, inv) in enumerate(probs):
            o = _dot(vh_t, p) * inv
            for pl_ in range(GROUP_PAIRS):
                head = (pair0 + pl_) * HEADS_PER_TILE + par
                attn_t_ref[head * HEAD_DIM:(head + 1) * HEAD_DIM, cols] = o[:, pl_ * QBLK:(pl_ + 1) * QBLK]

    blk_i = lax.broadcasted_iota(jnp.int32, (GMLP_CHUNK, GMLP_CHUNK), 0) // CHUNK
    blk_j = lax.broadcasted_iota(jnp.int32, (GMLP_CHUNK, GMLP_CHUNK), 1) // CHUNK
    for g in range(GMLP_GROUPS):
        w = jnp.where(blk_j <= blk_i, ws_ref[g], 0.0).astype(BF16)
        b = bs_ref[:, g:g + 1]
        cols = slice(g * GMLP_GROUP_DIM, (g + 1) * GMLP_GROUP_DIM)
        for c in range(ts // GMLP_CHUNK):
            rows = slice(c * GMLP_CHUNK, (c + 1) * GMLP_CHUNK)
            sp = _dot(w, vn_ref[0, rows, cols]) + b
            gm_ref[rows, cols] = (u_ref[0, rows, cols].astype(F32) * sp).astype(BF16)

    def gmlp_branch(c):
        cols = slice(c * SIDE_COLS, (c + 1) * SIDE_COLS)
        return gb_ref[0, :, cols].astype(F32) * _dot(gm_ref[...], wbg_ref[:, cols])

    n_side = D_MODEL // SIDE_COLS
    side = []
    s_vals, p_vals = {}, {}
    for i in range(len(units) + 2):
        if i < len(units):
            s_vals[i] = scores(*units[i])
        if 1 <= i <= len(units):
            p_vals[i - 1] = softmax(s_vals.pop(i - 1), *units[i - 1])
        if i >= 2:
            weighted_values(p_vals.pop(i - 2), *units[i - 2])
        if i % (len(units) // n_side) == 1 and len(side) < n_side:
            side.append(gmlp_branch(len(side)))
    assert len(side) == n_side

    attn = jnp.transpose(attn_t_ref[...]).astype(BF16)
    merged = ga_ref[0].astype(F32) * _dot(attn, wba_ref[...]) + jnp.concatenate(side, axis=1)
    out_ref[0] = x1_ref[0] + _rms(_dot(merged.astype(BF16), wo_ref[...]), g_ref[...])


def _prompt_mixer(q, kv, u, vn, ga, gb, x1, bias, sink, w_s, b_s_t, wba, wbg, wo, g_post, *, ts):
    batch, seq, _ = kv.shape
    assert seq % ts == 0 and ts % QBLK == 0 and QBLK == GMLP_CHUNK == WINDOW
    tile = lambda b, t: (b, t, 0)
    prev = lambda b, t: (b, jnp.maximum(t * (ts // WINDOW) - 1, 0), 0)
    wide = pl.BlockSpec((1, ts, D_MODEL), tile)
    n_t = seq // ts
    q_spec = pl.BlockSpec((ts // QBLK, N_HEAD_PAIRS * QBLK, LANES), lambda b, t: (b * n_t + t, 0, 0))
    return pl.pallas_call(
        functools.partial(_prompt_mixer_body, ts),
        grid=(batch, n_t),
        in_specs=[q_spec, pl.BlockSpec((1, ts, 2 * KV_WIDTH), tile),
                  pl.BlockSpec((1, WINDOW, 2 * KV_WIDTH), prev),
                  wide, wide, wide, wide, wide,
                  _resident(bias.shape), _resident(sink.shape), _resident(w_s.shape),
                  _resident(b_s_t.shape), _resident(wba.shape), _resident(wbg.shape),
                  _resident(wo.shape), _resident(g_post.shape)],
        out_specs=wide,
        out_shape=jax.ShapeDtypeStruct((batch, seq, D_MODEL), F32),
        scratch_shapes=[pltpu.VMEM((ATTN_WIDTH, ts), F32), pltpu.VMEM((ts, GMLP_WIDTH), BF16)],
        compiler_params=_params(2),
        name="prompt_mixer",
    )(q, kv, kv, u, vn, ga, gb, x1, bias, sink, w_s, b_s_t, wba, wbg, wo, g_post)


def _sample_mixer_body(n_batch, n_new, q_ref, kv_ref, ck_ref, cv_ref, u_ref, vn_ref, ga_ref, gb_ref,
                       x1_ref, bias_ref, sink_ref, ws_ref, bs_ref, wba_ref, wbg_ref, wo_ref, g_ref,
                       out_ref, attn_ref, gm_ref):
    for b in range(n_batch):
        rows = slice(b * n_new, (b + 1) * n_new)
        q = q_ref[rows, :]
        kv = kv_ref[rows, :]
        k_all = jnp.concatenate([ck_ref[b], kv[:, :KV_WIDTH]], axis=0).astype(BF16)
        v_all = jnp.concatenate([cv_ref[b], kv[:, KV_WIDTH:]], axis=0).astype(BF16)
        for h in range(N_KV_HEADS):
            cols = slice(h * HEAD_DIM, (h + 1) * HEAD_DIM)
            o = _sink_attention(_stack_heads(q, h), k_all[:, cols], v_all[:, cols],
                                bias_ref[h], sink_ref[h])
            width = Q_PER_KV * HEAD_DIM
            attn_ref[rows, h * width:(h + 1) * width] = _unstack_heads(o, n_new).astype(BF16)
        for g in range(GMLP_GROUPS):
            cols = slice(g * GMLP_GROUP_DIM, (g + 1) * GMLP_GROUP_DIM)
            w = ws_ref[g, :n_new, :n_new].astype(BF16)
            sp = _dot(w, vn_ref[rows, cols].astype(BF16)) + bs_ref[:n_new, g:g + 1]
            gm_ref[rows, cols] = (u_ref[rows, cols].astype(F32) * sp).astype(BF16)

    out_ref[...] = _merge_out(x1_ref[...], attn_ref[...], gm_ref[...],
                              ga_ref[...].astype(F32), gb_ref[...].astype(F32),
                              wba_ref, wbg_ref, wo_ref, g_ref[...])


def _sample_mixer(q, kv, cache_k, cache_v, u, vn, ga, gb, x1, bias, sink, w_s, b_s_t,
                  wba, wbg, wo, g_post, *, n_batch, n_new):
    assert n_new <= CHUNK
    t = q.shape[0]
    vmem = pl.BlockSpec(memory_space=pltpu.VMEM)
    return pl.pallas_call(
        functools.partial(_sample_mixer_body, n_batch, n_new),
        in_specs=[vmem] * 17,
        out_specs=vmem,
        out_shape=jax.ShapeDtypeStruct((t, D_MODEL), F32),
        scratch_shapes=[pltpu.VMEM((t, ATTN_WIDTH), BF16), pltpu.VMEM((t, GMLP_WIDTH), BF16)],
        compiler_params=pltpu.CompilerParams(vmem_limit_bytes=VMEM_LIMIT_BYTES),
        name="sample_mixer",
    )(q, kv, cache_k, cache_v, u, vn, ga, gb, x1, bias, sink, w_s, b_s_t, wba, wbg, wo, g_post)


def _prompt_sink_rows(sinks):
    s = sinks.astype(F32).reshape(N_KV_HEADS, PAIR_GROUPS, GROUP_PAIRS, HEADS_PER_TILE)
    s = jnp.transpose(s, (0, 1, 3, 2))[..., None]
    s = jnp.broadcast_to(s, (N_KV_HEADS, PAIR_GROUPS, HEADS_PER_TILE, GROUP_PAIRS, QBLK))
    return s.reshape(N_KV_HEADS * PAIR_GROUPS * HEADS_PER_TILE, GROUP_PAIRS * QBLK)


def _sink_rows(sinks, n_q):
    s = jnp.broadcast_to(sinks.astype(F32).reshape(N_KV_HEADS, Q_PER_KV, 1), (N_KV_HEADS, Q_PER_KV, n_q))
    return s.reshape(N_KV_HEADS, Q_PER_KV * n_q, 1)


PROMPT_TM = 512
PROMPT_TS = 512


def kernel(x_prompt, x_sample, cache_win_k, cache_win_v, rel_bias_table, norm_gains, ffn1_w_gate, ffn1_w_up, ffn1_w_down, w_in, attn_sinks, gmlp_ln_g, gmlp_ln_b, gmlp_w_s, gmlp_b_s, w_branch_attn, w_branch_gmlp, w_out, ffn2_w_gate, ffn2_w_up, ffn2_w_down):
    depth = norm_gains.shape[0]
    batch, seq, _ = x_prompt.shape
    dec_batch, dec_seq, _ = x_sample.shape
    n_cache = cache_win_k.shape[2]
    assert seq % PROMPT_TS == 0 and dec_seq <= CHUNK

    bias_p = _prompt_bias(rel_bias_table)
    bias_s = _relative_bias(rel_bias_table, dec_seq, n_cache + dec_seq, n_cache)

    xp = x_prompt.reshape(batch * seq, D_MODEL)
    xs = x_sample.reshape(dec_batch * dec_seq, D_MODEL)
    t_s = xs.shape[0]
    kp, vp, ks, vs, gs = [], [], [], [], []
    for l in range(depth):
        g = norm_gains[l].astype(F32)
        win, wba, wbg, wo = _to_bf16(w_in[l], w_branch_attn[l], w_branch_gmlp[l], w_out[l])
        ln = jnp.stack([gmlp_ln_g[l], gmlp_ln_b[l]]).astype(F32)
        w_s = gmlp_w_s[l].astype(F32)
        b_s_t = jnp.transpose(gmlp_b_s[l]).astype(F32)
        g_ffn1, g_post2, g_ffn2 = g[0:3], g[3:4], g[4:6]
        sink_p = _prompt_sink_rows(attn_sinks[l])
        sink_s = _sink_rows(attn_sinks[l], dec_seq)

        x1, h2, *w1 = _ffn_stream(xs, g_ffn1, ffn1_w_gate[l], ffn1_w_up[l], ffn1_w_down[l],
                                  emit_next=True, name="ffn1_sample")
        q, kv, u, vn, ga, gb = _in_proj(h2, win, ln, tm=t_s, vn_dtype=F32, q_blocked=False,
                                        name="in_proj_sample")
        ck = cache_win_k[l].reshape(dec_batch, n_cache, KV_WIDTH)
        cv = cache_win_v[l].reshape(dec_batch, n_cache, KV_WIDTH)
        x2 = _sample_mixer(q, kv, ck, cv, u, vn, ga, gb, x1, bias_s, sink_s, w_s, b_s_t,
                           wba, wbg, wo, g_post2, n_batch=dec_batch, n_new=dec_seq)
        xs, *w2 = _ffn_stream(x2, g_ffn2, ffn2_w_gate[l], ffn2_w_up[l], ffn2_w_down[l],
                              emit_next=False, name="ffn2_sample")
        ks.append(kv[:, :KV_WIDTH].reshape(dec_batch, dec_seq, N_KV_HEADS, HEAD_DIM))
        vs.append(kv[:, KV_WIDTH:].reshape(dec_batch, dec_seq, N_KV_HEADS, HEAD_DIM))
        gs.append(vn.reshape(dec_batch, dec_seq, GMLP_WIDTH))

        x1, h2 = _ffn(xp, g_ffn1, *w1, tm=PROMPT_TM, emit_next=True, name="ffn1_prompt")
        q, kv, u, vn, ga, gb = _in_proj(h2, win, ln, tm=2 * PROMPT_TM, vn_dtype=BF16, q_blocked=True,
                                        name="in_proj_prompt")
        b3 = lambda a: a.reshape(batch, seq, a.shape[-1])
        x2 = _prompt_mixer(q, b3(kv), b3(u), b3(vn), b3(ga), b3(gb), b3(x1), bias_p, sink_p,
                           w_s, b_s_t, wba, wbg, wo, g_post2, ts=PROMPT_TS)
        (xp,) = _ffn(x2.reshape(batch * seq, D_MODEL), g_ffn2, *w2, tm=PROMPT_TM, emit_next=False,
                     name="ffn2_prompt")
        kv_win = b3(kv)[:, seq - WINDOW:, :]
        kp.append(kv_win[..., :KV_WIDTH].reshape(batch, WINDOW, N_KV_HEADS, HEAD_DIM))
        vp.append(kv_win[..., KV_WIDTH:].reshape(batch, WINDOW, N_KV_HEADS, HEAD_DIM))

    return (xp.reshape(batch, seq, D_MODEL), xs.reshape(dec_batch, dec_seq, D_MODEL),
            jnp.stack(kp), jnp.stack(vp), jnp.stack(ks), jnp.stack(vs), jnp.stack(gs))
```

```python
import functools
import math

import jax
import jax.numpy as jnp
import numpy as np
from jax import lax
from jax.experimental import pallas as pl
from jax.experimental.pallas import tpu as pltpu

D_MODEL = 1024
CHUNK = 64
N_Q_HEADS = 16
N_KV_HEADS = 2
HEAD_DIM = 64
Q_PER_KV = N_Q_HEADS // N_KV_HEADS
ATTN_WIDTH = N_Q_HEADS * HEAD_DIM
KV_WIDTH = N_KV_HEADS * HEAD_DIM
WINDOW = 128
GMLP_WIDTH = 1024
GMLP_GROUPS = 4
GMLP_GROUP_DIM = GMLP_WIDTH // GMLP_GROUPS
GMLP_CHUNK = 128
NUM_BUCKETS = 32
MAX_DISTANCE = 128
D_FF = 2816
EPS = 1e-6
NEG_INF = -1e30
LOG2E = math.log2(math.e)

OFF_Q = 0
OFF_KV = ATTN_WIDTH
OFF_U = OFF_KV + 2 * KV_WIDTH
OFF_GV = OFF_U + GMLP_WIDTH
OFF_GA = OFF_GV + GMLP_WIDTH
OFF_GB = OFF_GA + D_MODEL

V7X_VMEM_BYTES = 64 * 1024 * 1024
VMEM_LIMIT_BYTES = V7X_VMEM_BYTES - 8 * 1024 * 1024
MXU_TILE = 256
LANES = 128

QBLK = 2 * CHUNK
KBLK = WINDOW + QBLK
HEADS_PER_TILE = LANES // HEAD_DIM
N_HEAD_PAIRS = N_Q_HEADS // HEADS_PER_TILE
PAIRS_PER_KV = Q_PER_KV // HEADS_PER_TILE
GROUP_PAIRS = 4
PAIR_GROUPS = PAIRS_PER_KV // GROUP_PAIRS
SIDE_COLS = MXU_TILE

BF16 = jnp.bfloat16
F32 = jnp.float32


def _dot(a, b):
    return jnp.dot(a, b, preferred_element_type=F32)


def _rms(x, g):
    return x * lax.rsqrt(jnp.mean(x * x, axis=-1, keepdims=True) + EPS) * g


def _resident(shape):
    zeros = (0,) * len(shape)
    return pl.BlockSpec(shape, lambda *_: zeros, pipeline_mode=pl.Buffered(1))


def _params(n_axes):
    return pltpu.CompilerParams(dimension_semantics=("arbitrary",) * n_axes,
                                vmem_limit_bytes=VMEM_LIMIT_BYTES)


CAST_BLOCK_BYTES = 2 * 1024 * 1024


def _cast_body(*refs):
    n = len(refs) // 2
    for w_ref, o_ref in zip(refs[:n], refs[n:]):
        o_ref[...] = w_ref[...].astype(o_ref.dtype)


def _to_bf16(*ws):
    k = ws[0].shape[0]
    assert all(w.ndim == 2 and w.shape[0] == k for w in ws)
    n_max = max(w.shape[1] for w in ws)
    rows = [r for r in range(16, k + 1, 16) if k % r == 0 and r * n_max * 4 <= CAST_BLOCK_BYTES]
    bk = max(rows) if rows else k
    specs = [pl.BlockSpec((bk, w.shape[1]), lambda i: (i, 0)) for w in ws]
    return pl.pallas_call(
        _cast_body,
        grid=(k // bk,),
        in_specs=specs,
        out_specs=specs,
        out_shape=[jax.ShapeDtypeStruct(w.shape, BF16) for w in ws],
        compiler_params=_params(1),
        name="weights_to_bf16",
    )(*ws)


def _ordering_zero(*arrays):
    m = None
    for a in arrays:
        r = jnp.max(jnp.max(a.astype(F32), axis=0, keepdims=True), axis=1, keepdims=True)
        m = r if m is None else jnp.maximum(m, r)
    bits = lax.bitcast_convert_type(m, jnp.uint32)
    return lax.bitcast_convert_type((bits >> 16) >> 16, F32)


FF_CHUNK = MXU_TILE
FFN_SIDE_PIECES = 8


def _ffn_body(n_tiles, emit_next, xp_ref, xe_ref, g_ref, wg_ref, wu_ref, wd_ref, *refs):
    y_ref = refs[0]
    h_ref, acc_ref, act_ref = refs[-3:]
    s = pl.program_id(0)
    slot = s % 2
    tm = y_ref.shape[0]

    def pre_norm(dst, rows):
        h_ref[dst, rows, :] = _rms(xp_ref[rows, :], g_ref[0:1, :]).astype(BF16)
        return [h_ref[dst, rows, :]]

    def finish(rows):
        y = xe_ref[rows, :] + 0.5 * _rms(acc_ref[rows, :], g_ref[1:2, :])
        y_ref[rows, :] = y
        stored = [y_ref[rows, :]]
        if emit_next:
            refs[1][rows, :] = _rms(y, g_ref[2:3, :]).astype(BF16)
            stored.append(refs[1][rows, :])
        return stored

    @pl.when(s == 0)
    def _():
        pre_norm(0, slice(None))
        acc_ref[...] = jnp.zeros_like(acc_ref)

    @pl.when(jnp.logical_and(s >= 1, s <= n_tiles))
    def _():
        h = h_ref[1 - slot]
        acc = None
        anchor = None
        piece_rows = tm // FFN_SIDE_PIECES
        assert D_FF // FF_CHUNK > FFN_SIDE_PIECES
        for ci in range(D_FF // FF_CHUNK):
            cols = slice(ci * FF_CHUNK, (ci + 1) * FF_CHUNK)
            gate = _dot(h, wg_ref[:, cols])
            up = _dot(h, wu_ref[:, cols])
            if anchor is not None:
                up = up + anchor
            act_ref[:, cols] = (jax.nn.silu(gate) * up).astype(BF16)
            anchor = None
            if ci < FFN_SIDE_PIECES:
                rows = slice(ci * piece_rows, (ci + 1) * piece_rows)
                anchor = _ordering_zero(*(finish(rows) + pre_norm(slot, rows)))
        acc_ref[...] = _dot(act_ref[...], wd_ref[...])

    @pl.when(s == n_tiles + 1)
    def _():
        finish(slice(None))


def _ffn(x, gains, wg, wu, wd, *, tm, emit_next, name):
    t = x.shape[0]
    assert t % tm == 0
    n_tiles = t // tm
    head = lambda s: (jnp.minimum(s, n_tiles - 1), 0)
    tail = lambda s: (jnp.clip(s - 2, 0, n_tiles - 1), 0)
    out_shape = [jax.ShapeDtypeStruct((t, D_MODEL), F32)]
    out_specs = [pl.BlockSpec((tm, D_MODEL), tail)]
    if emit_next:
        out_shape.append(jax.ShapeDtypeStruct((t, D_MODEL), BF16))
        out_specs.append(pl.BlockSpec((tm, D_MODEL), tail))
    return pl.pallas_call(
        functools.partial(_ffn_body, n_tiles, emit_next),
        grid=(n_tiles + 2,),
        in_specs=[pl.BlockSpec((tm, D_MODEL), head), pl.BlockSpec((tm, D_MODEL), tail),
                  _resident(gains.shape), _resident(wg.shape), _resident(wu.shape), _resident(wd.shape)],
        out_specs=out_specs,
        out_shape=out_shape,
        scratch_shapes=[pltpu.VMEM((2, tm, D_MODEL), BF16), pltpu.VMEM((tm, D_MODEL), F32),
                        pltpu.VMEM((tm, D_FF), BF16)],
        compiler_params=_params(1),
        name=name,
    )(x, x, gains, wg, wu, wd)


def _ffn_stream_body(emit_next, x_ref, g_ref, wg_ref, wu_ref, wd_ref, *refs):
    n_out = 2 if emit_next else 1
    y_ref = refs[0]
    wg16_ref, wu16_ref, wd16_ref = refs[n_out:n_out + 3]
    h_ref, acc_ref = refs[-2:]
    c = pl.program_id(0)

    @pl.when(c == 0)
    def _():
        h_ref[...] = _rms(x_ref[...], g_ref[0:1, :]).astype(BF16)
        acc_ref[...] = jnp.zeros_like(acc_ref)

    wg, wu, wd = (r[...].astype(BF16) for r in (wg_ref, wu_ref, wd_ref))
    wg16_ref[...], wu16_ref[...], wd16_ref[...] = wg, wu, wd
    h = h_ref[...]
    act = (jax.nn.silu(_dot(h, wg)) * _dot(h, wu)).astype(BF16)
    acc_ref[...] += _dot(act, wd)

    @pl.when(c == pl.num_programs(0) - 1)
    def _():
        y = x_ref[...] + 0.5 * _rms(acc_ref[...], g_ref[1:2, :])
        y_ref[...] = y
        if emit_next:
            refs[1][...] = _rms(y, g_ref[2:3, :]).astype(BF16)


def _ffn_stream(x, gains, wg, wu, wd, *, emit_next, name):
    t = x.shape[0]
    whole = lambda shape: pl.BlockSpec(shape, lambda c: (0, 0))
    col = pl.BlockSpec((D_MODEL, FF_CHUNK), lambda c: (0, c))
    row = pl.BlockSpec((FF_CHUNK, D_MODEL), lambda c: (c, 0))
    out_shape = [jax.ShapeDtypeStruct((t, D_MODEL), F32)]
    if emit_next:
        out_shape.append(jax.ShapeDtypeStruct((t, D_MODEL), BF16))
    out_specs = [whole((t, D_MODEL))] * len(out_shape) + [col, col, row]
    out_shape += [jax.ShapeDtypeStruct(w.shape, BF16) for w in (wg, wu, wd)]
    return pl.pallas_call(
        functools.partial(_ffn_stream_body, emit_next),
        grid=(D_FF // FF_CHUNK,),
        in_specs=[whole((t, D_MODEL)), whole(gains.shape), col, col, row],
        out_specs=out_specs,
        out_shape=out_shape,
        scratch_shapes=[pltpu.VMEM((t, D_MODEL), BF16), pltpu.VMEM((t, D_MODEL), F32)],
        compiler_params=_params(1),
        name=name,
    )(x, gains, wg, wu, wd)


def _gelu(x):
    return 0.5 * x * (1.0 + lax.erf(x * np.sqrt(0.5).astype(np.float32)))


def _layer_norm(x, g, b):
    mu = jnp.mean(x, axis=-1, keepdims=True)
    xc = x - mu
    var = jnp.mean(xc * xc, axis=-1, keepdims=True)
    return xc * lax.rsqrt(var + EPS) * g + b


IN_PROJ_LN_PIECES = 8


def _in_proj_body(q_blocked, h_ref, w_ref, ln_ref, q_ref, kv_ref, *refs):
    if q_blocked:
        packed_ref, gv_ref = refs
        u_ref, vn_ref, ga_ref, gb_ref = (packed_ref.at[:, k * D_MODEL:(k + 1) * D_MODEL] for k in range(4))
    else:
        u_ref, vn_ref, ga_ref, gb_ref, gv_ref = refs
    _in_proj_compute(q_blocked, h_ref, w_ref, ln_ref, q_ref, kv_ref, u_ref, vn_ref, ga_ref, gb_ref, gv_ref)


def _in_proj_compute(q_blocked, h_ref, w_ref, ln_ref, q_ref, kv_ref, u_ref, vn_ref, ga_ref, gb_ref, gv_ref):
    h = h_ref[...]
    tm = h.shape[0]
    n_chunks = D_MODEL // MXU_TILE
    q_scale = HEAD_DIM ** -0.5 * (LOG2E if q_blocked else 1.0)

    def store_q(c, r):
        q = (r * q_scale).astype(q_ref.dtype)
        if q_blocked:
            for b in range(tm // QBLK):
                for j in range(MXU_TILE // LANES):
                    p = c * (MXU_TILE // LANES) + j
                    q_ref[b, p * QBLK:(p + 1) * QBLK, :] = q[b * QBLK:(b + 1) * QBLK, j * LANES:(j + 1) * LANES]
        else:
            q_ref[:, c * MXU_TILE:(c + 1) * MXU_TILE] = q

    def store_cols(ref, fn):
        def store(c, r):
            ref[:, c * MXU_TILE:(c + 1) * MXU_TILE] = fn(r).astype(ref.dtype)
        return store

    def layer_norm_piece(k):
        rows = slice(k * (tm // IN_PROJ_LN_PIECES), (k + 1) * (tm // IN_PROJ_LN_PIECES))
        vn_ref[rows, :] = _layer_norm(_gelu(gv_ref[rows, :]), ln_ref[0:1, :], ln_ref[1:2, :]).astype(vn_ref.dtype)

    def run(off, c, store):
        store(c, _dot(h, w_ref[:, off + c * MXU_TILE: off + (c + 1) * MXU_TILE]))

    store_gv = lambda c, r: gv_ref.__setitem__((slice(None), slice(c * MXU_TILE, (c + 1) * MXU_TILE)), r)
    store_u, store_ga, store_gb = (store_cols(u_ref, _gelu), store_cols(ga_ref, jax.nn.sigmoid),
                                   store_cols(gb_ref, jax.nn.sigmoid))
    for c in range(n_chunks):
        run(OFF_GV, c, store_gv)
    run(OFF_KV, 0, lambda c, r: kv_ref.__setitem__(Ellipsis, r))
    for c in range(n_chunks):
        run(OFF_Q, c, store_q)
        layer_norm_piece(2 * c)
        run(OFF_GA, c, store_ga)
        layer_norm_piece(2 * c + 1)
    for c in range(n_chunks):
        run(OFF_U, c, store_u)
        run(OFF_GB, c, store_gb)


def _in_proj(h, w_in, ln, *, tm, vn_dtype, q_blocked, name):
    t = h.shape[0]
    assert t % tm == 0
    row = lambda i: (i, 0)
    wide = lambda dt: jax.ShapeDtypeStruct((t, D_MODEL), dt)
    wide_spec = pl.BlockSpec((tm, D_MODEL), row)
    if q_blocked:
        assert tm % QBLK == 0 and vn_dtype == BF16
        q_shape = jax.ShapeDtypeStruct((t // QBLK, N_HEAD_PAIRS * QBLK, LANES), BF16)
        q_spec = pl.BlockSpec((tm // QBLK, N_HEAD_PAIRS * QBLK, LANES), lambda i: (i, 0, 0))
        act_shapes = [jax.ShapeDtypeStruct((t, 4 * D_MODEL), BF16)]
        act_specs = [pl.BlockSpec((tm, 4 * D_MODEL), row)]
    else:
        q_shape, q_spec = wide(BF16), wide_spec
        act_shapes = [wide(BF16), wide(vn_dtype), wide(BF16), wide(BF16)]
        act_specs = [wide_spec] * 4
    return pl.pallas_call(
        functools.partial(_in_proj_body, q_blocked),
        grid=(t // tm,),
        in_specs=[wide_spec, _resident(w_in.shape), _resident(ln.shape)],
        out_specs=[q_spec, pl.BlockSpec((tm, 2 * KV_WIDTH), row)] + act_specs,
        out_shape=[q_shape, jax.ShapeDtypeStruct((t, 2 * KV_WIDTH), F32)] + act_shapes,
        scratch_shapes=[pltpu.VMEM((tm, GMLP_WIDTH), F32)],
        compiler_params=_params(1),
        name=name,
    )(h, w_in, ln)


def _t5_bucket(rel):
    half = NUM_BUCKETS // 2
    max_exact = half // 2
    ret = np.where(rel > 0, half, 0)
    n = np.abs(rel)
    nf = np.maximum(n, 1).astype(np.float32)
    scaled = (np.log(nf / np.float32(max_exact)) / np.float32(math.log(MAX_DISTANCE / max_exact))
              * np.float32(half - max_exact))
    exact = np.log(np.maximum(n, 1) / max_exact) / math.log(MAX_DISTANCE / max_exact) * (half - max_exact)
    assert np.array_equal(scaled.astype(np.int32), exact.astype(np.int32))
    large = np.minimum(max_exact + scaled.astype(np.int32), half - 1)
    return (ret + np.where(n < max_exact, n, large)).astype(np.int32)


def _bias_body(table_ref, bucket_ref, out_ref):
    bucket = bucket_ref[...]
    hits = [bucket == b for b in range(NUM_BUCKETS)]
    for head in range(N_Q_HEADS):
        acc = jnp.zeros(bucket.shape, F32)
        for b in range(NUM_BUCKETS):
            acc = jnp.where(hits[b], table_ref[b, head], acc)
        out_ref[head] = acc


def _prompt_bias_body(table_ref, bucket_ref, out_ref):
    bucket = bucket_ref[...]
    hits = [bucket == b for b in range(NUM_BUCKETS)]
    key_chunk = lax.broadcasted_iota(jnp.int32, bucket.shape, 0) // CHUNK
    q_chunk = lax.broadcasted_iota(jnp.int32, bucket.shape, 1) // CHUNK
    in_window = jnp.logical_and(key_chunk >= q_chunk, key_chunk <= q_chunk + WINDOW // CHUNK)
    valid = [in_window, jnp.logical_and(in_window, key_chunk >= WINDOW // CHUNK)]
    for head in range(N_Q_HEADS):
        acc = jnp.zeros(bucket.shape, F32)
        for b in range(NUM_BUCKETS):
            acc = jnp.where(hits[b], table_ref[b, head], acc)
        h, rest = divmod(head, Q_PER_KV)
        pair, par = divmod(rest, HEADS_PER_TILE)
        pg, pl_ = divmod(pair, GROUP_PAIRS)
        for v in range(2):
            out_ref[v, h, pg, par * KBLK:(par + 1) * KBLK, pl_ * QBLK:(pl_ + 1) * QBLK] = (
                jnp.where(valid[v], acc * LOG2E, NEG_INF))


def _prompt_bias(table):
    rel = (np.arange(KBLK) - WINDOW)[:, None] - np.arange(QBLK)[None, :]
    bucket = jnp.asarray(_t5_bucket(rel))
    return pl.pallas_call(
        _prompt_bias_body,
        in_specs=[pl.BlockSpec(memory_space=pltpu.SMEM),
                  pl.BlockSpec(memory_space=pltpu.VMEM)],
        out_specs=pl.BlockSpec(memory_space=pltpu.VMEM),
        out_shape=jax.ShapeDtypeStruct((2, N_KV_HEADS, PAIR_GROUPS, 2 * KBLK, GROUP_PAIRS * QBLK), F32),
        name="rel_bias_prompt",
    )(table, bucket)


def _relative_bias(table, n_q, n_keys, n_past):
    rel = (np.arange(n_keys) - n_past)[None, :] - np.arange(n_q)[:, None]
    bucket = jnp.asarray(_t5_bucket(rel))
    bias = pl.pallas_call(
        _bias_body,
        in_specs=[pl.BlockSpec(memory_space=pltpu.SMEM),
                  pl.BlockSpec(memory_space=pltpu.VMEM)],
        out_specs=pl.BlockSpec(memory_space=pltpu.VMEM),
        out_shape=jax.ShapeDtypeStruct((N_Q_HEADS, n_q, n_keys), F32),
        name="rel_bias_%d" % n_q,
    )(table, bucket)
    return bias.reshape(N_KV_HEADS, Q_PER_KV * n_q, n_keys)


def _sink_attention(qh, kh, vh, bias, sink, invalid=None):
    s = lax.dot_general(qh, kh, (((1,), (1,)), ((), ())), preferred_element_type=F32) + bias
    if invalid is not None:
        s = jnp.where(invalid, NEG_INF, s)
    m = jnp.maximum(jnp.max(s, axis=-1, keepdims=True), sink)
    p = jnp.exp(s - m)
    denom = jnp.sum(p, axis=-1, keepdims=True) + jnp.exp(sink - m)
    return _dot(p.astype(BF16), vh) / denom


def _stack_heads(q, kv_head):
    base = kv_head * Q_PER_KV * HEAD_DIM
    return jnp.concatenate(
        [q[:, base + g * HEAD_DIM: base + (g + 1) * HEAD_DIM] for g in range(Q_PER_KV)], axis=0)


def _unstack_heads(o, n):
    return jnp.concatenate([o[g * n:(g + 1) * n, :] for g in range(Q_PER_KV)], axis=1)


def _merge_out(x1, attn, gm, ga, gb, wba_ref, wbg_ref, wo_ref, g_post):
    merged = ga * _dot(attn, wba_ref[...]) + gb * _dot(gm, wbg_ref[...])
    return x1 + _rms(_dot(merged.astype(BF16), wo_ref[...]), g_post)


def _prompt_mixer_body(ts, q_ref, kv_ref, kvp_ref, packed_ref, x1_ref,
                       bias_ref, sink_ref, ws_ref, bs_ref, wba_ref, wbg_ref, wo_ref, g_ref,
                       out_ref, attn_t_ref, gm_ref):
    u_ref, vn_ref, ga_ref, gb_ref = (packed_ref.at[0, :, k * D_MODEL:(k + 1) * D_MODEL] for k in range(4))
    first_variant = jnp.where(pl.program_id(1) == 0, 1, 0)
    kv_all = jnp.concatenate([kvp_ref[0], kv_ref[0]], axis=0)
    k_all = kv_all[:, :KV_WIDTH]
    v_t = jnp.transpose(kv_all[:, KV_WIDTH:]).astype(BF16)
    low = lax.broadcasted_iota(jnp.int32, k_all.shape, 1) < HEAD_DIM
    k_swapped = pltpu.roll(k_all, HEAD_DIM, axis=1)
    zero = jnp.zeros_like(k_all)
    k_par = [[jnp.where(low, k_all, zero), jnp.where(low, zero, k_swapped)],
             [jnp.where(low, k_swapped, zero), jnp.where(low, zero, k_all)]]
    k_par = [[k.astype(BF16) for k in ks] for ks in k_par]

    units = [(blk, h, pg) for blk in range(ts // QBLK) for h in range(N_KV_HEADS)
             for pg in range(PAIR_GROUPS)]

    def scores(blk, h, pg):
        keys = slice(blk * QBLK, blk * QBLK + KBLK)
        k_blk = jnp.concatenate([k_par[h][0][keys], k_par[h][1][keys]], axis=0)
        pair0 = h * PAIRS_PER_KV + pg * GROUP_PAIRS
        qa = q_ref[blk, pair0 * QBLK:(pair0 + GROUP_PAIRS) * QBLK, :]
        s = lax.dot_general(k_blk, qa, (((1,), (1,)), ((), ())), preferred_element_type=F32)
        variant = first_variant if blk == 0 else 0
        return s + bias_ref[variant, h, pg]

    def softmax(s, blk, h, pg):
        out = []
        for par in range(HEADS_PER_TILE):
            sp = s[par * KBLK:(par + 1) * KBLK]
            row = (h * PAIR_GROUPS + pg) * HEADS_PER_TILE + par
            sink = sink_ref[row:row + 1, :] * LOG2E
            m = jnp.maximum(jnp.max(sp, axis=0, keepdims=True), sink)
            p = jnp.exp2(sp - m)
            denom = jnp.sum(p, axis=0, keepdims=True) + jnp.exp2(sink - m)
            out.append((p.astype(BF16), 1.0 / denom))
        return out

    def weighted_values(probs, blk, h, pg):
        keys = slice(blk * QBLK, blk * QBLK + KBLK)
        cols = slice(blk * QBLK, (blk + 1) * QBLK)
        vh_t = v_t[h * HEAD_DIM:(h + 1) * HEAD_DIM, keys]
        pair0 = h * PAIRS_PER_KV + pg * GROUP_PAIRS
        for par, (p, inv) in enumerate(probs):
            o = _dot(vh_t, p) * inv
            for pl_ in range(GROUP_PAIRS):
                head = (pair0 + pl_) * HEADS_PER_TILE + par
                attn_t_ref[head * HEAD_DIM:(head + 1) * HEAD_DIM, cols] = o[:, pl_ * QBLK:(pl_ + 1) * QBLK]

    blk_i = lax.broadcasted_iota(jnp.int32, (GMLP_CHUNK, GMLP_CHUNK), 0) // CHUNK
    blk_j = lax.broadcasted_iota(jnp.int32, (GMLP_CHUNK, GMLP_CHUNK), 1) // CHUNK
    for g in range(GMLP_GROUPS):
        w = jnp.where(blk_j <= blk_i, ws_ref[g], 0.0).astype(BF16)
        b = bs_ref[:, g:g + 1]
        cols = slice(g * GMLP_GROUP_DIM, (g + 1) * GMLP_GROUP_DIM)
        for c in range(ts // GMLP_CHUNK):
            rows = slice(c * GMLP_CHUNK, (c + 1) * GMLP_CHUNK)
            sp = _dot(w, vn_ref[rows, cols]) + b
            gm_ref[rows, cols] = (u_ref[rows, cols].astype(F32) * sp).astype(BF16)

    def gmlp_branch(c):
        cols = slice(c * SIDE_COLS, (c + 1) * SIDE_COLS)
        return gb_ref[:, cols].astype(F32) * _dot(gm_ref[...], wbg_ref[:, cols])

    n_side = D_MODEL // SIDE_COLS
    side = []
    s_vals, p_vals = {}, {}
    for i in range(len(units) + 2):
        if i < len(units):
            s_vals[i] = scores(*units[i])
        if 1 <= i <= len(units):
            p_vals[i - 1] = softmax(s_vals.pop(i - 1), *units[i - 1])
        if i >= 2:
            weighted_values(p_vals.pop(i - 2), *units[i - 2])
        if i % (len(units) // n_side) == 1 and len(side) < n_side:
            side.append(gmlp_branch(len(side)))
    assert len(side) == n_side

    attn = jnp.transpose(attn_t_ref[...]).astype(BF16)
    merged = ga_ref[...].astype(F32) * _dot(attn, wba_ref[...]) + jnp.concatenate(side, axis=1)
    out_ref[0] = x1_ref[0] + _rms(_dot(merged.astype(BF16), wo_ref[...]), g_ref[...])


def _prompt_mixer(q, kv, packed, x1, bias, sink, w_s, b_s_t, wba, wbg, wo, g_post, *, ts):
    batch, seq, _ = kv.shape
    assert seq % ts == 0 and ts % QBLK == 0 and QBLK == GMLP_CHUNK == WINDOW
    tile = lambda b, t: (b, t, 0)
    prev = lambda b, t: (b, jnp.maximum(t * (ts // WINDOW) - 1, 0), 0)
    wide = pl.BlockSpec((1, ts, D_MODEL), tile)
    n_t = seq // ts
    q_spec = pl.BlockSpec((ts // QBLK, N_HEAD_PAIRS * QBLK, LANES), lambda b, t: (b * n_t + t, 0, 0))
    return pl.pallas_call(
        functools.partial(_prompt_mixer_body, ts),
        grid=(batch, n_t),
        in_specs=[q_spec, pl.BlockSpec((1, ts, 2 * KV_WIDTH), tile),
                  pl.BlockSpec((1, WINDOW, 2 * KV_WIDTH), prev),
                  pl.BlockSpec((1, ts, 4 * D_MODEL), tile), wide,
                  _resident(bias.shape), _resident(sink.shape), _resident(w_s.shape),
                  _resident(b_s_t.shape), _resident(wba.shape), _resident(wbg.shape),
                  _resident(wo.shape), _resident(g_post.shape)],
        out_specs=wide,
        out_shape=jax.ShapeDtypeStruct((batch, seq, D_MODEL), F32),
        scratch_shapes=[pltpu.VMEM((ATTN_WIDTH, ts), F32), pltpu.VMEM((ts, GMLP_WIDTH), BF16)],
        compiler_params=_params(2),
        name="prompt_mixer",
    )(q, kv, kv, packed, x1, bias, sink, w_s, b_s_t, wba, wbg, wo, g_post)


def _sample_mixer_body(n_batch, n_new, q_ref, kv_ref, ck_ref, cv_ref, u_ref, vn_ref, ga_ref, gb_ref,
                       x1_ref, bias_ref, sink_ref, ws_ref, bs_ref, wba_ref, wbg_ref, wo_ref, g_ref,
                       out_ref, attn_ref, gm_ref):
    for b in range(n_batch):
        rows = slice(b * n_new, (b + 1) * n_new)
        q = q_ref[rows, :]
        kv = kv_ref[rows, :]
        k_all = jnp.concatenate([ck_ref[b], kv[:, :KV_WIDTH]], axis=0).astype(BF16)
        v_all = jnp.concatenate([cv_ref[b], kv[:, KV_WIDTH:]], axis=0).astype(BF16)
        for h in range(N_KV_HEADS):
            cols = slice(h * HEAD_DIM, (h + 1) * HEAD_DIM)
            o = _sink_attention(_stack_heads(q, h), k_all[:, cols], v_all[:, cols],
                                bias_ref[h], sink_ref[h])
            width = Q_PER_KV * HEAD_DIM
            attn_ref[rows, h * width:(h + 1) * width] = _unstack_heads(o, n_new).astype(BF16)
        for g in range(GMLP_GROUPS):
            cols = slice(g * GMLP_GROUP_DIM, (g + 1) * GMLP_GROUP_DIM)
            w = ws_ref[g, :n_new, :n_new].astype(BF16)
            sp = _dot(w, vn_ref[rows, cols].astype(BF16)) + bs_ref[:n_new, g:g + 1]
            gm_ref[rows, cols] = (u_ref[rows, cols].astype(F32) * sp).astype(BF16)

    out_ref[...] = _merge_out(x1_ref[...], attn_ref[...], gm_ref[...],
                              ga_ref[...].astype(F32), gb_ref[...].astype(F32),
                              wba_ref, wbg_ref, wo_ref, g_ref[...])


def _sample_mixer(q, kv, cache_k, cache_v, u, vn, ga, gb, x1, bias, sink, w_s, b_s_t,
                  wba, wbg, wo, g_post, *, n_batch, n_new):
    assert n_new <= CHUNK
    t = q.shape[0]
    vmem = pl.BlockSpec(memory_space=pltpu.VMEM)
    return pl.pallas_call(
        functools.partial(_sample_mixer_body, n_batch, n_new),
        in_specs=[vmem] * 17,
        out_specs=vmem,
        out_shape=jax.ShapeDtypeStruct((t, D_MODEL), F32),
        scratch_shapes=[pltpu.VMEM((t, ATTN_WIDTH), BF16), pltpu.VMEM((t, GMLP_WIDTH), BF16)],
        compiler_params=pltpu.CompilerParams(vmem_limit_bytes=VMEM_LIMIT_BYTES),
        name="sample_mixer",
    )(q, kv, cache_k, cache_v, u, vn, ga, gb, x1, bias, sink, w_s, b_s_t, wba, wbg, wo, g_post)


def _prompt_sink_rows(sinks):
    s = sinks.astype(F32).reshape(N_KV_HEADS, PAIR_GROUPS, GROUP_PAIRS, HEADS_PER_TILE)
    s = jnp.transpose(s, (0, 1, 3, 2))[..., None]
    s = jnp.broadcast_to(s, (N_KV_HEADS, PAIR_GROUPS, HEADS_PER_TILE, GROUP_PAIRS, QBLK))
    return s.reshape(N_KV_HEADS * PAIR_GROUPS * HEADS_PER_TILE, GROUP_PAIRS * QBLK)


def _sink_rows(sinks, n_q):
    s = jnp.broadcast_to(sinks.astype(F32).reshape(N_KV_HEADS, Q_PER_KV, 1), (N_KV_HEADS, Q_PER_KV, n_q))
    return s.reshape(N_KV_HEADS, Q_PER_KV * n_q, 1)


PROMPT_TM = 512
PROMPT_TS = 512


def kernel(x_prompt, x_sample, cache_win_k, cache_win_v, rel_bias_table, norm_gains, ffn1_w_gate, ffn1_w_up, ffn1_w_down, w_in, attn_sinks, gmlp_ln_g, gmlp_ln_b, gmlp_w_s, gmlp_b_s, w_branch_attn, w_branch_gmlp, w_out, ffn2_w_gate, ffn2_w_up, ffn2_w_down):
    depth = norm_gains.shape[0]
    batch, seq, _ = x_prompt.shape
    dec_batch, dec_seq, _ = x_sample.shape
    n_cache = cache_win_k.shape[2]
    assert seq % PROMPT_TS == 0 and dec_seq <= CHUNK

    bias_p = _prompt_bias(rel_bias_table)
    bias_s = _relative_bias(rel_bias_table, dec_seq, n_cache + dec_seq, n_cache)

    xp = x_prompt.reshape(batch * seq, D_MODEL)
    xs = x_sample.reshape(dec_batch * dec_seq, D_MODEL)
    t_s = xs.shape[0]
    kp, vp, ks, vs, gs = [], [], [], [], []
    for l in range(depth):
        g = norm_gains[l].astype(F32)
        win, wba, wbg, wo = _to_bf16(w_in[l], w_branch_attn[l], w_branch_gmlp[l], w_out[l])
        ln = jnp.stack([gmlp_ln_g[l], gmlp_ln_b[l]]).astype(F32)
        w_s = gmlp_w_s[l].astype(F32)
        b_s_t = jnp.transpose(gmlp_b_s[l]).astype(F32)
        g_ffn1, g_post2, g_ffn2 = g[0:3], g[3:4], g[4:6]
        sink_p = _prompt_sink_rows(attn_sinks[l])
        sink_s = _sink_rows(attn_sinks[l], dec_seq)

        x1, h2, *w1 = _ffn_stream(xs, g_ffn1, ffn1_w_gate[l], ffn1_w_up[l], ffn1_w_down[l],
                                  emit_next=True, name="ffn1_sample")
        q, kv, u, vn, ga, gb = _in_proj(h2, win, ln, tm=t_s, vn_dtype=F32, q_blocked=False,
                                        name="in_proj_sample")
        ck = cache_win_k[l].reshape(dec_batch, n_cache, KV_WIDTH)
        cv = cache_win_v[l].reshape(dec_batch, n_cache, KV_WIDTH)
        x2 = _sample_mixer(q, kv, ck, cv, u, vn, ga, gb, x1, bias_s, sink_s, w_s, b_s_t,
                           wba, wbg, wo, g_post2, n_batch=dec_batch, n_new=dec_seq)
        xs, *w2 = _ffn_stream(x2, g_ffn2, ffn2_w_gate[l], ffn2_w_up[l], ffn2_w_down[l],
                              emit_next=False, name="ffn2_sample")
        ks.append(kv[:, :KV_WIDTH].reshape(dec_batch, dec_seq, N_KV_HEADS, HEAD_DIM))
        vs.append(kv[:, KV_WIDTH:].reshape(dec_batch, dec_seq, N_KV_HEADS, HEAD_DIM))
        gs.append(vn.reshape(dec_batch, dec_seq, GMLP_WIDTH))

        x1, h2 = _ffn(xp, g_ffn1, *w1, tm=PROMPT_TM, emit_next=True, name="ffn1_prompt")
        q, kv, packed = _in_proj(h2, win, ln, tm=2 * PROMPT_TM, vn_dtype=BF16, q_blocked=True,
                                 name="in_proj_prompt")
        b3 = lambda a: a.reshape(batch, seq, a.shape[-1])
        x2 = _prompt_mixer(q, b3(kv), b3(packed), b3(x1), bias_p, sink_p,
                           w_s, b_s_t, wba, wbg, wo, g_post2, ts=PROMPT_TS)
        (xp,) = _ffn(x2.reshape(batch * seq, D_MODEL), g_ffn2, *w2, tm=PROMPT_TM, emit_next=False,
                     name="ffn2_prompt")
        kv_win = b3(kv)[:, seq - WINDOW:, :]
        kp.append(kv_win[..., :KV_WIDTH].reshape(batch, WINDOW, N_KV_HEADS, HEAD_DIM))
        vp.append(kv_win[..., KV_WIDTH:].reshape(batch, WINDOW, N_KV_HEADS, HEAD_DIM))

    return (xp.reshape(batch, seq, D_MODEL), xs.reshape(dec_batch, dec_seq, D_MODEL),
            jnp.stack(kp), jnp.stack(vp), jnp.stack(ks), jnp.stack(vs), jnp.stack(gs))
```

```python
import functools
import math

import jax
import jax.numpy as jnp
import numpy as np
from jax import lax
from jax.experimental import pallas as pl
from jax.experimental.pallas import tpu as pltpu

D_MODEL = 1024
CHUNK = 64
N_Q_HEADS = 16
N_KV_HEADS = 2
HEAD_DIM = 64
Q_PER_KV = N_Q_HEADS // N_KV_HEADS
ATTN_WIDTH = N_Q_HEADS * HEAD_DIM
KV_WIDTH = N_KV_HEADS * HEAD_DIM
WINDOW = 128
GMLP_WIDTH = 1024
GMLP_GROUPS = 4
GMLP_GROUP_DIM = GMLP_WIDTH // GMLP_GROUPS
GMLP_CHUNK = 128
NUM_BUCKETS = 32
MAX_DISTANCE = 128
D_FF = 2816
EPS = 1e-6
NEG_INF = -1e30
LOG2E = math.log2(math.e)

OFF_Q = 0
OFF_KV = ATTN_WIDTH
OFF_U = OFF_KV + 2 * KV_WIDTH
OFF_GV = OFF_U + GMLP_WIDTH
OFF_GA = OFF_GV + GMLP_WIDTH
OFF_GB = OFF_GA + D_MODEL

V7X_VMEM_BYTES = 64 * 1024 * 1024
VMEM_LIMIT_BYTES = V7X_VMEM_BYTES - 8 * 1024 * 1024
MXU_TILE = 256
LANES = 128

QBLK = 2 * CHUNK
KBLK = WINDOW + QBLK
HEADS_PER_TILE = LANES // HEAD_DIM
N_HEAD_PAIRS = N_Q_HEADS // HEADS_PER_TILE
PAIRS_PER_KV = Q_PER_KV // HEADS_PER_TILE
GROUP_PAIRS = 4
PAIR_GROUPS = PAIRS_PER_KV // GROUP_PAIRS
SIDE_COLS = MXU_TILE

BF16 = jnp.bfloat16
F32 = jnp.float32


def _dot(a, b):
    return jnp.dot(a, b, preferred_element_type=F32)


def _rms(x, g):
    return x * lax.rsqrt(jnp.mean(x * x, axis=-1, keepdims=True) + EPS) * g


def _resident(shape):
    zeros = (0,) * len(shape)
    return pl.BlockSpec(shape, lambda *_: zeros, pipeline_mode=pl.Buffered(1))


def _params(n_axes):
    return pltpu.CompilerParams(dimension_semantics=("arbitrary",) * n_axes,
                                vmem_limit_bytes=VMEM_LIMIT_BYTES)


CAST_BLOCK_BYTES = 2 * 1024 * 1024


def _cast_body(*refs):
    n = len(refs) // 2
    for w_ref, o_ref in zip(refs[:n], refs[n:]):
        o_ref[...] = w_ref[...].astype(o_ref.dtype)


def _to_bf16(*ws):
    k = ws[0].shape[0]
    assert all(w.ndim == 2 and w.shape[0] == k for w in ws)
    n_max = max(w.shape[1] for w in ws)
    rows = [r for r in range(16, k + 1, 16) if k % r == 0 and r * n_max * 4 <= CAST_BLOCK_BYTES]
    bk = max(rows) if rows else k
    specs = [pl.BlockSpec((bk, w.shape[1]), lambda i: (i, 0)) for w in ws]
    return pl.pallas_call(
        _cast_body,
        grid=(k // bk,),
        in_specs=specs,
        out_specs=specs,
        out_shape=[jax.ShapeDtypeStruct(w.shape, BF16) for w in ws],
        compiler_params=_params(1),
        name="weights_to_bf16",
    )(*ws)


def _ordering_zero(*arrays):
    m = None
    for a in arrays:
        r = jnp.max(jnp.max(a.astype(F32), axis=0, keepdims=True), axis=1, keepdims=True)
        m = r if m is None else jnp.maximum(m, r)
    bits = lax.bitcast_convert_type(m, jnp.uint32)
    return lax.bitcast_convert_type((bits >> 16) >> 16, F32)


FF_CHUNK = MXU_TILE
FFN_SIDE_PIECES = 8


def _ffn_body(n_tiles, emit_next, xp_ref, xe_ref, g_ref, wg_ref, wu_ref, wd_ref, *refs):
    y_ref = refs[0]
    h_ref, acc_ref, act_ref = refs[-3:]
    s = pl.program_id(0)
    slot = s % 2
    tm = y_ref.shape[0]

    def pre_norm(dst, rows):
        h_ref[dst, rows, :] = _rms(xp_ref[rows, :], g_ref[0:1, :]).astype(BF16)
        return [h_ref[dst, rows, :]]

    def finish(rows):
        y = xe_ref[rows, :] + 0.5 * _rms(acc_ref[rows, :], g_ref[1:2, :])
        y_ref[rows, :] = y
        stored = [y_ref[rows, :]]
        if emit_next:
            refs[1][rows, :] = _rms(y, g_ref[2:3, :]).astype(BF16)
            stored.append(refs[1][rows, :])
        return stored

    @pl.when(s == 0)
    def _():
        pre_norm(0, slice(None))
        acc_ref[...] = jnp.zeros_like(acc_ref)

    @pl.when(jnp.logical_and(s >= 1, s <= n_tiles))
    def _():
        h = h_ref[1 - slot]
        acc = None
        anchor = None
        piece_rows = tm // FFN_SIDE_PIECES
        assert D_FF // FF_CHUNK > FFN_SIDE_PIECES
        for ci in range(D_FF // FF_CHUNK):
            cols = slice(ci * FF_CHUNK, (ci + 1) * FF_CHUNK)
            gate = _dot(h, wg_ref[:, cols])
            up = _dot(h, wu_ref[:, cols])
            if anchor is not None:
                up = up + anchor
            act_ref[:, cols] = (jax.nn.silu(gate) * up).astype(BF16)
            anchor = None
            if ci < FFN_SIDE_PIECES:
                rows = slice(ci * piece_rows, (ci + 1) * piece_rows)
                anchor = _ordering_zero(*(finish(rows) + pre_norm(slot, rows)))
        acc_ref[...] = _dot(act_ref[...], wd_ref[...])

    @pl.when(s == n_tiles + 1)
    def _():
        finish(slice(None))


def _ffn(x, gains, wg, wu, wd, *, tm, emit_next, name):
    t = x.shape[0]
    assert t % tm == 0
    n_tiles = t // tm
    head = lambda s: (jnp.minimum(s, n_tiles - 1), 0)
    tail = lambda s: (jnp.clip(s - 2, 0, n_tiles - 1), 0)
    out_shape = [jax.ShapeDtypeStruct((t, D_MODEL), F32)]
    out_specs = [pl.BlockSpec((tm, D_MODEL), tail)]
    if emit_next:
        out_shape.append(jax.ShapeDtypeStruct((t, D_MODEL), BF16))
        out_specs.append(pl.BlockSpec((tm, D_MODEL), tail))
    return pl.pallas_call(
        functools.partial(_ffn_body, n_tiles, emit_next),
        grid=(n_tiles + 2,),
        in_specs=[pl.BlockSpec((tm, D_MODEL), head), pl.BlockSpec((tm, D_MODEL), tail),
                  _resident(gains.shape), _resident(wg.shape), _resident(wu.shape), _resident(wd.shape)],
        out_specs=out_specs,
        out_shape=out_shape,
        scratch_shapes=[pltpu.VMEM((2, tm, D_MODEL), BF16), pltpu.VMEM((tm, D_MODEL), F32),
                        pltpu.VMEM((tm, D_FF), BF16)],
        compiler_params=_params(1),
        name=name,
    )(x, x, gains, wg, wu, wd)


def _ffn_stream_body(emit_next, x_ref, g_ref, wg_ref, wu_ref, wd_ref, *refs):
    n_out = 2 if emit_next else 1
    y_ref = refs[0]
    wg16_ref, wu16_ref, wd16_ref = refs[n_out:n_out + 3]
    h_ref, acc_ref = refs[-2:]
    c = pl.program_id(0)

    @pl.when(c == 0)
    def _():
        h_ref[...] = _rms(x_ref[...], g_ref[0:1, :]).astype(BF16)
        acc_ref[...] = jnp.zeros_like(acc_ref)

    wg, wu, wd = (r[...].astype(BF16) for r in (wg_ref, wu_ref, wd_ref))
    wg16_ref[...], wu16_ref[...], wd16_ref[...] = wg, wu, wd
    h = h_ref[...]
    act = (jax.nn.silu(_dot(h, wg)) * _dot(h, wu)).astype(BF16)
    acc_ref[...] += _dot(act, wd)

    @pl.when(c == pl.num_programs(0) - 1)
    def _():
        y = x_ref[...] + 0.5 * _rms(acc_ref[...], g_ref[1:2, :])
        y_ref[...] = y
        if emit_next:
            refs[1][...] = _rms(y, g_ref[2:3, :]).astype(BF16)


def _ffn_stream(x, gains, wg, wu, wd, *, emit_next, name):
    t = x.shape[0]
    whole = lambda shape: pl.BlockSpec(shape, lambda c: (0, 0))
    col = pl.BlockSpec((D_MODEL, FF_CHUNK), lambda c: (0, c))
    row = pl.BlockSpec((FF_CHUNK, D_MODEL), lambda c: (c, 0))
    out_shape = [jax.ShapeDtypeStruct((t, D_MODEL), F32)]
    if emit_next:
        out_shape.append(jax.ShapeDtypeStruct((t, D_MODEL), BF16))
    out_specs = [whole((t, D_MODEL))] * len(out_shape) + [col, col, row]
    out_shape += [jax.ShapeDtypeStruct(w.shape, BF16) for w in (wg, wu, wd)]
    return pl.pallas_call(
        functools.partial(_ffn_stream_body, emit_next),
        grid=(D_FF // FF_CHUNK,),
        in_specs=[whole((t, D_MODEL)), whole(gains.shape), col, col, row],
        out_specs=out_specs,
        out_shape=out_shape,
        scratch_shapes=[pltpu.VMEM((t, D_MODEL), BF16), pltpu.VMEM((t, D_MODEL), F32)],
        compiler_params=_params(1),
        name=name,
    )(x, gains, wg, wu, wd)


def _gelu(x):
    return 0.5 * x * (1.0 + lax.erf(x * np.sqrt(0.5).astype(np.float32)))


def _layer_norm(x, g, b):
    mu = jnp.mean(x, axis=-1, keepdims=True)
    xc = x - mu
    var = jnp.mean(xc * xc, axis=-1, keepdims=True)
    return xc * lax.rsqrt(var + EPS) * g + b


IN_PROJ_LN_PIECES = 8


def _in_proj_body(q_blocked, h_ref, w_ref, ln_ref, q_ref, kv_ref, *refs):
    if q_blocked:
        packed_ref, gv_ref = refs
        u_ref, vn_ref, ga_ref, gb_ref = (packed_ref.at[:, k * D_MODEL:(k + 1) * D_MODEL] for k in range(4))
    else:
        u_ref, vn_ref, ga_ref, gb_ref, gv_ref = refs
    _in_proj_compute(q_blocked, h_ref, w_ref, ln_ref, q_ref, kv_ref, u_ref, vn_ref, ga_ref, gb_ref, gv_ref)


def _in_proj_compute(q_blocked, h_ref, w_ref, ln_ref, q_ref, kv_ref, u_ref, vn_ref, ga_ref, gb_ref, gv_ref):
    h = h_ref[...]
    tm = h.shape[0]
    n_chunks = D_MODEL // MXU_TILE
    q_scale = HEAD_DIM ** -0.5 * (LOG2E if q_blocked else 1.0)

    def store_q(c, r):
        q = (r * q_scale).astype(q_ref.dtype)
        if q_blocked:
            for b in range(tm // QBLK):
                for j in range(MXU_TILE // LANES):
                    p = c * (MXU_TILE // LANES) + j
                    q_ref[b, p * QBLK:(p + 1) * QBLK, :] = q[b * QBLK:(b + 1) * QBLK, j * LANES:(j + 1) * LANES]
        else:
            q_ref[:, c * MXU_TILE:(c + 1) * MXU_TILE] = q

    def store_cols(ref, fn):
        def store(c, r):
            ref[:, c * MXU_TILE:(c + 1) * MXU_TILE] = fn(r).astype(ref.dtype)
        return store

    def layer_norm_piece(k):
        rows = slice(k * (tm // IN_PROJ_LN_PIECES), (k + 1) * (tm // IN_PROJ_LN_PIECES))
        vn_ref[rows, :] = _layer_norm(_gelu(gv_ref[rows, :]), ln_ref[0:1, :], ln_ref[1:2, :]).astype(vn_ref.dtype)

    def run(off, c, store):
        store(c, _dot(h, w_ref[:, off + c * MXU_TILE: off + (c + 1) * MXU_TILE]))

    store_gv = lambda c, r: gv_ref.__setitem__((slice(None), slice(c * MXU_TILE, (c + 1) * MXU_TILE)), r)
    store_u, store_ga, store_gb = (store_cols(u_ref, _gelu), store_cols(ga_ref, jax.nn.sigmoid),
                                   store_cols(gb_ref, jax.nn.sigmoid))
    for c in range(n_chunks):
        run(OFF_GV, c, store_gv)
    run(OFF_KV, 0, lambda c, r: kv_ref.__setitem__(Ellipsis, r))
    for c in range(n_chunks):
        run(OFF_Q, c, store_q)
        layer_norm_piece(2 * c)
        run(OFF_GA, c, store_ga)
        layer_norm_piece(2 * c + 1)
    for c in range(n_chunks):
        run(OFF_U, c, store_u)
        run(OFF_GB, c, store_gb)


def _in_proj(h, w_in, ln, *, tm, vn_dtype, q_blocked, name):
    t = h.shape[0]
    assert t % tm == 0
    row = lambda i: (i, 0)
    wide = lambda dt: jax.ShapeDtypeStruct((t, D_MODEL), dt)
    wide_spec = pl.BlockSpec((tm, D_MODEL), row)
    if q_blocked:
        assert tm % QBLK == 0 and vn_dtype == BF16
        q_shape = jax.ShapeDtypeStruct((t // QBLK, N_HEAD_PAIRS * QBLK, LANES), BF16)
        q_spec = pl.BlockSpec((tm // QBLK, N_HEAD_PAIRS * QBLK, LANES), lambda i: (i, 0, 0))
        act_shapes = [jax.ShapeDtypeStruct((t, 4 * D_MODEL), BF16)]
        act_specs = [pl.BlockSpec((tm, 4 * D_MODEL), row)]
    else:
        q_shape, q_spec = wide(BF16), wide_spec
        act_shapes = [wide(BF16), wide(vn_dtype), wide(BF16), wide(BF16)]
        act_specs = [wide_spec] * 4
    return pl.pallas_call(
        functools.partial(_in_proj_body, q_blocked),
        grid=(t // tm,),
        in_specs=[wide_spec, _resident(w_in.shape), _resident(ln.shape)],
        out_specs=[q_spec, pl.BlockSpec((tm, 2 * KV_WIDTH), row)] + act_specs,
        out_shape=[q_shape, jax.ShapeDtypeStruct((t, 2 * KV_WIDTH), F32)] + act_shapes,
        scratch_shapes=[pltpu.VMEM((tm, GMLP_WIDTH), F32)],
        compiler_params=_params(1),
        name=name,
    )(h, w_in, ln)


def _t5_bucket(rel):
    half = NUM_BUCKETS // 2
    max_exact = half // 2
    ret = np.where(rel > 0, half, 0)
    n = np.abs(rel)
    nf = np.maximum(n, 1).astype(np.float32)
    scaled = (np.log(nf / np.float32(max_exact)) / np.float32(math.log(MAX_DISTANCE / max_exact))
              * np.float32(half - max_exact))
    exact = np.log(np.maximum(n, 1) / max_exact) / math.log(MAX_DISTANCE / max_exact) * (half - max_exact)
    assert np.array_equal(scaled.astype(np.int32), exact.astype(np.int32))
    large = np.minimum(max_exact + scaled.astype(np.int32), half - 1)
    return (ret + np.where(n < max_exact, n, large)).astype(np.int32)


def _bias_body(table_ref, bucket_ref, out_ref):
    bucket = bucket_ref[...]
    hits = [bucket == b for b in range(NUM_BUCKETS)]
    for head in range(N_Q_HEADS):
        acc = jnp.zeros(bucket.shape, F32)
        for b in range(NUM_BUCKETS):
            acc = jnp.where(hits[b], table_ref[b, head], acc)
        out_ref[head] = acc


def _prompt_bias_body(table_ref, bucket_ref, out_ref):
    bucket = bucket_ref[...]
    hits = [bucket == b for b in range(NUM_BUCKETS)]
    key_chunk = lax.broadcasted_iota(jnp.int32, bucket.shape, 0) // CHUNK
    q_chunk = lax.broadcasted_iota(jnp.int32, bucket.shape, 1) // CHUNK
    in_window = jnp.logical_and(key_chunk >= q_chunk, key_chunk <= q_chunk + WINDOW // CHUNK)
    valid = [in_window, jnp.logical_and(in_window, key_chunk >= WINDOW // CHUNK)]
    for head in range(N_Q_HEADS):
        acc = jnp.zeros(bucket.shape, F32)
        for b in range(NUM_BUCKETS):
            acc = jnp.where(hits[b], table_ref[b, head], acc)
        h, rest = divmod(head, Q_PER_KV)
        pair, par = divmod(rest, HEADS_PER_TILE)
        pg, pl_ = divmod(pair, GROUP_PAIRS)
        for v in range(2):
            out_ref[v, h, pg, par * KBLK:(par + 1) * KBLK, pl_ * QBLK:(pl_ + 1) * QBLK] = (
                jnp.where(valid[v], acc * LOG2E, NEG_INF))


def _prompt_bias(table):
    rel = (np.arange(KBLK) - WINDOW)[:, None] - np.arange(QBLK)[None, :]
    bucket = jnp.asarray(_t5_bucket(rel))
    return pl.pallas_call(
        _prompt_bias_body,
        in_specs=[pl.BlockSpec(memory_space=pltpu.SMEM),
                  pl.BlockSpec(memory_space=pltpu.VMEM)],
        out_specs=pl.BlockSpec(memory_space=pltpu.VMEM),
        out_shape=jax.ShapeDtypeStruct((2, N_KV_HEADS, PAIR_GROUPS, 2 * KBLK, GROUP_PAIRS * QBLK), F32),
        name="rel_bias_prompt",
    )(table, bucket)


def _relative_bias(table, n_q, n_keys, n_past):
    rel = (np.arange(n_keys) - n_past)[None, :] - np.arange(n_q)[:, None]
    bucket = jnp.asarray(_t5_bucket(rel))
    bias = pl.pallas_call(
        _bias_body,
        in_specs=[pl.BlockSpec(memory_space=pltpu.SMEM),
                  pl.BlockSpec(memory_space=pltpu.VMEM)],
        out_specs=pl.BlockSpec(memory_space=pltpu.VMEM),
        out_shape=jax.ShapeDtypeStruct((N_Q_HEADS, n_q, n_keys), F32),
        name="rel_bias_%d" % n_q,
    )(table, bucket)
    return bias.reshape(N_KV_HEADS, Q_PER_KV * n_q, n_keys)


def _stack_heads(q, kv_head):
    base = kv_head * Q_PER_KV * HEAD_DIM
    return jnp.concatenate(
        [q[:, base + g * HEAD_DIM: base + (g + 1) * HEAD_DIM] for g in range(Q_PER_KV)], axis=0)


def _unstack_heads(o, n):
    return jnp.concatenate([o[g * n:(g + 1) * n, :] for g in range(Q_PER_KV)], axis=1)


def _merge_out(x1, attn, gm, ga, gb, wba_ref, wbg_ref, wo_ref, g_post):
    merged = ga * _dot(attn, wba_ref[...]) + gb * _dot(gm, wbg_ref[...])
    return x1 + _rms(_dot(merged.astype(BF16), wo_ref[...]), g_post)


def _prompt_mixer_body(ts, q_ref, kv_ref, kvp_ref, packed_ref, x1_ref,
                       bias_ref, sink_ref, ws_ref, bs_ref, wba_ref, wbg_ref, wo_ref, g_ref,
                       out_ref, attn_t_ref, gm_ref):
    u_ref, vn_ref, ga_ref, gb_ref = (packed_ref.at[0, :, k * D_MODEL:(k + 1) * D_MODEL] for k in range(4))
    first_variant = jnp.where(pl.program_id(1) == 0, 1, 0)
    kv_all = jnp.concatenate([kvp_ref[0], kv_ref[0]], axis=0)
    k_all = kv_all[:, :KV_WIDTH]
    v_t = jnp.transpose(kv_all[:, KV_WIDTH:]).astype(BF16)
    low = lax.broadcasted_iota(jnp.int32, k_all.shape, 1) < HEAD_DIM
    k_swapped = pltpu.roll(k_all, HEAD_DIM, axis=1)
    zero = jnp.zeros_like(k_all)
    k_par = [[jnp.where(low, k_all, zero), jnp.where(low, zero, k_swapped)],
             [jnp.where(low, k_swapped, zero), jnp.where(low, zero, k_all)]]
    k_par = [[k.astype(BF16) for k in ks] for ks in k_par]

    units = [(blk, h, pg) for blk in range(ts // QBLK) for h in range(N_KV_HEADS)
             for pg in range(PAIR_GROUPS)]

    def scores(blk, h, pg):
        keys = slice(blk * QBLK, blk * QBLK + KBLK)
        k_blk = jnp.concatenate([k_par[h][0][keys], k_par[h][1][keys]], axis=0)
        pair0 = h * PAIRS_PER_KV + pg * GROUP_PAIRS
        qa = q_ref[blk, pair0 * QBLK:(pair0 + GROUP_PAIRS) * QBLK, :]
        s = lax.dot_general(k_blk, qa, (((1,), (1,)), ((), ())), preferred_element_type=F32)
        variant = first_variant if blk == 0 else 0
        return s + bias_ref[variant, h, pg]

    def softmax(s, blk, h, pg):
        out = []
        for par in range(HEADS_PER_TILE):
            sp = s[par * KBLK:(par + 1) * KBLK]
            row = (h * PAIR_GROUPS + pg) * HEADS_PER_TILE + par
            sink = sink_ref[row:row + 1, :] * LOG2E
            m = jnp.maximum(jnp.max(sp, axis=0, keepdims=True), sink)
            p = jnp.exp2(sp - m)
            denom = jnp.sum(p, axis=0, keepdims=True) + jnp.exp2(sink - m)
            out.append((p.astype(BF16), 1.0 / denom))
        return out

    def weighted_values(probs, blk, h, pg):
        keys = slice(blk * QBLK, blk * QBLK + KBLK)
        cols = slice(blk * QBLK, (blk + 1) * QBLK)
        vh_t = v_t[h * HEAD_DIM:(h + 1) * HEAD_DIM, keys]
        pair0 = h * PAIRS_PER_KV + pg * GROUP_PAIRS
        for par, (p, inv) in enumerate(probs):
            o = _dot(vh_t, p) * inv
            for pl_ in range(GROUP_PAIRS):
                head = (pair0 + pl_) * HEADS_PER_TILE + par
                attn_t_ref[head * HEAD_DIM:(head + 1) * HEAD_DIM, cols] = o[:, pl_ * QBLK:(pl_ + 1) * QBLK]

    blk_i = lax.broadcasted_iota(jnp.int32, (GMLP_CHUNK, GMLP_CHUNK), 0) // CHUNK
    blk_j = lax.broadcasted_iota(jnp.int32, (GMLP_CHUNK, GMLP_CHUNK), 1) // CHUNK
    for g in range(GMLP_GROUPS):
        w = jnp.where(blk_j <= blk_i, ws_ref[g], 0.0).astype(BF16)
        b = bs_ref[:, g:g + 1]
        cols = slice(g * GMLP_GROUP_DIM, (g + 1) * GMLP_GROUP_DIM)
        for c in range(ts // GMLP_CHUNK):
            rows = slice(c * GMLP_CHUNK, (c + 1) * GMLP_CHUNK)
            sp = _dot(w, vn_ref[rows, cols]) + b
            gm_ref[rows, cols] = (u_ref[rows, cols].astype(F32) * sp).astype(BF16)

    def gmlp_branch(c):
        cols = slice(c * SIDE_COLS, (c + 1) * SIDE_COLS)
        return gb_ref[:, cols].astype(F32) * _dot(gm_ref[...], wbg_ref[:, cols])

    n_side = D_MODEL // SIDE_COLS
    side = []
    s_vals, p_vals = {}, {}
    for i in range(len(units) + 2):
        if i < len(units):
            s_vals[i] = scores(*units[i])
        if 1 <= i <= len(units):
            p_vals[i - 1] = softmax(s_vals.pop(i - 1), *units[i - 1])
        if i >= 2:
            weighted_values(p_vals.pop(i - 2), *units[i - 2])
        if i % (len(units) // n_side) == 1 and len(side) < n_side:
            side.append(gmlp_branch(len(side)))
    assert len(side) == n_side

    attn = jnp.transpose(attn_t_ref[...]).astype(BF16)
    merged = ga_ref[...].astype(F32) * _dot(attn, wba_ref[...]) + jnp.concatenate(side, axis=1)
    out_ref[0] = x1_ref[0] + _rms(_dot(merged.astype(BF16), wo_ref[...]), g_ref[...])


def _prompt_mixer(q, kv, packed, x1, bias, sink, w_s, b_s_t, wba, wbg, wo, g_post, *, ts):
    batch, seq, _ = kv.shape
    assert seq % ts == 0 and ts % QBLK == 0 and QBLK == GMLP_CHUNK == WINDOW
    tile = lambda b, t: (b, t, 0)
    prev = lambda b, t: (b, jnp.maximum(t * (ts // WINDOW) - 1, 0), 0)
    wide = pl.BlockSpec((1, ts, D_MODEL), tile)
    n_t = seq // ts
    q_spec = pl.BlockSpec((ts // QBLK, N_HEAD_PAIRS * QBLK, LANES), lambda b, t: (b * n_t + t, 0, 0))
    return pl.pallas_call(
        functools.partial(_prompt_mixer_body, ts),
        grid=(batch, n_t),
        in_specs=[q_spec, pl.BlockSpec((1, ts, 2 * KV_WIDTH), tile),
                  pl.BlockSpec((1, WINDOW, 2 * KV_WIDTH), prev),
                  pl.BlockSpec((1, ts, 4 * D_MODEL), tile), wide,
                  _resident(bias.shape), _resident(sink.shape), _resident(w_s.shape),
                  _resident(b_s_t.shape), _resident(wba.shape), _resident(wbg.shape),
                  _resident(wo.shape), _resident(g_post.shape)],
        out_specs=wide,
        out_shape=jax.ShapeDtypeStruct((batch, seq, D_MODEL), F32),
        scratch_shapes=[pltpu.VMEM((ATTN_WIDTH, ts), F32), pltpu.VMEM((ts, GMLP_WIDTH), BF16)],
        compiler_params=_params(2),
        name="prompt_mixer",
    )(q, kv, kv, packed, x1, bias, sink, w_s, b_s_t, wba, wbg, wo, g_post)


def _sample_mixer_body(n_batch, n_new, q_ref, kv_ref, ck_ref, cv_ref, u_ref, vn_ref, ga_ref, gb_ref,
                       x1_ref, bias_ref, sink_ref, ws_ref, bs_ref, wba_ref, wbg_ref, wo_ref, g_ref,
                       out_ref, attn_ref, gm_ref):
    units = [(b, h) for b in range(n_batch) for h in range(N_KV_HEADS)]
    width = Q_PER_KV * HEAD_DIM

    def keys_values(b, h, part):
        rows = slice(b * n_new, (b + 1) * n_new)
        cols = slice(part * KV_WIDTH + h * HEAD_DIM, part * KV_WIDTH + (h + 1) * HEAD_DIM)
        cache = (ck_ref, cv_ref)[part]
        return jnp.concatenate([cache[b, :, h * HEAD_DIM:(h + 1) * HEAD_DIM], kv_ref[rows, cols]],
                               axis=0).astype(BF16)

    def scores(b, h):
        qh = _stack_heads(q_ref[b * n_new:(b + 1) * n_new, :], h)
        s = lax.dot_general(qh, keys_values(b, h, 0), (((1,), (1,)), ((), ())), preferred_element_type=F32)
        return s + bias_ref[h]

    def softmax(s, h):
        sink = sink_ref[h]
        m = jnp.maximum(jnp.max(s, axis=-1, keepdims=True), sink)
        p = jnp.exp(s - m)
        denom = jnp.sum(p, axis=-1, keepdims=True) + jnp.exp(sink - m)
        return p.astype(BF16), denom

    def weighted_values(p, denom, b, h):
        o = _dot(p, keys_values(b, h, 1)) / denom
        attn_ref[b * n_new:(b + 1) * n_new, h * width:(h + 1) * width] = _unstack_heads(o, n_new).astype(BF16)

    s_vals, p_vals = {}, {}
    for i in range(len(units) + 2):
        if i < len(units):
            s_vals[i] = scores(*units[i])
        if 1 <= i <= len(units):
            p_vals[i - 1] = softmax(s_vals.pop(i - 1), units[i - 1][1])
        if i >= 2:
            weighted_values(*p_vals.pop(i - 2), *units[i - 2])

    for g in range(GMLP_GROUPS):
        cols = slice(g * GMLP_GROUP_DIM, (g + 1) * GMLP_GROUP_DIM)
        sp = _dot(ws_ref[g].astype(BF16), vn_ref[:, cols].astype(BF16)) + bs_ref[:, g:g + 1]
        gm_ref[:, cols] = (u_ref[:, cols].astype(F32) * sp).astype(BF16)

    out_ref[...] = _merge_out(x1_ref[...], attn_ref[...], gm_ref[...],
                              ga_ref[...].astype(F32), gb_ref[...].astype(F32),
                              wba_ref, wbg_ref, wo_ref, g_ref[...])


def _sample_mixer(q, kv, cache_k, cache_v, u, vn, ga, gb, x1, bias, sink, w_s, b_s_t,
                  wba, wbg, wo, g_post, *, n_batch, n_new):
    assert n_new <= CHUNK
    t = q.shape[0]
    vmem = pl.BlockSpec(memory_space=pltpu.VMEM)
    return pl.pallas_call(
        functools.partial(_sample_mixer_body, n_batch, n_new),
        in_specs=[vmem] * 17,
        out_specs=vmem,
        out_shape=jax.ShapeDtypeStruct((t, D_MODEL), F32),
        scratch_shapes=[pltpu.VMEM((t, ATTN_WIDTH), BF16), pltpu.VMEM((t, GMLP_WIDTH), BF16)],
        compiler_params=pltpu.CompilerParams(vmem_limit_bytes=VMEM_LIMIT_BYTES),
        name="sample_mixer",
    )(q, kv, cache_k, cache_v, u, vn, ga, gb, x1, bias, sink, w_s, b_s_t, wba, wbg, wo, g_post)


def _prompt_sink_rows(sinks):
    s = sinks.astype(F32).reshape(N_KV_HEADS, PAIR_GROUPS, GROUP_PAIRS, HEADS_PER_TILE)
    s = jnp.transpose(s, (0, 1, 3, 2))[..., None]
    s = jnp.broadcast_to(s, (N_KV_HEADS, PAIR_GROUPS, HEADS_PER_TILE, GROUP_PAIRS, QBLK))
    return s.reshape(N_KV_HEADS * PAIR_GROUPS * HEADS_PER_TILE, GROUP_PAIRS * QBLK)


def _sink_rows(sinks, n_q):
    s = jnp.broadcast_to(sinks.astype(F32).reshape(N_KV_HEADS, Q_PER_KV, 1), (N_KV_HEADS, Q_PER_KV, n_q))
    return s.reshape(N_KV_HEADS, Q_PER_KV * n_q, 1)


PROMPT_TM = 512
PROMPT_TS = 512


def kernel(x_prompt, x_sample, cache_win_k, cache_win_v, rel_bias_table, norm_gains, ffn1_w_gate, ffn1_w_up, ffn1_w_down, w_in, attn_sinks, gmlp_ln_g, gmlp_ln_b, gmlp_w_s, gmlp_b_s, w_branch_attn, w_branch_gmlp, w_out, ffn2_w_gate, ffn2_w_up, ffn2_w_down):
    depth = norm_gains.shape[0]
    batch, seq, _ = x_prompt.shape
    dec_batch, dec_seq, _ = x_sample.shape
    n_cache = cache_win_k.shape[2]
    assert seq % PROMPT_TS == 0 and dec_seq <= CHUNK

    bias_p = _prompt_bias(rel_bias_table)
    bias_s = _relative_bias(rel_bias_table, dec_seq, n_cache + dec_seq, n_cache)

    xp = x_prompt.reshape(batch * seq, D_MODEL)
    xs = x_sample.reshape(dec_batch * dec_seq, D_MODEL)
    t_s = xs.shape[0]
    kp, vp, ks, vs, gs = [], [], [], [], []
    for l in range(depth):
        g = norm_gains[l].astype(F32)
        win, wba, wbg, wo = _to_bf16(w_in[l], w_branch_attn[l], w_branch_gmlp[l], w_out[l])
        ln = jnp.stack([gmlp_ln_g[l], gmlp_ln_b[l]]).astype(F32)
        w_s = gmlp_w_s[l].astype(F32)
        b_s_t = jnp.transpose(gmlp_b_s[l]).astype(F32)
        g_ffn1, g_post2, g_ffn2 = g[0:3], g[3:4], g[4:6]
        sink_p = _prompt_sink_rows(attn_sinks[l])
        sink_s = _sink_rows(attn_sinks[l], dec_seq)

        x1, h2, *w1 = _ffn_stream(xs, g_ffn1, ffn1_w_gate[l], ffn1_w_up[l], ffn1_w_down[l],
                                  emit_next=True, name="ffn1_sample")
        q, kv, u, vn, ga, gb = _in_proj(h2, win, ln, tm=t_s, vn_dtype=F32, q_blocked=False,
                                        name="in_proj_sample")
        ck = cache_win_k[l].reshape(dec_batch, n_cache, KV_WIDTH)
        cv = cache_win_v[l].reshape(dec_batch, n_cache, KV_WIDTH)
        eye = jnp.eye(dec_batch, dtype=F32)
        w_s_blocks = jnp.einsum("bc,gij->gbicj", eye, w_s[:, :dec_seq, :dec_seq]).reshape(
            GMLP_GROUPS, t_s, t_s)
        b_s_rows = jnp.tile(b_s_t[:dec_seq], (dec_batch, 1))
        x2 = _sample_mixer(q, kv, ck, cv, u, vn, ga, gb, x1, bias_s, sink_s, w_s_blocks, b_s_rows,
                           wba, wbg, wo, g_post2, n_batch=dec_batch, n_new=dec_seq)
        xs, *w2 = _ffn_stream(x2, g_ffn2, ffn2_w_gate[l], ffn2_w_up[l], ffn2_w_down[l],
                              emit_next=False, name="ffn2_sample")
        ks.append(kv[:, :KV_WIDTH].reshape(dec_batch, dec_seq, N_KV_HEADS, HEAD_DIM))
        vs.append(kv[:, KV_WIDTH:].reshape(dec_batch, dec_seq, N_KV_HEADS, HEAD_DIM))
        gs.append(vn.reshape(dec_batch, dec_seq, GMLP_WIDTH))

        x1, h2 = _ffn(xp, g_ffn1, *w1, tm=PROMPT_TM, emit_next=True, name="ffn1_prompt")
        q, kv, packed = _in_proj(h2, win, ln, tm=2 * PROMPT_TM, vn_dtype=BF16, q_blocked=True,
                                 name="in_proj_prompt")
        b3 = lambda a: a.reshape(batch, seq, a.shape[-1])
        x2 = _prompt_mixer(q, b3(kv), b3(packed), b3(x1), bias_p, sink_p,
                           w_s, b_s_t, wba, wbg, wo, g_post2, ts=PROMPT_TS)
        (xp,) = _ffn(x2.reshape(batch * seq, D_MODEL), g_ffn2, *w2, tm=PROMPT_TM, emit_next=False,
                     name="ffn2_prompt")
        kv_win = b3(kv)[:, seq - WINDOW:, :]
        kp.append(kv_win[..., :KV_WIDTH].reshape(batch, WINDOW, N_KV_HEADS, HEAD_DIM))
        vp.append(kv_win[..., KV_WIDTH:].reshape(batch, WINDOW, N_KV_HEADS, HEAD_DIM))

    return (xp.reshape(batch, seq, D_MODEL), xs.reshape(dec_batch, dec_seq, D_MODEL),
            jnp.stack(kp), jnp.stack(vp), jnp.stack(ks), jnp.stack(vs), jnp.stack(gs))
```

```python
import functools
import math

import jax
import jax.numpy as jnp
import numpy as np
from jax import lax
from jax.experimental import pallas as pl
from jax.experimental.pallas import tpu as pltpu

D_MODEL = 1024
CHUNK = 64
N_Q_HEADS = 16
N_KV_HEADS = 2
HEAD_DIM = 64
Q_PER_KV = N_Q_HEADS // N_KV_HEADS
ATTN_WIDTH = N_Q_HEADS * HEAD_DIM
KV_WIDTH = N_KV_HEADS * HEAD_DIM
WINDOW = 128
GMLP_WIDTH = 1024
GMLP_GROUPS = 4
GMLP_GROUP_DIM = GMLP_WIDTH // GMLP_GROUPS
GMLP_CHUNK = 128
NUM_BUCKETS = 32
MAX_DISTANCE = 128
D_FF = 2816
EPS = 1e-6
NEG_INF = -1e30
LOG2E = math.log2(math.e)

OFF_Q = 0
OFF_KV = ATTN_WIDTH
OFF_U = OFF_KV + 2 * KV_WIDTH
OFF_GV = OFF_U + GMLP_WIDTH
OFF_GA = OFF_GV + GMLP_WIDTH
OFF_GB = OFF_GA + D_MODEL

V7X_VMEM_BYTES = 64 * 1024 * 1024
VMEM_LIMIT_BYTES = V7X_VMEM_BYTES - 8 * 1024 * 1024
MXU_TILE = 256
LANES = 128

QBLK = 2 * CHUNK
KBLK = WINDOW + QBLK
HEADS_PER_TILE = LANES // HEAD_DIM
N_HEAD_PAIRS = N_Q_HEADS // HEADS_PER_TILE
PAIRS_PER_KV = Q_PER_KV // HEADS_PER_TILE
GROUP_PAIRS = 4
PAIR_GROUPS = PAIRS_PER_KV // GROUP_PAIRS
SIDE_COLS = MXU_TILE

BF16 = jnp.bfloat16
F32 = jnp.float32


def _dot(a, b):
    return jnp.dot(a, b, preferred_element_type=F32)


def _rms(x, g):
    return x * lax.rsqrt(jnp.mean(x * x, axis=-1, keepdims=True) + EPS) * g


def _resident(shape):
    zeros = (0,) * len(shape)
    return pl.BlockSpec(shape, lambda *_: zeros, pipeline_mode=pl.Buffered(1))


def _params(n_axes):
    return pltpu.CompilerParams(dimension_semantics=("arbitrary",) * n_axes,
                                vmem_limit_bytes=VMEM_LIMIT_BYTES)


CAST_BLOCK_BYTES = 2 * 1024 * 1024


def _cast_body(*refs):
    n = len(refs) // 2
    for w_ref, o_ref in zip(refs[:n], refs[n:]):
        o_ref[...] = w_ref[...].astype(o_ref.dtype)


def _to_bf16(*ws):
    k = ws[0].shape[0]
    assert all(w.ndim == 2 and w.shape[0] == k for w in ws)
    n_max = max(w.shape[1] for w in ws)
    rows = [r for r in range(16, k + 1, 16) if k % r == 0 and r * n_max * 4 <= CAST_BLOCK_BYTES]
    bk = max(rows) if rows else k
    specs = [pl.BlockSpec((bk, w.shape[1]), lambda i: (i, 0)) for w in ws]
    return pl.pallas_call(
        _cast_body,
        grid=(k // bk,),
        in_specs=specs,
        out_specs=specs,
        out_shape=[jax.ShapeDtypeStruct(w.shape, BF16) for w in ws],
        compiler_params=_params(1),
        name="weights_to_bf16",
    )(*ws)


def _ordering_zero(*arrays):
    m = None
    for a in arrays:
        r = jnp.max(jnp.max(a.astype(F32), axis=0, keepdims=True), axis=1, keepdims=True)
        m = r if m is None else jnp.maximum(m, r)
    bits = lax.bitcast_convert_type(m, jnp.uint32)
    return lax.bitcast_convert_type((bits >> 16) >> 16, F32)


FF_CHUNK = MXU_TILE
FFN_SIDE_PIECES = 8


def _ffn_body(n_tiles, emit_next, xp_ref, xe_ref, g_ref, wg_ref, wu_ref, wd_ref, *refs):
    y_ref = refs[0]
    h_ref, acc_ref, act_ref = refs[-3:]
    s = pl.program_id(0)
    slot = s % 2
    tm = y_ref.shape[0]

    def pre_norm(dst, rows):
        h_ref[dst, rows, :] = _rms(xp_ref[rows, :], g_ref[0:1, :]).astype(BF16)
        return [h_ref[dst, rows, :]]

    def finish(rows):
        y = xe_ref[rows, :] + 0.5 * _rms(acc_ref[rows, :], g_ref[1:2, :])
        y_ref[rows, :] = y
        stored = [y_ref[rows, :]]
        if emit_next:
            refs[1][rows, :] = _rms(y, g_ref[2:3, :]).astype(BF16)
            stored.append(refs[1][rows, :])
        return stored

    @pl.when(s == 0)
    def _():
        pre_norm(0, slice(None))
        acc_ref[...] = jnp.zeros_like(acc_ref)

    @pl.when(jnp.logical_and(s >= 1, s <= n_tiles))
    def _():
        h = h_ref[1 - slot]
        anchor = None
        piece_rows = tm // FFN_SIDE_PIECES
        assert D_FF // FF_CHUNK > FFN_SIDE_PIECES
        for ci in range(D_FF // FF_CHUNK):
            cols = slice(ci * FF_CHUNK, (ci + 1) * FF_CHUNK)
            gate = _dot(h, wg_ref[:, cols])
            up = _dot(h, wu_ref[:, cols])
            if anchor is not None:
                up = up + anchor
            act_ref[:, cols] = (jax.nn.silu(gate) * up).astype(BF16)
            anchor = None
            if ci < FFN_SIDE_PIECES:
                rows = slice(ci * piece_rows, (ci + 1) * piece_rows)
                anchor = _ordering_zero(*(finish(rows) + pre_norm(slot, rows)))
        acc_ref[...] = _dot(act_ref[...], wd_ref[...])

    @pl.when(s == n_tiles + 1)
    def _():
        finish(slice(None))


def _ffn(x, gains, wg, wu, wd, *, tm, emit_next, name):
    t = x.shape[0]
    assert t % tm == 0
    n_tiles = t // tm
    head = lambda s: (jnp.minimum(s, n_tiles - 1), 0)
    tail = lambda s: (jnp.clip(s - 2, 0, n_tiles - 1), 0)
    out_shape = [jax.ShapeDtypeStruct((t, D_MODEL), F32)]
    out_specs = [pl.BlockSpec((tm, D_MODEL), tail)]
    if emit_next:
        out_shape.append(jax.ShapeDtypeStruct((t, D_MODEL), BF16))
        out_specs.append(pl.BlockSpec((tm, D_MODEL), tail))
    return pl.pallas_call(
        functools.partial(_ffn_body, n_tiles, emit_next),
        grid=(n_tiles + 2,),
        in_specs=[pl.BlockSpec((tm, D_MODEL), head), pl.BlockSpec((tm, D_MODEL), tail),
                  _resident(gains.shape), _resident(wg.shape), _resident(wu.shape), _resident(wd.shape)],
        out_specs=out_specs,
        out_shape=out_shape,
        scratch_shapes=[pltpu.VMEM((2, tm, D_MODEL), BF16), pltpu.VMEM((tm, D_MODEL), F32),
                        pltpu.VMEM((tm, D_FF), BF16)],
        compiler_params=_params(1),
        name=name,
    )(x, x, gains, wg, wu, wd)


def _ffn_stream_body(emit_next, x_ref, g_ref, wg_ref, wu_ref, wd_ref, *refs):
    n_out = 2 if emit_next else 1
    y_ref = refs[0]
    wg16_ref, wu16_ref, wd16_ref = refs[n_out:n_out + 3]
    h_ref, acc_ref = refs[-2:]
    c = pl.program_id(0)

    @pl.when(c == 0)
    def _():
        h_ref[...] = _rms(x_ref[...], g_ref[0:1, :]).astype(BF16)
        acc_ref[...] = jnp.zeros_like(acc_ref)

    wg, wu, wd = (r[...].astype(BF16) for r in (wg_ref, wu_ref, wd_ref))
    wg16_ref[...], wu16_ref[...], wd16_ref[...] = wg, wu, wd
    h = h_ref[...]
    act = (jax.nn.silu(_dot(h, wg)) * _dot(h, wu)).astype(BF16)
    acc_ref[...] += _dot(act, wd)

    @pl.when(c == pl.num_programs(0) - 1)
    def _():
        y = x_ref[...] + 0.5 * _rms(acc_ref[...], g_ref[1:2, :])
        y_ref[...] = y
        if emit_next:
            refs[1][...] = _rms(y, g_ref[2:3, :]).astype(BF16)


def _ffn_stream(x, gains, wg, wu, wd, *, emit_next, name):
    t = x.shape[0]
    whole = lambda shape: pl.BlockSpec(shape, lambda c: (0, 0))
    col = pl.BlockSpec((D_MODEL, FF_CHUNK), lambda c: (0, c))
    row = pl.BlockSpec((FF_CHUNK, D_MODEL), lambda c: (c, 0))
    out_shape = [jax.ShapeDtypeStruct((t, D_MODEL), F32)]
    if emit_next:
        out_shape.append(jax.ShapeDtypeStruct((t, D_MODEL), BF16))
    out_specs = [whole((t, D_MODEL))] * len(out_shape) + [col, col, row]
    out_shape += [jax.ShapeDtypeStruct(w.shape, BF16) for w in (wg, wu, wd)]
    return pl.pallas_call(
        functools.partial(_ffn_stream_body, emit_next),
        grid=(D_FF // FF_CHUNK,),
        in_specs=[whole((t, D_MODEL)), whole(gains.shape), col, col, row],
        out_specs=out_specs,
        out_shape=out_shape,
        scratch_shapes=[pltpu.VMEM((t, D_MODEL), BF16), pltpu.VMEM((t, D_MODEL), F32)],
        compiler_params=_params(1),
        name=name,
    )(x, gains, wg, wu, wd)


def _gelu(x):
    return 0.5 * x * (1.0 + lax.erf(x * np.sqrt(0.5).astype(np.float32)))


def _layer_norm(x, g, b):
    mu = jnp.mean(x, axis=-1, keepdims=True)
    xc = x - mu
    var = jnp.mean(xc * xc, axis=-1, keepdims=True)
    return xc * lax.rsqrt(var + EPS) * g + b


IN_PROJ_LN_PIECES = 8


def _in_proj_body(q_blocked, h_ref, w_ref, ln_ref, q_ref, kv_ref, *refs):
    if q_blocked:
        packed_ref, gv_ref = refs
        u_ref, vn_ref, ga_ref, gb_ref = (packed_ref.at[:, k * D_MODEL:(k + 1) * D_MODEL] for k in range(4))
    else:
        u_ref, vn_ref, ga_ref, gb_ref, gv_ref = refs
    _in_proj_compute(q_blocked, h_ref, w_ref, ln_ref, q_ref, kv_ref, u_ref, vn_ref, ga_ref, gb_ref, gv_ref)


def _in_proj_compute(q_blocked, h_ref, w_ref, ln_ref, q_ref, kv_ref, u_ref, vn_ref, ga_ref, gb_ref, gv_ref):
    h = h_ref[...]
    tm = h.shape[0]
    n_chunks = D_MODEL // MXU_TILE
    q_scale = HEAD_DIM ** -0.5 * (LOG2E if q_blocked else 1.0)

    def store_q(c, r):
        q = (r * q_scale).astype(q_ref.dtype)
        if q_blocked:
            for b in range(tm // QBLK):
                for j in range(MXU_TILE // LANES):
                    p = c * (MXU_TILE // LANES) + j
                    q_ref[b, p * QBLK:(p + 1) * QBLK, :] = q[b * QBLK:(b + 1) * QBLK, j * LANES:(j + 1) * LANES]
        else:
            q_ref[:, c * MXU_TILE:(c + 1) * MXU_TILE] = q

    def store_cols(ref, fn):
        def store(c, r):
            ref[:, c * MXU_TILE:(c + 1) * MXU_TILE] = fn(r).astype(ref.dtype)
        return store

    def layer_norm_piece(k):
        rows = slice(k * (tm // IN_PROJ_LN_PIECES), (k + 1) * (tm // IN_PROJ_LN_PIECES))
        vn_ref[rows, :] = _layer_norm(_gelu(gv_ref[rows, :]), ln_ref[0:1, :], ln_ref[1:2, :]).astype(vn_ref.dtype)

    def run(off, c, store):
        store(c, _dot(h, w_ref[:, off + c * MXU_TILE: off + (c + 1) * MXU_TILE]))

    store_gv = lambda c, r: gv_ref.__setitem__((slice(None), slice(c * MXU_TILE, (c + 1) * MXU_TILE)), r)
    store_u, store_ga, store_gb = (store_cols(u_ref, _gelu), store_cols(ga_ref, jax.nn.sigmoid),
                                   store_cols(gb_ref, jax.nn.sigmoid))
    for c in range(n_chunks):
        run(OFF_GV, c, store_gv)
    run(OFF_KV, 0, lambda c, r: kv_ref.__setitem__(Ellipsis, r))
    for c in range(n_chunks):
        run(OFF_Q, c, store_q)
        layer_norm_piece(2 * c)
        run(OFF_GA, c, store_ga)
        layer_norm_piece(2 * c + 1)
    for c in range(n_chunks):
        run(OFF_U, c, store_u)
        run(OFF_GB, c, store_gb)


def _in_proj(h, w_in, ln, *, tm, vn_dtype, q_blocked, name):
    t = h.shape[0]
    assert t % tm == 0
    row = lambda i: (i, 0)
    wide = lambda dt: jax.ShapeDtypeStruct((t, D_MODEL), dt)
    wide_spec = pl.BlockSpec((tm, D_MODEL), row)
    if q_blocked:
        assert tm % QBLK == 0 and vn_dtype == BF16
        q_shape = jax.ShapeDtypeStruct((t // QBLK, N_HEAD_PAIRS * QBLK, LANES), BF16)
        q_spec = pl.BlockSpec((tm // QBLK, N_HEAD_PAIRS * QBLK, LANES), lambda i: (i, 0, 0))
        act_shapes = [jax.ShapeDtypeStruct((t, 4 * D_MODEL), BF16)]
        act_specs = [pl.BlockSpec((tm, 4 * D_MODEL), row)]
    else:
        q_shape, q_spec = wide(BF16), wide_spec
        act_shapes = [wide(BF16), wide(vn_dtype), wide(BF16), wide(BF16)]
        act_specs = [wide_spec] * 4
    return pl.pallas_call(
        functools.partial(_in_proj_body, q_blocked),
        grid=(t // tm,),
        in_specs=[wide_spec, _resident(w_in.shape), _resident(ln.shape)],
        out_specs=[q_spec, pl.BlockSpec((tm, 2 * KV_WIDTH), row)] + act_specs,
        out_shape=[q_shape, jax.ShapeDtypeStruct((t, 2 * KV_WIDTH), F32)] + act_shapes,
        scratch_shapes=[pltpu.VMEM((tm, GMLP_WIDTH), F32)],
        compiler_params=_params(1),
        name=name,
    )(h, w_in, ln)


def _t5_bucket(rel):
    half = NUM_BUCKETS // 2
    max_exact = half // 2
    ret = np.where(rel > 0, half, 0)
    n = np.abs(rel)
    nf = np.maximum(n, 1).astype(np.float32)
    scaled = (np.log(nf / np.float32(max_exact)) / np.float32(math.log(MAX_DISTANCE / max_exact))
              * np.float32(half - max_exact))
    exact = np.log(np.maximum(n, 1) / max_exact) / math.log(MAX_DISTANCE / max_exact) * (half - max_exact)
    assert np.array_equal(scaled.astype(np.int32), exact.astype(np.int32))
    large = np.minimum(max_exact + scaled.astype(np.int32), half - 1)
    return (ret + np.where(n < max_exact, n, large)).astype(np.int32)


def _bias_body(table_ref, bucket_ref, out_ref):
    bucket = bucket_ref[...]
    hits = [bucket == b for b in range(NUM_BUCKETS)]
    for head in range(N_Q_HEADS):
        acc = jnp.zeros(bucket.shape, F32)
        for b in range(NUM_BUCKETS):
            acc = jnp.where(hits[b], table_ref[b, head], acc)
        out_ref[head] = acc


def _prompt_bias_body(table_ref, bucket_ref, out_ref):
    bucket = bucket_ref[...]
    hits = [bucket == b for b in range(NUM_BUCKETS)]
    key_chunk = lax.broadcasted_iota(jnp.int32, bucket.shape, 0) // CHUNK
    q_chunk = lax.broadcasted_iota(jnp.int32, bucket.shape, 1) // CHUNK
    in_window = jnp.logical_and(key_chunk >= q_chunk, key_chunk <= q_chunk + WINDOW // CHUNK)
    valid = [in_window, jnp.logical_and(in_window, key_chunk >= WINDOW // CHUNK)]
    for head in range(N_Q_HEADS):
        acc = jnp.zeros(bucket.shape, F32)
        for b in range(NUM_BUCKETS):
            acc = jnp.where(hits[b], table_ref[b, head], acc)
        h, rest = divmod(head, Q_PER_KV)
        pair, par = divmod(rest, HEADS_PER_TILE)
        pg, pl_ = divmod(pair, GROUP_PAIRS)
        for v in range(2):
            out_ref[v, h, pg, par * KBLK:(par + 1) * KBLK, pl_ * QBLK:(pl_ + 1) * QBLK] = (
                jnp.where(valid[v], acc * LOG2E, NEG_INF))


def _prompt_bias(table):
    rel = (np.arange(KBLK) - WINDOW)[:, None] - np.arange(QBLK)[None, :]
    bucket = jnp.asarray(_t5_bucket(rel))
    return pl.pallas_call(
        _prompt_bias_body,
        in_specs=[pl.BlockSpec(memory_space=pltpu.SMEM),
                  pl.BlockSpec(memory_space=pltpu.VMEM)],
        out_specs=pl.BlockSpec(memory_space=pltpu.VMEM),
        out_shape=jax.ShapeDtypeStruct((2, N_KV_HEADS, PAIR_GROUPS, 2 * KBLK, GROUP_PAIRS * QBLK), F32),
        name="rel_bias_prompt",
    )(table, bucket)


def _relative_bias(table, n_q, n_keys, n_past):
    rel = (np.arange(n_keys) - n_past)[None, :] - np.arange(n_q)[:, None]
    bucket = jnp.asarray(_t5_bucket(rel))
    bias = pl.pallas_call(
        _bias_body,
        in_specs=[pl.BlockSpec(memory_space=pltpu.SMEM),
                  pl.BlockSpec(memory_space=pltpu.VMEM)],
        out_specs=pl.BlockSpec(memory_space=pltpu.VMEM),
        out_shape=jax.ShapeDtypeStruct((N_Q_HEADS, n_q, n_keys), F32),
        name="rel_bias_%d" % n_q,
    )(table, bucket)
    return bias.reshape(N_KV_HEADS, Q_PER_KV * n_q, n_keys)


def _sink_attention(qh, kh, vh, bias, sink, invalid=None):
    s = lax.dot_general(qh, kh, (((1,), (1,)), ((), ())), preferred_element_type=F32) + bias
    if invalid is not None:
        s = jnp.where(invalid, NEG_INF, s)
    m = jnp.maximum(jnp.max(s, axis=-1, keepdims=True), sink)
    p = jnp.exp(s - m)
    denom = jnp.sum(p, axis=-1, keepdims=True) + jnp.exp(sink - m)
    return _dot(p.astype(BF16), vh) / denom


def _stack_heads(q, kv_head):
    base = kv_head * Q_PER_KV * HEAD_DIM
    return jnp.concatenate(
        [q[:, base + g * HEAD_DIM: base + (g + 1) * HEAD_DIM] for g in range(Q_PER_KV)], axis=0)


def _unstack_heads(o, n):
    return jnp.concatenate([o[g * n:(g + 1) * n, :] for g in range(Q_PER_KV)], axis=1)


def _merge_out(x1, attn, gm, ga, gb, wba_ref, wbg_ref, wo_ref, g_post):
    merged = ga * _dot(attn, wba_ref[...]) + gb * _dot(gm, wbg_ref[...])
    return x1 + _rms(_dot(merged.astype(BF16), wo_ref[...]), g_post)


def _prompt_mixer_body(ts, q_ref, kv_ref, kvp_ref, packed_ref, x1_ref,
                       bias_ref, sink_ref, ws_ref, bs_ref, wba_ref, wbg_ref, wo_ref, g_ref,
                       out_ref, attn_t_ref, gm_ref):
    u_ref, vn_ref, ga_ref, gb_ref = (packed_ref.at[0, :, k * D_MODEL:(k + 1) * D_MODEL] for k in range(4))
    first_variant = jnp.where(pl.program_id(1) == 0, 1, 0)
    kv_all = jnp.concatenate([kvp_ref[0], kv_ref[0]], axis=0)
    k_all = kv_all[:, :KV_WIDTH]
    v_t = jnp.transpose(kv_all[:, KV_WIDTH:]).astype(BF16)
    low = lax.broadcasted_iota(jnp.int32, k_all.shape, 1) < HEAD_DIM
    k_swapped = pltpu.roll(k_all, HEAD_DIM, axis=1)
    zero = jnp.zeros_like(k_all)
    k_par = [[jnp.where(low, k_all, zero), jnp.where(low, zero, k_swapped)],
             [jnp.where(low, k_swapped, zero), jnp.where(low, zero, k_all)]]
    k_par = [[k.astype(BF16) for k in ks] for ks in k_par]

    units = [(blk, h, pg) for blk in range(ts // QBLK) for h in range(N_KV_HEADS)
             for pg in range(PAIR_GROUPS)]

    def scores(blk, h, pg):
        keys = slice(blk * QBLK, blk * QBLK + KBLK)
        k_blk = jnp.concatenate([k_par[h][0][keys], k_par[h][1][keys]], axis=0)
        pair0 = h * PAIRS_PER_KV + pg * GROUP_PAIRS
        qa = q_ref[blk, pair0 * QBLK:(pair0 + GROUP_PAIRS) * QBLK, :]
        s = lax.dot_general(k_blk, qa, (((1,), (1,)), ((), ())), preferred_element_type=F32)
        variant = first_variant if blk == 0 else 0
        return s + bias_ref[variant, h, pg]

    def softmax(s, blk, h, pg):
        out = []
        for par in range(HEADS_PER_TILE):
            sp = s[par * KBLK:(par + 1) * KBLK]
            row = (h * PAIR_GROUPS + pg) * HEADS_PER_TILE + par
            sink = sink_ref[row:row + 1, :] * LOG2E
            m = jnp.maximum(jnp.max(sp, axis=0, keepdims=True), sink)
            p = jnp.exp2(sp - m)
            denom = jnp.sum(p, axis=0, keepdims=True) + jnp.exp2(sink - m)
            out.append((p.astype(BF16), 1.0 / denom))
        return out

    def weighted_values(probs, blk, h, pg):
        keys = slice(blk * QBLK, blk * QBLK + KBLK)
        cols = slice(blk * QBLK, (blk + 1) * QBLK)
        vh_t = v_t[h * HEAD_DIM:(h + 1) * HEAD_DIM, keys]
        pair0 = h * PAIRS_PER_KV + pg * GROUP_PAIRS
        for par, (p, inv) in enumerate(probs):
            o = _dot(vh_t, p) * inv
            for pl_ in range(GROUP_PAIRS):
                head = (pair0 + pl_) * HEADS_PER_TILE + par
                attn_t_ref[head * HEAD_DIM:(head + 1) * HEAD_DIM, cols] = o[:, pl_ * QBLK:(pl_ + 1) * QBLK]

    blk_i = lax.broadcasted_iota(jnp.int32, (GMLP_CHUNK, GMLP_CHUNK), 0) // CHUNK
    blk_j = lax.broadcasted_iota(jnp.int32, (GMLP_CHUNK, GMLP_CHUNK), 1) // CHUNK
    for g in range(GMLP_GROUPS):
        w = jnp.where(blk_j <= blk_i, ws_ref[g], 0.0).astype(BF16)
        b = bs_ref[:, g:g + 1]
        cols = slice(g * GMLP_GROUP_DIM, (g + 1) * GMLP_GROUP_DIM)
        for c in range(ts // GMLP_CHUNK):
            rows = slice(c * GMLP_CHUNK, (c + 1) * GMLP_CHUNK)
            sp = _dot(w, vn_ref[rows, cols]) + b
            gm_ref[rows, cols] = (u_ref[rows, cols].astype(F32) * sp).astype(BF16)

    def gmlp_branch(c):
        cols = slice(c * SIDE_COLS, (c + 1) * SIDE_COLS)
        return gb_ref[:, cols].astype(F32) * _dot(gm_ref[...], wbg_ref[:, cols])

    n_side = D_MODEL // SIDE_COLS
    side = []
    s_vals, p_vals = {}, {}
    for i in range(len(units) + 2):
        if i < len(units):
            s_vals[i] = scores(*units[i])
        if 1 <= i <= len(units):
            p_vals[i - 1] = softmax(s_vals.pop(i - 1), *units[i - 1])
        if i >= 2:
            weighted_values(p_vals.pop(i - 2), *units[i - 2])
        if i % (len(units) // n_side) == 1 and len(side) < n_side:
            side.append(gmlp_branch(len(side)))
    assert len(side) == n_side

    attn = jnp.transpose(attn_t_ref[...]).astype(BF16)
    merged = ga_ref[...].astype(F32) * _dot(attn, wba_ref[...]) + jnp.concatenate(side, axis=1)
    out_ref[0] = x1_ref[0] + _rms(_dot(merged.astype(BF16), wo_ref[...]), g_ref[...])


def _prompt_mixer(q, kv, packed, x1, bias, sink, w_s, b_s_t, wba, wbg, wo, g_post, *, ts):
    batch, seq, _ = kv.shape
    assert seq % ts == 0 and ts % QBLK == 0 and QBLK == GMLP_CHUNK == WINDOW
    tile = lambda b, t: (b, t, 0)
    prev = lambda b, t: (b, jnp.maximum(t * (ts // WINDOW) - 1, 0), 0)
    wide = pl.BlockSpec((1, ts, D_MODEL), tile)
    n_t = seq // ts
    q_spec = pl.BlockSpec((ts // QBLK, N_HEAD_PAIRS * QBLK, LANES), lambda b, t: (b * n_t + t, 0, 0))
    return pl.pallas_call(
        functools.partial(_prompt_mixer_body, ts),
        grid=(batch, n_t),
        in_specs=[q_spec, pl.BlockSpec((1, ts, 2 * KV_WIDTH), tile),
                  pl.BlockSpec((1, WINDOW, 2 * KV_WIDTH), prev),
                  pl.BlockSpec((1, ts, 4 * D_MODEL), tile), wide,
                  _resident(bias.shape), _resident(sink.shape), _resident(w_s.shape),
                  _resident(b_s_t.shape), _resident(wba.shape), _resident(wbg.shape),
                  _resident(wo.shape), _resident(g_post.shape)],
        out_specs=wide,
        out_shape=jax.ShapeDtypeStruct((batch, seq, D_MODEL), F32),
        scratch_shapes=[pltpu.VMEM((ATTN_WIDTH, ts), F32), pltpu.VMEM((ts, GMLP_WIDTH), BF16)],
        compiler_params=_params(2),
        name="prompt_mixer",
    )(q, kv, kv, packed, x1, bias, sink, w_s, b_s_t, wba, wbg, wo, g_post)


def _sample_mixer_body(n_batch, n_new, q_ref, kv_ref, ck_ref, cv_ref, u_ref, vn_ref, ga_ref, gb_ref,
                       x1_ref, bias_ref, sink_ref, ws_ref, bs_ref, wba_ref, wbg_ref, wo_ref, g_ref,
                       out_ref, attn_ref, gm_ref):
    for b in range(n_batch):
        rows = slice(b * n_new, (b + 1) * n_new)
        q = q_ref[rows, :]
        kv = kv_ref[rows, :]
        k_all = jnp.concatenate([ck_ref[b], kv[:, :KV_WIDTH]], axis=0).astype(BF16)
        v_all = jnp.concatenate([cv_ref[b], kv[:, KV_WIDTH:]], axis=0).astype(BF16)
        for h in range(N_KV_HEADS):
            cols = slice(h * HEAD_DIM, (h + 1) * HEAD_DIM)
            o = _sink_attention(_stack_heads(q, h), k_all[:, cols], v_all[:, cols],
                                bias_ref[h], sink_ref[h])
            width = Q_PER_KV * HEAD_DIM
            attn_ref[rows, h * width:(h + 1) * width] = _unstack_heads(o, n_new).astype(BF16)
        for g in range(GMLP_GROUPS):
            cols = slice(g * GMLP_GROUP_DIM, (g + 1) * GMLP_GROUP_DIM)
            w = ws_ref[g, :n_new, :n_new].astype(BF16)
            sp = _dot(w, vn_ref[rows, cols].astype(BF16)) + bs_ref[:n_new, g:g + 1]
            gm_ref[rows, cols] = (u_ref[rows, cols].astype(F32) * sp).astype(BF16)

    out_ref[...] = _merge_out(x1_ref[...], attn_ref[...], gm_ref[...],
                              ga_ref[...].astype(F32), gb_ref[...].astype(F32),
                              wba_ref, wbg_ref, wo_ref, g_ref[...])


def _sample_mixer(q, kv, cache_k, cache_v, u, vn, ga, gb, x1, bias, sink, w_s, b_s_t,
                  wba, wbg, wo, g_post, *, n_batch, n_new):
    assert n_new <= CHUNK
    t = q.shape[0]
    vmem = pl.BlockSpec(memory_space=pltpu.VMEM)
    return pl.pallas_call(
        functools.partial(_sample_mixer_body, n_batch, n_new),
        in_specs=[vmem] * 17,
        out_specs=vmem,
        out_shape=jax.ShapeDtypeStruct((t, D_MODEL), F32),
        scratch_shapes=[pltpu.VMEM((t, ATTN_WIDTH), BF16), pltpu.VMEM((t, GMLP_WIDTH), BF16)],
        compiler_params=pltpu.CompilerParams(vmem_limit_bytes=VMEM_LIMIT_BYTES),
        name="sample_mixer",
    )(q, kv, cache_k, cache_v, u, vn, ga, gb, x1, bias, sink, w_s, b_s_t, wba, wbg, wo, g_post)


def _prompt_sink_rows(sinks):
    s = sinks.astype(F32).reshape(N_KV_HEADS, PAIR_GROUPS, GROUP_PAIRS, HEADS_PER_TILE)
    s = jnp.transpose(s, (0, 1, 3, 2))[..., None]
    s = jnp.broadcast_to(s, (N_KV_HEADS, PAIR_GROUPS, HEADS_PER_TILE, GROUP_PAIRS, QBLK))
    return s.reshape(N_KV_HEADS * PAIR_GROUPS * HEADS_PER_TILE, GROUP_PAIRS * QBLK)


def _sink_rows(sinks, n_q):
    s = jnp.broadcast_to(sinks.astype(F32).reshape(N_KV_HEADS, Q_PER_KV, 1), (N_KV_HEADS, Q_PER_KV, n_q))
    return s.reshape(N_KV_HEADS, Q_PER_KV * n_q, 1)


PROMPT_TM = 512
PROMPT_TS = 512


def kernel(x_prompt, x_sample, cache_win_k, cache_win_v, rel_bias_table, norm_gains, ffn1_w_gate, ffn1_w_up, ffn1_w_down, w_in, attn_sinks, gmlp_ln_g, gmlp_ln_b, gmlp_w_s, gmlp_b_s, w_branch_attn, w_branch_gmlp, w_out, ffn2_w_gate, ffn2_w_up, ffn2_w_down):
    depth = norm_gains.shape[0]
    batch, seq, _ = x_prompt.shape
    dec_batch, dec_seq, _ = x_sample.shape
    n_cache = cache_win_k.shape[2]
    assert seq % PROMPT_TS == 0 and dec_seq <= CHUNK

    bias_p = _prompt_bias(rel_bias_table)
    bias_s = _relative_bias(rel_bias_table, dec_seq, n_cache + dec_seq, n_cache)

    xp = x_prompt.reshape(batch * seq, D_MODEL)
    xs = x_sample.reshape(dec_batch * dec_seq, D_MODEL)
    t_s = xs.shape[0]
    kp, vp, ks, vs, gs = [], [], [], [], []
    for l in range(depth):
        g = norm_gains[l].astype(F32)
        win, wba, wbg, wo = _to_bf16(w_in[l], w_branch_attn[l], w_branch_gmlp[l], w_out[l])
        ln = jnp.stack([gmlp_ln_g[l], gmlp_ln_b[l]]).astype(F32)
        w_s = gmlp_w_s[l].astype(F32)
        b_s_t = jnp.transpose(gmlp_b_s[l]).astype(F32)
        g_ffn1, g_post2, g_ffn2 = g[0:3], g[3:4], g[4:6]
        sink_p = _prompt_sink_rows(attn_sinks[l])
        sink_s = _sink_rows(attn_sinks[l], dec_seq)

        x1, h2, *w1 = _ffn_stream(xs, g_ffn1, ffn1_w_gate[l], ffn1_w_up[l], ffn1_w_down[l],
                                  emit_next=True, name="ffn1_sample")
        q, kv, u, vn, ga, gb = _in_proj(h2, win, ln, tm=t_s, vn_dtype=F32, q_blocked=False,
                                        name="in_proj_sample")
        ck = cache_win_k[l].reshape(dec_batch, n_cache, KV_WIDTH)
        cv = cache_win_v[l].reshape(dec_batch, n_cache, KV_WIDTH)
        x2 = _sample_mixer(q, kv, ck, cv, u, vn, ga, gb, x1, bias_s, sink_s, w_s, b_s_t,
                           wba, wbg, wo, g_post2, n_batch=dec_batch, n_new=dec_seq)
        xs, *w2 = _ffn_stream(x2, g_ffn2, ffn2_w_gate[l], ffn2_w_up[l], ffn2_w_down[l],
                              emit_next=False, name="ffn2_sample")
        ks.append(kv[:, :KV_WIDTH].reshape(dec_batch, dec_seq, N_KV_HEADS, HEAD_DIM))
        vs.append(kv[:, KV_WIDTH:].reshape(dec_batch, dec_seq, N_KV_HEADS, HEAD_DIM))
        gs.append(vn.reshape(dec_batch, dec_seq, GMLP_WIDTH))

        x1, h2 = _ffn(xp, g_ffn1, *w1, tm=PROMPT_TM, emit_next=True, name="ffn1_prompt")
        q, kv, packed = _in_proj(h2, win, ln, tm=2 * PROMPT_TM, vn_dtype=BF16, q_blocked=True,
                                 name="in_proj_prompt")
        b3 = lambda a: a.reshape(batch, seq, a.shape[-1])
        x2 = _prompt_mixer(q, b3(kv), b3(packed), b3(x1), bias_p, sink_p,
                           w_s, b_s_t, wba, wbg, wo, g_post2, ts=PROMPT_TS)
        (xp,) = _ffn(x2.reshape(batch * seq, D_MODEL), g_ffn2, *w2, tm=PROMPT_TM, emit_next=False,
                     name="ffn2_prompt")
        kv_win = b3(kv)[:, seq - WINDOW:, :]
        kp.append(kv_win[..., :KV_WIDTH].reshape(batch, WINDOW, N_KV_HEADS, HEAD_DIM))
        vp.append(kv_win[..., KV_WIDTH:].reshape(batch, WINDOW, N_KV_HEADS, HEAD_DIM))

    return (xp.reshape(batch, seq, D_MODEL), xs.reshape(dec_batch, dec_seq, D_MODEL),
            jnp.stack(kp), jnp.stack(vp), jnp.stack(ks), jnp.stack(vs), jnp.stack(gs))
```

```python
import functools
import math

import jax
import jax.numpy as jnp
import numpy as np
from jax import lax
from jax.experimental import pallas as pl
from jax.experimental.pallas import tpu as pltpu

D_MODEL = 1024
CHUNK = 64
N_Q_HEADS = 16
N_KV_HEADS = 2
HEAD_DIM = 64
Q_PER_KV = N_Q_HEADS // N_KV_HEADS
ATTN_WIDTH = N_Q_HEADS * HEAD_DIM
KV_WIDTH = N_KV_HEADS * HEAD_DIM
WINDOW = 128
GMLP_WIDTH = 1024
GMLP_GROUPS = 4
GMLP_GROUP_DIM = GMLP_WIDTH // GMLP_GROUPS
GMLP_CHUNK = 128
NUM_BUCKETS = 32
MAX_DISTANCE = 128
D_FF = 2816
EPS = 1e-6
NEG_INF = -1e30
LOG2E = math.log2(math.e)

OFF_Q = 0
OFF_KV = ATTN_WIDTH
OFF_U = OFF_KV + 2 * KV_WIDTH
OFF_GV = OFF_U + GMLP_WIDTH
OFF_GA = OFF_GV + GMLP_WIDTH
OFF_GB = OFF_GA + D_MODEL

V7X_VMEM_BYTES = 64 * 1024 * 1024
VMEM_LIMIT_BYTES = V7X_VMEM_BYTES - 8 * 1024 * 1024
MXU_TILE = 256
LANES = 128

QBLK = 2 * CHUNK
KBLK = WINDOW + QBLK
HEADS_PER_TILE = LANES // HEAD_DIM
N_HEAD_PAIRS = N_Q_HEADS // HEADS_PER_TILE
PAIRS_PER_KV = Q_PER_KV // HEADS_PER_TILE
GROUP_PAIRS = 4
PAIR_GROUPS = PAIRS_PER_KV // GROUP_PAIRS
SIDE_COLS = MXU_TILE

BF16 = jnp.bfloat16
F32 = jnp.float32


def _dot(a, b):
    return jnp.dot(a, b, preferred_element_type=F32)


def _rms(x, g):
    return x * lax.rsqrt(jnp.mean(x * x, axis=-1, keepdims=True) + EPS) * g


def _resident(shape):
    zeros = (0,) * len(shape)
    return pl.BlockSpec(shape, lambda *_: zeros, pipeline_mode=pl.Buffered(1))


def _params(n_axes):
    return pltpu.CompilerParams(dimension_semantics=("arbitrary",) * n_axes,
                                vmem_limit_bytes=VMEM_LIMIT_BYTES)


CAST_BLOCK_BYTES = 2 * 1024 * 1024


def _cast_body(*refs):
    n = len(refs) // 2
    for w_ref, o_ref in zip(refs[:n], refs[n:]):
        o_ref[...] = w_ref[...].astype(o_ref.dtype)


def _to_bf16(*ws):
    k = ws[0].shape[0]
    assert all(w.ndim == 2 and w.shape[0] == k for w in ws)
    n_max = max(w.shape[1] for w in ws)
    rows = [r for r in range(16, k + 1, 16) if k % r == 0 and r * n_max * 4 <= CAST_BLOCK_BYTES]
    bk = max(rows) if rows else k
    specs = [pl.BlockSpec((bk, w.shape[1]), lambda i: (i, 0)) for w in ws]
    return pl.pallas_call(
        _cast_body,
        grid=(k // bk,),
        in_specs=specs,
        out_specs=specs,
        out_shape=[jax.ShapeDtypeStruct(w.shape, BF16) for w in ws],
        compiler_params=_params(1),
        name="weights_to_bf16",
    )(*ws)


def _ordering_zero(*arrays):
    m = None
    for a in arrays:
        r = jnp.max(jnp.max(a.astype(F32), axis=0, keepdims=True), axis=1, keepdims=True)
        m = r if m is None else jnp.maximum(m, r)
    bits = lax.bitcast_convert_type(m, jnp.uint32)
    return lax.bitcast_convert_type((bits >> 16) >> 16, F32)


FF_CHUNK = MXU_TILE
FFN_SIDE_PIECES = 8


def _ffn_body(n_tiles, emit_next, xp_ref, xe_ref, g_ref, wg_ref, wu_ref, wd_ref, *refs):
    y_ref = refs[0]
    h_ref, acc_ref, act_ref = refs[-3:]
    s = pl.program_id(0)
    slot = s % 2
    tm = y_ref.shape[0]

    def pre_norm(dst, rows):
        h_ref[dst, rows, :] = _rms(xp_ref[rows, :], g_ref[0:1, :]).astype(BF16)
        return [h_ref[dst, rows, :]]

    def finish(rows):
        y = xe_ref[rows, :] + 0.5 * _rms(acc_ref[rows, :], g_ref[1:2, :])
        y_ref[rows, :] = y
        stored = [y_ref[rows, :]]
        if emit_next:
            refs[1][rows, :] = _rms(y, g_ref[2:3, :]).astype(BF16)
            stored.append(refs[1][rows, :])
        return stored

    @pl.when(s == 0)
    def _():
        pre_norm(0, slice(None))
        acc_ref[...] = jnp.zeros_like(acc_ref)

    @pl.when(jnp.logical_and(s >= 1, s <= n_tiles))
    def _():
        h = h_ref[1 - slot]
        anchor = None
        piece_rows = tm // FFN_SIDE_PIECES
        assert D_FF // FF_CHUNK > FFN_SIDE_PIECES
        for ci in range(D_FF // FF_CHUNK):
            cols = slice(ci * FF_CHUNK, (ci + 1) * FF_CHUNK)
            gate = _dot(h, wg_ref[:, cols])
            up = _dot(h, wu_ref[:, cols])
            if anchor is not None:
                up = up + anchor
            act_ref[:, cols] = (jax.nn.silu(gate) * up).astype(BF16)
            anchor = None
            if ci < FFN_SIDE_PIECES:
                rows = slice(ci * piece_rows, (ci + 1) * piece_rows)
                anchor = _ordering_zero(*(finish(rows) + pre_norm(slot, rows)))
        acc_ref[...] = _dot(act_ref[...], wd_ref[...])

    @pl.when(s == n_tiles + 1)
    def _():
        finish(slice(None))


def _ffn(x, gains, wg, wu, wd, *, tm, emit_next, name):
    t = x.shape[0]
    assert t % tm == 0
    n_tiles = t // tm
    head = lambda s: (jnp.minimum(s, n_tiles - 1), 0)
    tail = lambda s: (jnp.clip(s - 2, 0, n_tiles - 1), 0)
    out_shape = [jax.ShapeDtypeStruct((t, D_MODEL), F32)]
    out_specs = [pl.BlockSpec((tm, D_MODEL), tail)]
    if emit_next:
        out_shape.append(jax.ShapeDtypeStruct((t, D_MODEL), BF16))
        out_specs.append(pl.BlockSpec((tm, D_MODEL), tail))
    return pl.pallas_call(
        functools.partial(_ffn_body, n_tiles, emit_next),
        grid=(n_tiles + 2,),
        in_specs=[pl.BlockSpec((tm, D_MODEL), head), pl.BlockSpec((tm, D_MODEL), tail),
                  _resident(gains.shape), _resident(wg.shape), _resident(wu.shape), _resident(wd.shape)],
        out_specs=out_specs,
        out_shape=out_shape,
        scratch_shapes=[pltpu.VMEM((2, tm, D_MODEL), BF16), pltpu.VMEM((tm, D_MODEL), F32),
                        pltpu.VMEM((tm, D_FF), BF16)],
        compiler_params=_params(1),
        name=name,
    )(x, x, gains, wg, wu, wd)


def _ffn_stream_body(emit_next, x_ref, g_ref, wg_ref, wu_ref, wd_ref, *refs):
    n_out = 2 if emit_next else 1
    y_ref = refs[0]
    wg16_ref, wu16_ref, wd16_ref = refs[n_out:n_out + 3]
    h_ref, acc_ref = refs[-2:]
    c = pl.program_id(0)

    @pl.when(c == 0)
    def _():
        h_ref[...] = _rms(x_ref[...], g_ref[0:1, :]).astype(BF16)
        acc_ref[...] = jnp.zeros_like(acc_ref)

    wg, wu, wd = (r[...].astype(BF16) for r in (wg_ref, wu_ref, wd_ref))
    wg16_ref[...], wu16_ref[...], wd16_ref[...] = wg, wu, wd
    h = h_ref[...]
    act = (jax.nn.silu(_dot(h, wg)) * _dot(h, wu)).astype(BF16)
    acc_ref[...] += _dot(act, wd)

    @pl.when(c == pl.num_programs(0) - 1)
    def _():
        y = x_ref[...] + 0.5 * _rms(acc_ref[...], g_ref[1:2, :])
        y_ref[...] = y
        if emit_next:
            refs[1][...] = _rms(y, g_ref[2:3, :]).astype(BF16)


def _ffn_stream(x, gains, wg, wu, wd, *, emit_next, name):
    t = x.shape[0]
    whole = lambda shape: pl.BlockSpec(shape, lambda c: (0, 0))
    col = pl.BlockSpec((D_MODEL, FF_CHUNK), lambda c: (0, c))
    row = pl.BlockSpec((FF_CHUNK, D_MODEL), lambda c: (c, 0))
    out_shape = [jax.ShapeDtypeStruct((t, D_MODEL), F32)]
    if emit_next:
        out_shape.append(jax.ShapeDtypeStruct((t, D_MODEL), BF16))
    out_specs = [whole((t, D_MODEL))] * len(out_shape) + [col, col, row]
    out_shape += [jax.ShapeDtypeStruct(w.shape, BF16) for w in (wg, wu, wd)]
    return pl.pallas_call(
        functools.partial(_ffn_stream_body, emit_next),
        grid=(D_FF // FF_CHUNK,),
        in_specs=[whole((t, D_MODEL)), whole(gains.shape), col, col, row],
        out_specs=out_specs,
        out_shape=out_shape,
        scratch_shapes=[pltpu.VMEM((t, D_MODEL), BF16), pltpu.VMEM((t, D_MODEL), F32)],
        compiler_params=_params(1),
        name=name,
    )(x, gains, wg, wu, wd)


def _gelu(x):
    return 0.5 * x * (1.0 + lax.erf(x * np.sqrt(0.5).astype(np.float32)))


def _layer_norm(x, g, b):
    mu = jnp.mean(x, axis=-1, keepdims=True)
    xc = x - mu
    var = jnp.mean(xc * xc, axis=-1, keepdims=True)
    return xc * lax.rsqrt(var + EPS) * g + b


IN_PROJ_LN_PIECES = 8


def _in_proj_body(q_blocked, h_ref, w_ref, ln_ref, q_ref, kv_ref, *refs):
    if q_blocked:
        packed_ref, gv_ref = refs
        u_ref, vn_ref, ga_ref, gb_ref = (packed_ref.at[:, k * D_MODEL:(k + 1) * D_MODEL] for k in range(4))
        w16_ref = None
    else:
        u_ref, vn_ref, ga_ref, gb_ref, w16_ref, gv_ref = refs
    _in_proj_compute(q_blocked, h_ref, w_ref, w16_ref, ln_ref, q_ref, kv_ref, u_ref, vn_ref, ga_ref, gb_ref,
                     gv_ref)


def _in_proj_compute(q_blocked, h_ref, w_ref, w16_ref, ln_ref, q_ref, kv_ref, u_ref, vn_ref, ga_ref, gb_ref,
                     gv_ref):
    h = h_ref[...]
    tm = h.shape[0]
    n_chunks = D_MODEL // MXU_TILE
    q_scale = HEAD_DIM ** -0.5 * (LOG2E if q_blocked else 1.0)

    def store_q(c, r):
        q = (r * q_scale).astype(q_ref.dtype)
        if q_blocked:
            for b in range(tm // QBLK):
                for j in range(MXU_TILE // LANES):
                    p = c * (MXU_TILE // LANES) + j
                    q_ref[b, p * QBLK:(p + 1) * QBLK, :] = q[b * QBLK:(b + 1) * QBLK, j * LANES:(j + 1) * LANES]
        else:
            q_ref[:, c * MXU_TILE:(c + 1) * MXU_TILE] = q

    def store_cols(ref, fn):
        def store(c, r):
            ref[:, c * MXU_TILE:(c + 1) * MXU_TILE] = fn(r).astype(ref.dtype)
        return store

    def layer_norm_piece(k):
        rows = slice(k * (tm // IN_PROJ_LN_PIECES), (k + 1) * (tm // IN_PROJ_LN_PIECES))
        vn_ref[rows, :] = _layer_norm(_gelu(gv_ref[rows, :]), ln_ref[0:1, :], ln_ref[1:2, :]).astype(vn_ref.dtype)

    def run(off, c, store):
        cols = slice(off + c * MXU_TILE, off + (c + 1) * MXU_TILE)
        w = w_ref[:, cols]
        if w16_ref is not None:
            w = w.astype(BF16)
            w16_ref[:, cols] = w
        store(c, _dot(h, w))

    store_gv = lambda c, r: gv_ref.__setitem__((slice(None), slice(c * MXU_TILE, (c + 1) * MXU_TILE)), r)
    store_u, store_ga, store_gb = (store_cols(u_ref, _gelu), store_cols(ga_ref, jax.nn.sigmoid),
                                   store_cols(gb_ref, jax.nn.sigmoid))
    for c in range(n_chunks):
        run(OFF_GV, c, store_gv)
    run(OFF_KV, 0, lambda c, r: kv_ref.__setitem__(Ellipsis, r))
    for c in range(n_chunks):
        run(OFF_Q, c, store_q)
        layer_norm_piece(2 * c)
        run(OFF_GA, c, store_ga)
        layer_norm_piece(2 * c + 1)
    for c in range(n_chunks):
        run(OFF_U, c, store_u)
        run(OFF_GB, c, store_gb)


def _in_proj(h, w_in, ln, *, tm, vn_dtype, q_blocked, name):
    t = h.shape[0]
    assert t % tm == 0
    row = lambda i: (i, 0)
    wide = lambda dt: jax.ShapeDtypeStruct((t, D_MODEL), dt)
    wide_spec = pl.BlockSpec((tm, D_MODEL), row)
    if q_blocked:
        assert tm % QBLK == 0 and vn_dtype == BF16
        q_shape = jax.ShapeDtypeStruct((t // QBLK, N_HEAD_PAIRS * QBLK, LANES), BF16)
        q_spec = pl.BlockSpec((tm // QBLK, N_HEAD_PAIRS * QBLK, LANES), lambda i: (i, 0, 0))
        act_shapes = [jax.ShapeDtypeStruct((t, 4 * D_MODEL), BF16)]
        act_specs = [pl.BlockSpec((tm, 4 * D_MODEL), row)]
    else:
        assert t == tm and w_in.dtype == F32
        q_shape, q_spec = wide(BF16), wide_spec
        act_shapes = [wide(BF16), wide(vn_dtype), wide(BF16), wide(BF16),
                      jax.ShapeDtypeStruct(w_in.shape, BF16)]
        act_specs = [wide_spec] * 4 + [pl.BlockSpec(w_in.shape, lambda i: (0, 0))]
    return pl.pallas_call(
        functools.partial(_in_proj_body, q_blocked),
        grid=(t // tm,),
        in_specs=[wide_spec, _resident(w_in.shape), _resident(ln.shape)],
        out_specs=[q_spec, pl.BlockSpec((tm, 2 * KV_WIDTH), row)] + act_specs,
        out_shape=[q_shape, jax.ShapeDtypeStruct((t, 2 * KV_WIDTH), F32)] + act_shapes,
        scratch_shapes=[pltpu.VMEM((tm, GMLP_WIDTH), F32)],
        compiler_params=_params(1),
        name=name,
    )(h, w_in, ln)


def _t5_bucket(rel):
    half = NUM_BUCKETS // 2
    max_exact = half // 2
    ret = np.where(rel > 0, half, 0)
    n = np.abs(rel)
    nf = np.maximum(n, 1).astype(np.float32)
    scaled = (np.log(nf / np.float32(max_exact)) / np.float32(math.log(MAX_DISTANCE / max_exact))
              * np.float32(half - max_exact))
    exact = np.log(np.maximum(n, 1) / max_exact) / math.log(MAX_DISTANCE / max_exact) * (half - max_exact)
    assert np.array_equal(scaled.astype(np.int32), exact.astype(np.int32))
    large = np.minimum(max_exact + scaled.astype(np.int32), half - 1)
    return (ret + np.where(n < max_exact, n, large)).astype(np.int32)


def _bias_body(table_ref, bucket_ref, out_ref):
    bucket = bucket_ref[...]
    hits = [bucket == b for b in range(NUM_BUCKETS)]
    for head in range(N_Q_HEADS):
        acc = jnp.zeros(bucket.shape, F32)
        for b in range(NUM_BUCKETS):
            acc = jnp.where(hits[b], table_ref[b, head], acc)
        out_ref[head] = acc


def _prompt_bias_body(table_ref, bucket_ref, out_ref):
    bucket = bucket_ref[...]
    hits = [bucket == b for b in range(NUM_BUCKETS)]
    key_chunk = lax.broadcasted_iota(jnp.int32, bucket.shape, 0) // CHUNK
    q_chunk = lax.broadcasted_iota(jnp.int32, bucket.shape, 1) // CHUNK
    in_window = jnp.logical_and(key_chunk >= q_chunk, key_chunk <= q_chunk + WINDOW // CHUNK)
    valid = [in_window, jnp.logical_and(in_window, key_chunk >= WINDOW // CHUNK)]
    for head in range(N_Q_HEADS):
        acc = jnp.zeros(bucket.shape, F32)
        for b in range(NUM_BUCKETS):
            acc = jnp.where(hits[b], table_ref[b, head], acc)
        h, rest = divmod(head, Q_PER_KV)
        pair, par = divmod(rest, HEADS_PER_TILE)
        pg, pl_ = divmod(pair, GROUP_PAIRS)
        for v in range(2):
            out_ref[v, h, pg, par * KBLK:(par + 1) * KBLK, pl_ * QBLK:(pl_ + 1) * QBLK] = (
                jnp.where(valid[v], acc * LOG2E, NEG_INF))


def _prompt_bias(table):
    rel = (np.arange(KBLK) - WINDOW)[:, None] - np.arange(QBLK)[None, :]
    bucket = jnp.asarray(_t5_bucket(rel))
    return pl.pallas_call(
        _prompt_bias_body,
        in_specs=[pl.BlockSpec(memory_space=pltpu.SMEM),
                  pl.BlockSpec(memory_space=pltpu.VMEM)],
        out_specs=pl.BlockSpec(memory_space=pltpu.VMEM),
        out_shape=jax.ShapeDtypeStruct((2, N_KV_HEADS, PAIR_GROUPS, 2 * KBLK, GROUP_PAIRS * QBLK), F32),
        name="rel_bias_prompt",
    )(table, bucket)


def _relative_bias(table, n_q, n_keys, n_past):
    rel = (np.arange(n_keys) - n_past)[None, :] - np.arange(n_q)[:, None]
    bucket = jnp.asarray(_t5_bucket(rel))
    bias = pl.pallas_call(
        _bias_body,
        in_specs=[pl.BlockSpec(memory_space=pltpu.SMEM),
                  pl.BlockSpec(memory_space=pltpu.VMEM)],
        out_specs=pl.BlockSpec(memory_space=pltpu.VMEM),
        out_shape=jax.ShapeDtypeStruct((N_Q_HEADS, n_q, n_keys), F32),
        name="rel_bias_%d" % n_q,
    )(table, bucket)
    return bias.reshape(N_KV_HEADS, Q_PER_KV * n_q, n_keys)


def _sink_attention(qh, kh, vh, bias, sink, invalid=None):
    s = lax.dot_general(qh, kh, (((1,), (1,)), ((), ())), preferred_element_type=F32) + bias
    if invalid is not None:
        s = jnp.where(invalid, NEG_INF, s)
    m = jnp.maximum(jnp.max(s, axis=-1, keepdims=True), sink)
    p = jnp.exp(s - m)
    denom = jnp.sum(p, axis=-1, keepdims=True) + jnp.exp(sink - m)
    return _dot(p.astype(BF16), vh) / denom


def _stack_heads(q, kv_head):
    base = kv_head * Q_PER_KV * HEAD_DIM
    return jnp.concatenate(
        [q[:, base + g * HEAD_DIM: base + (g + 1) * HEAD_DIM] for g in range(Q_PER_KV)], axis=0)


def _unstack_heads(o, n):
    return jnp.concatenate([o[g * n:(g + 1) * n, :] for g in range(Q_PER_KV)], axis=1)


def _merge_out(x1, attn, gm, ga, gb, wba_ref, wbg_ref, wo_ref, g_post):
    merged = ga * _dot(attn, wba_ref[...]) + gb * _dot(gm, wbg_ref[...])
    return x1 + _rms(_dot(merged.astype(BF16), wo_ref[...]), g_post)


def _prompt_mixer_body(ts, q_ref, kv_ref, kvp_ref, packed_ref, x1_ref,
                       bias_ref, sink_ref, ws_ref, bs_ref, wba_ref, wbg_ref, wo_ref, g_ref,
                       out_ref, attn_t_ref, gm_ref):
    u_ref, vn_ref, ga_ref, gb_ref = (packed_ref.at[0, :, k * D_MODEL:(k + 1) * D_MODEL] for k in range(4))
    first_variant = jnp.where(pl.program_id(1) == 0, 1, 0)
    kv_all = jnp.concatenate([kvp_ref[0], kv_ref[0]], axis=0)
    k_all = kv_all[:, :KV_WIDTH]
    v_t = jnp.transpose(kv_all[:, KV_WIDTH:]).astype(BF16)
    low = lax.broadcasted_iota(jnp.int32, k_all.shape, 1) < HEAD_DIM
    k_swapped = pltpu.roll(k_all, HEAD_DIM, axis=1)
    zero = jnp.zeros_like(k_all)
    k_par = [[jnp.where(low, k_all, zero), jnp.where(low, zero, k_swapped)],
             [jnp.where(low, k_swapped, zero), jnp.where(low, zero, k_all)]]
    k_par = [[k.astype(BF16) for k in ks] for ks in k_par]

    units = [(blk, h, pg) for blk in range(ts // QBLK) for h in range(N_KV_HEADS)
             for pg in range(PAIR_GROUPS)]

    def scores(blk, h, pg):
        keys = slice(blk * QBLK, blk * QBLK + KBLK)
        k_blk = jnp.concatenate([k_par[h][0][keys], k_par[h][1][keys]], axis=0)
        pair0 = h * PAIRS_PER_KV + pg * GROUP_PAIRS
        qa = q_ref[blk, pair0 * QBLK:(pair0 + GROUP_PAIRS) * QBLK, :]
        s = lax.dot_general(k_blk, qa, (((1,), (1,)), ((), ())), preferred_element_type=F32)
        variant = first_variant if blk == 0 else 0
        return s + bias_ref[variant, h, pg]

    def softmax(s, blk, h, pg):
        out = []
        for par in range(HEADS_PER_TILE):
            sp = s[par * KBLK:(par + 1) * KBLK]
            row = (h * PAIR_GROUPS + pg) * HEADS_PER_TILE + par
            sink = sink_ref[row:row + 1, :] * LOG2E
            m = jnp.maximum(jnp.max(sp, axis=0, keepdims=True), sink)
            p = jnp.exp2(sp - m)
            denom = jnp.sum(p, axis=0, keepdims=True) + jnp.exp2(sink - m)
            out.append((p.astype(BF16), 1.0 / denom))
        return out

    def weighted_values(probs, blk, h, pg):
        keys = slice(blk * QBLK, blk * QBLK + KBLK)
        cols = slice(blk * QBLK, (blk + 1) * QBLK)
        vh_t = v_t[h * HEAD_DIM:(h + 1) * HEAD_DIM, keys]
        pair0 = h * PAIRS_PER_KV + pg * GROUP_PAIRS
        for par, (p, inv) in enumerate(probs):
            o = _dot(vh_t, p) * inv
            for pl_ in range(GROUP_PAIRS):
                head = (pair0 + pl_) * HEADS_PER_TILE + par
                attn_t_ref[head * HEAD_DIM:(head + 1) * HEAD_DIM, cols] = o[:, pl_ * QBLK:(pl_ + 1) * QBLK]

    blk_i = lax.broadcasted_iota(jnp.int32, (GMLP_CHUNK, GMLP_CHUNK), 0) // CHUNK
    blk_j = lax.broadcasted_iota(jnp.int32, (GMLP_CHUNK, GMLP_CHUNK), 1) // CHUNK
    for g in range(GMLP_GROUPS):
        w = jnp.where(blk_j <= blk_i, ws_ref[g], 0.0).astype(BF16)
        b = bs_ref[:, g:g + 1]
        cols = slice(g * GMLP_GROUP_DIM, (g + 1) * GMLP_GROUP_DIM)
        for c in range(ts // GMLP_CHUNK):
            rows = slice(c * GMLP_CHUNK, (c + 1) * GMLP_CHUNK)
            sp = _dot(w, vn_ref[rows, cols]) + b
            gm_ref[rows, cols] = (u_ref[rows, cols].astype(F32) * sp).astype(BF16)

    def gmlp_branch(c):
        cols = slice(c * SIDE_COLS, (c + 1) * SIDE_COLS)
        return gb_ref[:, cols].astype(F32) * _dot(gm_ref[...], wbg_ref[:, cols])

    n_side = D_MODEL // SIDE_COLS
    side = []
    s_vals, p_vals = {}, {}
    for i in range(len(units) + 2):
        if i < len(units):
            s_vals[i] = scores(*units[i])
        if 1 <= i <= len(units):
            p_vals[i - 1] = softmax(s_vals.pop(i - 1), *units[i - 1])
        if i >= 2:
            weighted_values(p_vals.pop(i - 2), *units[i - 2])
        if i % (len(units) // n_side) == 1 and len(side) < n_side:
            side.append(gmlp_branch(len(side)))
    assert len(side) == n_side

    attn = jnp.transpose(attn_t_ref[...]).astype(BF16)
    merged = ga_ref[...].astype(F32) * _dot(attn, wba_ref[...]) + jnp.concatenate(side, axis=1)
    out_ref[0] = x1_ref[0] + _rms(_dot(merged.astype(BF16), wo_ref[...]), g_ref[...])


def _prompt_mixer(q, kv, packed, x1, bias, sink, w_s, b_s_t, wba, wbg, wo, g_post, *, ts):
    batch, seq, _ = kv.shape
    assert seq % ts == 0 and ts % QBLK == 0 and QBLK == GMLP_CHUNK == WINDOW
    tile = lambda b, t: (b, t, 0)
    prev = lambda b, t: (b, jnp.maximum(t * (ts // WINDOW) - 1, 0), 0)
    wide = pl.BlockSpec((1, ts, D_MODEL), tile)
    n_t = seq // ts
    q_spec = pl.BlockSpec((ts // QBLK, N_HEAD_PAIRS * QBLK, LANES), lambda b, t: (b * n_t + t, 0, 0))
    return pl.pallas_call(
        functools.partial(_prompt_mixer_body, ts),
        grid=(batch, n_t),
        in_specs=[q_spec, pl.BlockSpec((1, ts, 2 * KV_WIDTH), tile),
                  pl.BlockSpec((1, WINDOW, 2 * KV_WIDTH), prev),
                  pl.BlockSpec((1, ts, 4 * D_MODEL), tile), wide,
                  _resident(bias.shape), _resident(sink.shape), _resident(w_s.shape),
                  _resident(b_s_t.shape), _resident(wba.shape), _resident(wbg.shape),
                  _resident(wo.shape), _resident(g_post.shape)],
        out_specs=wide,
        out_shape=jax.ShapeDtypeStruct((batch, seq, D_MODEL), F32),
        scratch_shapes=[pltpu.VMEM((ATTN_WIDTH, ts), F32), pltpu.VMEM((ts, GMLP_WIDTH), BF16)],
        compiler_params=_params(2),
        name="prompt_mixer",
    )(q, kv, kv, packed, x1, bias, sink, w_s, b_s_t, wba, wbg, wo, g_post)


def _sample_mixer_body(n_batch, n_new, q_ref, kv_ref, ck_ref, cv_ref, u_ref, vn_ref, ga_ref, gb_ref,
                       x1_ref, bias_ref, sink_ref, ws_ref, bs_ref, wba_ref, wbg_ref, wo_ref, g_ref,
                       out_ref, attn_ref, gm_ref):
    for b in range(n_batch):
        rows = slice(b * n_new, (b + 1) * n_new)
        q = q_ref[rows, :]
        kv = kv_ref[rows, :]
        k_all = jnp.concatenate([ck_ref[b], kv[:, :KV_WIDTH]], axis=0).astype(BF16)
        v_all = jnp.concatenate([cv_ref[b], kv[:, KV_WIDTH:]], axis=0).astype(BF16)
        for h in range(N_KV_HEADS):
            cols = slice(h * HEAD_DIM, (h + 1) * HEAD_DIM)
            o = _sink_attention(_stack_heads(q, h), k_all[:, cols], v_all[:, cols],
                                bias_ref[h], sink_ref[h])
            width = Q_PER_KV * HEAD_DIM
            attn_ref[rows, h * width:(h + 1) * width] = _unstack_heads(o, n_new).astype(BF16)
        for g in range(GMLP_GROUPS):
            cols = slice(g * GMLP_GROUP_DIM, (g + 1) * GMLP_GROUP_DIM)
            w = ws_ref[g, :n_new, :n_new].astype(BF16)
            sp = _dot(w, vn_ref[rows, cols].astype(BF16)) + bs_ref[:n_new, g:g + 1]
            gm_ref[rows, cols] = (u_ref[rows, cols].astype(F32) * sp).astype(BF16)

    out_ref[...] = _merge_out(x1_ref[...], attn_ref[...], gm_ref[...],
                              ga_ref[...].astype(F32), gb_ref[...].astype(F32),
                              wba_ref, wbg_ref, wo_ref, g_ref[...])


def _sample_mixer(q, kv, cache_k, cache_v, u, vn, ga, gb, x1, bias, sink, w_s, b_s_t,
                  wba, wbg, wo, g_post, *, n_batch, n_new):
    assert n_new <= CHUNK
    t = q.shape[0]
    vmem = pl.BlockSpec(memory_space=pltpu.VMEM)
    return pl.pallas_call(
        functools.partial(_sample_mixer_body, n_batch, n_new),
        in_specs=[vmem] * 17,
        out_specs=vmem,
        out_shape=jax.ShapeDtypeStruct((t, D_MODEL), F32),
        scratch_shapes=[pltpu.VMEM((t, ATTN_WIDTH), BF16), pltpu.VMEM((t, GMLP_WIDTH), BF16)],
        compiler_params=pltpu.CompilerParams(vmem_limit_bytes=VMEM_LIMIT_BYTES),
        name="sample_mixer",
    )(q, kv, cache_k, cache_v, u, vn, ga, gb, x1, bias, sink, w_s, b_s_t, wba, wbg, wo, g_post)


def _prompt_sink_rows(sinks):
    s = sinks.astype(F32).reshape(N_KV_HEADS, PAIR_GROUPS, GROUP_PAIRS, HEADS_PER_TILE)
    s = jnp.transpose(s, (0, 1, 3, 2))[..., None]
    s = jnp.broadcast_to(s, (N_KV_HEADS, PAIR_GROUPS, HEADS_PER_TILE, GROUP_PAIRS, QBLK))
    return s.reshape(N_KV_HEADS * PAIR_GROUPS * HEADS_PER_TILE, GROUP_PAIRS * QBLK)


def _sink_rows(sinks, n_q):
    s = jnp.broadcast_to(sinks.astype(F32).reshape(N_KV_HEADS, Q_PER_KV, 1), (N_KV_HEADS, Q_PER_KV, n_q))
    return s.reshape(N_KV_HEADS, Q_PER_KV * n_q, 1)


PROMPT_TM = 512
PROMPT_TS = 512


def kernel(x_prompt, x_sample, cache_win_k, cache_win_v, rel_bias_table, norm_gains, ffn1_w_gate, ffn1_w_up, ffn1_w_down, w_in, attn_sinks, gmlp_ln_g, gmlp_ln_b, gmlp_w_s, gmlp_b_s, w_branch_attn, w_branch_gmlp, w_out, ffn2_w_gate, ffn2_w_up, ffn2_w_down):
    depth = norm_gains.shape[0]
    batch, seq, _ = x_prompt.shape
    dec_batch, dec_seq, _ = x_sample.shape
    n_cache = cache_win_k.shape[2]
    assert seq % PROMPT_TS == 0 and dec_seq <= CHUNK

    bias_p = _prompt_bias(rel_bias_table)
    bias_s = _relative_bias(rel_bias_table, dec_seq, n_cache + dec_seq, n_cache)

    xp = x_prompt.reshape(batch * seq, D_MODEL)
    xs = x_sample.reshape(dec_batch * dec_seq, D_MODEL)
    t_s = xs.shape[0]
    kp, vp, ks, vs, gs = [], [], [], [], []
    for l in range(depth):
        g = norm_gains[l].astype(F32)
        wba, wbg, wo = _to_bf16(w_branch_attn[l], w_branch_gmlp[l], w_out[l])
        ln = jnp.stack([gmlp_ln_g[l], gmlp_ln_b[l]]).astype(F32)
        w_s = gmlp_w_s[l].astype(F32)
        b_s_t = jnp.transpose(gmlp_b_s[l]).astype(F32)
        g_ffn1, g_post2, g_ffn2 = g[0:3], g[3:4], g[4:6]
        sink_p = _prompt_sink_rows(attn_sinks[l])
        sink_s = _sink_rows(attn_sinks[l], dec_seq)

        x1, h2, *w1 = _ffn_stream(xs, g_ffn1, ffn1_w_gate[l], ffn1_w_up[l], ffn1_w_down[l],
                                  emit_next=True, name="ffn1_sample")
        q, kv, u, vn, ga, gb, win = _in_proj(h2, w_in[l], ln, tm=t_s, vn_dtype=F32, q_blocked=False,
                                             name="in_proj_sample")
        ck = cache_win_k[l].reshape(dec_batch, n_cache, KV_WIDTH)
        cv = cache_win_v[l].reshape(dec_batch, n_cache, KV_WIDTH)
        x2 = _sample_mixer(q, kv, ck, cv, u, vn, ga, gb, x1, bias_s, sink_s, w_s, b_s_t,
                           wba, wbg, wo, g_post2, n_batch=dec_batch, n_new=dec_seq)
        xs, *w2 = _ffn_stream(x2, g_ffn2, ffn2_w_gate[l], ffn2_w_up[l], ffn2_w_down[l],
                              emit_next=False, name="ffn2_sample")
        ks.append(kv[:, :KV_WIDTH].reshape(dec_batch, dec_seq, N_KV_HEADS, HEAD_DIM))
        vs.append(kv[:, KV_WIDTH:].reshape(dec_batch, dec_seq, N_KV_HEADS, HEAD_DIM))
        gs.append(vn.reshape(dec_batch, dec_seq, GMLP_WIDTH))

        x1, h2 = _ffn(xp, g_ffn1, *w1, tm=PROMPT_TM, emit_next=True, name="ffn1_prompt")
        q, kv, packed = _in_proj(h2, win, ln, tm=2 * PROMPT_TM, vn_dtype=BF16, q_blocked=True,
                                 name="in_proj_prompt")
        b3 = lambda a: a.reshape(batch, seq, a.shape[-1])
        x2 = _prompt_mixer(q, b3(kv), b3(packed), b3(x1), bias_p, sink_p,
                           w_s, b_s_t, wba, wbg, wo, g_post2, ts=PROMPT_TS)
        (xp,) = _ffn(x2.reshape(batch * seq, D_MODEL), g_ffn2, *w2, tm=PROMPT_TM, emit_next=False,
                     name="ffn2_prompt")
        kv_win = b3(kv)[:, seq - WINDOW:, :]
        kp.append(kv_win[..., :KV_WIDTH].reshape(batch, WINDOW, N_KV_HEADS, HEAD_DIM))
        vp.append(kv_win[..., KV_WIDTH:].reshape(batch, WINDOW, N_KV_HEADS, HEAD_DIM))

    return (xp.reshape(batch, seq, D_MODEL), xs.reshape(dec_batch, dec_seq, D_MODEL),
            jnp.stack(kp), jnp.stack(vp), jnp.stack(ks), jnp.stack(vs), jnp.stack(gs))
```

```python
import functools
import math

import jax
import jax.numpy as jnp
import numpy as np
from jax import lax
from jax.experimental import pallas as pl
from jax.experimental.pallas import tpu as pltpu

D_MODEL = 1024
CHUNK = 64
N_Q_HEADS = 16
N_KV_HEADS = 2
HEAD_DIM = 64
Q_PER_KV = N_Q_HEADS // N_KV_HEADS
ATTN_WIDTH = N_Q_HEADS * HEAD_DIM
KV_WIDTH = N_KV_HEADS * HEAD_DIM
WINDOW = 128
GMLP_WIDTH = 1024
GMLP_GROUPS = 4
GMLP_GROUP_DIM = GMLP_WIDTH // GMLP_GROUPS
GMLP_CHUNK = 128
NUM_BUCKETS = 32
MAX_DISTANCE = 128
D_FF = 2816
EPS = 1e-6
NEG_INF = -1e30
LOG2E = math.log2(math.e)

OFF_Q = 0
OFF_KV = ATTN_WIDTH
OFF_U = OFF_KV + 2 * KV_WIDTH
OFF_GV = OFF_U + GMLP_WIDTH
OFF_GA = OFF_GV + GMLP_WIDTH
OFF_GB = OFF_GA + D_MODEL

V7X_VMEM_BYTES = 64 * 1024 * 1024
VMEM_LIMIT_BYTES = V7X_VMEM_BYTES - 8 * 1024 * 1024
MXU_TILE = 256
LANES = 128

QBLK = 2 * CHUNK
KBLK = WINDOW + QBLK
HEADS_PER_TILE = LANES // HEAD_DIM
N_HEAD_PAIRS = N_Q_HEADS // HEADS_PER_TILE
PAIRS_PER_KV = Q_PER_KV // HEADS_PER_TILE
GROUP_PAIRS = 4
PAIR_GROUPS = PAIRS_PER_KV // GROUP_PAIRS
SIDE_COLS = MXU_TILE

BF16 = jnp.bfloat16
F32 = jnp.float32


def _dot(a, b):
    return jnp.dot(a, b, preferred_element_type=F32)


def _rms(x, g):
    return x * lax.rsqrt(jnp.mean(x * x, axis=-1, keepdims=True) + EPS) * g


def _resident(shape):
    zeros = (0,) * len(shape)
    return pl.BlockSpec(shape, lambda *_: zeros, pipeline_mode=pl.Buffered(1))


def _params(n_axes):
    return pltpu.CompilerParams(dimension_semantics=("arbitrary",) * n_axes,
                                vmem_limit_bytes=VMEM_LIMIT_BYTES)


def _ordering_zero(*arrays):
    m = None
    for a in arrays:
        r = jnp.max(jnp.max(a.astype(F32), axis=0, keepdims=True), axis=1, keepdims=True)
        m = r if m is None else jnp.maximum(m, r)
    bits = lax.bitcast_convert_type(m, jnp.uint32)
    return lax.bitcast_convert_type((bits >> 16) >> 16, F32)


FF_CHUNK = MXU_TILE
FFN_SIDE_PIECES = 8


def _ffn_body(n_tiles, emit_next, xp_ref, xe_ref, g_ref, wg_ref, wu_ref, wd_ref, *refs):
    y_ref = refs[0]
    h_ref, acc_ref, act_ref = refs[-3:]
    s = pl.program_id(0)
    slot = s % 2
    tm = y_ref.shape[0]

    def pre_norm(dst, rows):
        h_ref[dst, rows, :] = _rms(xp_ref[rows, :], g_ref[0:1, :]).astype(BF16)
        return [h_ref[dst, rows, :]]

    def finish(rows):
        y = xe_ref[rows, :] + 0.5 * _rms(acc_ref[rows, :], g_ref[1:2, :])
        y_ref[rows, :] = y
        stored = [y_ref[rows, :]]
        if emit_next:
            refs[1][rows, :] = _rms(y, g_ref[2:3, :]).astype(BF16)
            stored.append(refs[1][rows, :])
        return stored

    @pl.when(s == 0)
    def _():
        pre_norm(0, slice(None))
        acc_ref[...] = jnp.zeros_like(acc_ref)

    @pl.when(jnp.logical_and(s >= 1, s <= n_tiles))
    def _():
        h = h_ref[1 - slot]
        anchor = None
        piece_rows = tm // FFN_SIDE_PIECES
        assert D_FF // FF_CHUNK > FFN_SIDE_PIECES
        for ci in range(D_FF // FF_CHUNK):
            cols = slice(ci * FF_CHUNK, (ci + 1) * FF_CHUNK)
            gate = _dot(h, wg_ref[:, cols])
            up = _dot(h, wu_ref[:, cols])
            if anchor is not None:
                up = up + anchor
            act_ref[:, cols] = (jax.nn.silu(gate) * up).astype(BF16)
            anchor = None
            if ci < FFN_SIDE_PIECES:
                rows = slice(ci * piece_rows, (ci + 1) * piece_rows)
                anchor = _ordering_zero(*(finish(rows) + pre_norm(slot, rows)))
        acc_ref[...] = _dot(act_ref[...], wd_ref[...])

    @pl.when(s == n_tiles + 1)
    def _():
        finish(slice(None))


def _ffn(x, gains, wg, wu, wd, *, tm, emit_next, name):
    t = x.shape[0]
    assert t % tm == 0
    n_tiles = t // tm
    head = lambda s: (jnp.minimum(s, n_tiles - 1), 0)
    tail = lambda s: (jnp.clip(s - 2, 0, n_tiles - 1), 0)
    out_shape = [jax.ShapeDtypeStruct((t, D_MODEL), F32)]
    out_specs = [pl.BlockSpec((tm, D_MODEL), tail)]
    if emit_next:
        out_shape.append(jax.ShapeDtypeStruct((t, D_MODEL), BF16))
        out_specs.append(pl.BlockSpec((tm, D_MODEL), tail))
    return pl.pallas_call(
        functools.partial(_ffn_body, n_tiles, emit_next),
        grid=(n_tiles + 2,),
        in_specs=[pl.BlockSpec((tm, D_MODEL), head), pl.BlockSpec((tm, D_MODEL), tail),
                  _resident(gains.shape), _resident(wg.shape), _resident(wu.shape), _resident(wd.shape)],
        out_specs=out_specs,
        out_shape=out_shape,
        scratch_shapes=[pltpu.VMEM((2, tm, D_MODEL), BF16), pltpu.VMEM((tm, D_MODEL), F32),
                        pltpu.VMEM((tm, D_FF), BF16)],
        compiler_params=_params(1),
        name=name,
    )(x, x, gains, wg, wu, wd)


def _ffn_stream_body(emit_next, x_ref, g_ref, wg_ref, wu_ref, wd_ref, *refs):
    n_out = 2 if emit_next else 1
    y_ref = refs[0]
    wg16_ref, wu16_ref, wd16_ref = refs[n_out:n_out + 3]
    h_ref, acc_ref = refs[-2:]
    c = pl.program_id(0)

    @pl.when(c == 0)
    def _():
        h_ref[...] = _rms(x_ref[...], g_ref[0:1, :]).astype(BF16)
        acc_ref[...] = jnp.zeros_like(acc_ref)

    wg, wu, wd = (r[...].astype(BF16) for r in (wg_ref, wu_ref, wd_ref))
    wg16_ref[...], wu16_ref[...], wd16_ref[...] = wg, wu, wd
    h = h_ref[...]
    act = (jax.nn.silu(_dot(h, wg)) * _dot(h, wu)).astype(BF16)
    acc_ref[...] += _dot(act, wd)

    @pl.when(c == pl.num_programs(0) - 1)
    def _():
        y = x_ref[...] + 0.5 * _rms(acc_ref[...], g_ref[1:2, :])
        y_ref[...] = y
        if emit_next:
            refs[1][...] = _rms(y, g_ref[2:3, :]).astype(BF16)


def _ffn_stream(x, gains, wg, wu, wd, *, emit_next, name):
    t = x.shape[0]
    whole = lambda shape: pl.BlockSpec(shape, lambda c: (0, 0))
    col = pl.BlockSpec((D_MODEL, FF_CHUNK), lambda c: (0, c))
    row = pl.BlockSpec((FF_CHUNK, D_MODEL), lambda c: (c, 0))
    out_shape = [jax.ShapeDtypeStruct((t, D_MODEL), F32)]
    if emit_next:
        out_shape.append(jax.ShapeDtypeStruct((t, D_MODEL), BF16))
    out_specs = [whole((t, D_MODEL))] * len(out_shape) + [col, col, row]
    out_shape += [jax.ShapeDtypeStruct(w.shape, BF16) for w in (wg, wu, wd)]
    return pl.pallas_call(
        functools.partial(_ffn_stream_body, emit_next),
        grid=(D_FF // FF_CHUNK,),
        in_specs=[whole((t, D_MODEL)), whole(gains.shape), col, col, row],
        out_specs=out_specs,
        out_shape=out_shape,
        scratch_shapes=[pltpu.VMEM((t, D_MODEL), BF16), pltpu.VMEM((t, D_MODEL), F32)],
        compiler_params=_params(1),
        name=name,
    )(x, gains, wg, wu, wd)


def _gelu(x):
    return 0.5 * x * (1.0 + lax.erf(x * np.sqrt(0.5).astype(np.float32)))


def _layer_norm(x, g, b):
    mu = jnp.mean(x, axis=-1, keepdims=True)
    xc = x - mu
    var = jnp.mean(xc * xc, axis=-1, keepdims=True)
    return xc * lax.rsqrt(var + EPS) * g + b


IN_PROJ_LN_PIECES = 8


def _in_proj_body(q_blocked, h_ref, w_ref, ln_ref, q_ref, kv_ref, *refs):
    if q_blocked:
        packed_ref, gv_ref = refs
        u_ref, vn_ref, ga_ref, gb_ref = (packed_ref.at[:, k * D_MODEL:(k + 1) * D_MODEL] for k in range(4))
        w16_ref = None
    else:
        u_ref, vn_ref, ga_ref, gb_ref, w16_ref, gv_ref = refs
    _in_proj_compute(q_blocked, h_ref, w_ref, w16_ref, ln_ref, q_ref, kv_ref, u_ref, vn_ref, ga_ref, gb_ref,
                     gv_ref)


def _in_proj_compute(q_blocked, h_ref, w_ref, w16_ref, ln_ref, q_ref, kv_ref, u_ref, vn_ref, ga_ref, gb_ref,
                     gv_ref):
    h = h_ref[...]
    tm = h.shape[0]
    n_chunks = D_MODEL // MXU_TILE
    q_scale = HEAD_DIM ** -0.5 * (LOG2E if q_blocked else 1.0)

    def store_q(c, r):
        q = (r * q_scale).astype(q_ref.dtype)
        if q_blocked:
            for b in range(tm // QBLK):
                for j in range(MXU_TILE // LANES):
                    p = c * (MXU_TILE // LANES) + j
                    q_ref[b, p * QBLK:(p + 1) * QBLK, :] = q[b * QBLK:(b + 1) * QBLK, j * LANES:(j + 1) * LANES]
        else:
            q_ref[:, c * MXU_TILE:(c + 1) * MXU_TILE] = q

    def store_cols(ref, fn):
        def store(c, r):
            ref[:, c * MXU_TILE:(c + 1) * MXU_TILE] = fn(r).astype(ref.dtype)
        return store

    def layer_norm_piece(k):
        rows = slice(k * (tm // IN_PROJ_LN_PIECES), (k + 1) * (tm // IN_PROJ_LN_PIECES))
        vn_ref[rows, :] = _layer_norm(_gelu(gv_ref[rows, :]), ln_ref[0:1, :], ln_ref[1:2, :]).astype(vn_ref.dtype)

    def run(off, c, store):
        cols = slice(off + c * MXU_TILE, off + (c + 1) * MXU_TILE)
        w = w_ref[:, cols]
        if w16_ref is not None:
            w = w.astype(BF16)
            w16_ref[:, cols] = w
        store(c, _dot(h, w))

    store_gv = lambda c, r: gv_ref.__setitem__((slice(None), slice(c * MXU_TILE, (c + 1) * MXU_TILE)), r)
    store_u, store_ga, store_gb = (store_cols(u_ref, _gelu), store_cols(ga_ref, jax.nn.sigmoid),
                                   store_cols(gb_ref, jax.nn.sigmoid))
    for c in range(n_chunks):
        run(OFF_GV, c, store_gv)
    run(OFF_KV, 0, lambda c, r: kv_ref.__setitem__(Ellipsis, r))
    for c in range(n_chunks):
        run(OFF_Q, c, store_q)
        layer_norm_piece(2 * c)
        run(OFF_GA, c, store_ga)
        layer_norm_piece(2 * c + 1)
    for c in range(n_chunks):
        run(OFF_U, c, store_u)
        run(OFF_GB, c, store_gb)


def _in_proj(h, w_in, ln, *, tm, vn_dtype, q_blocked, name):
    t = h.shape[0]
    assert t % tm == 0
    row = lambda i: (i, 0)
    wide = lambda dt: jax.ShapeDtypeStruct((t, D_MODEL), dt)
    wide_spec = pl.BlockSpec((tm, D_MODEL), row)
    if q_blocked:
        assert tm % QBLK == 0 and vn_dtype == BF16
        q_shape = jax.ShapeDtypeStruct((t // QBLK, N_HEAD_PAIRS * QBLK, LANES), BF16)
        q_spec = pl.BlockSpec((tm // QBLK, N_HEAD_PAIRS * QBLK, LANES), lambda i: (i, 0, 0))
        act_shapes = [jax.ShapeDtypeStruct((t, 4 * D_MODEL), BF16)]
        act_specs = [pl.BlockSpec((tm, 4 * D_MODEL), row)]
    else:
        assert t == tm and w_in.dtype == F32
        q_shape, q_spec = wide(BF16), wide_spec
        act_shapes = [wide(BF16), wide(vn_dtype), wide(BF16), wide(BF16),
                      jax.ShapeDtypeStruct(w_in.shape, BF16)]
        act_specs = [wide_spec] * 4 + [pl.BlockSpec(w_in.shape, lambda i: (0, 0))]
    return pl.pallas_call(
        functools.partial(_in_proj_body, q_blocked),
        grid=(t // tm,),
        in_specs=[wide_spec, _resident(w_in.shape), _resident(ln.shape)],
        out_specs=[q_spec, pl.BlockSpec((tm, 2 * KV_WIDTH), row)] + act_specs,
        out_shape=[q_shape, jax.ShapeDtypeStruct((t, 2 * KV_WIDTH), F32)] + act_shapes,
        scratch_shapes=[pltpu.VMEM((tm, GMLP_WIDTH), F32)],
        compiler_params=_params(1),
        name=name,
    )(h, w_in, ln)


def _t5_bucket(rel):
    half = NUM_BUCKETS // 2
    max_exact = half // 2
    ret = np.where(rel > 0, half, 0)
    n = np.abs(rel)
    nf = np.maximum(n, 1).astype(np.float32)
    scaled = (np.log(nf / np.float32(max_exact)) / np.float32(math.log(MAX_DISTANCE / max_exact))
              * np.float32(half - max_exact))
    exact = np.log(np.maximum(n, 1) / max_exact) / math.log(MAX_DISTANCE / max_exact) * (half - max_exact)
    assert np.array_equal(scaled.astype(np.int32), exact.astype(np.int32))
    large = np.minimum(max_exact + scaled.astype(np.int32), half - 1)
    return (ret + np.where(n < max_exact, n, large)).astype(np.int32)


def _bias_body(table_ref, bucket_ref, out_ref):
    bucket = bucket_ref[...]
    hits = [bucket == b for b in range(NUM_BUCKETS)]
    for head in range(N_Q_HEADS):
        acc = jnp.zeros(bucket.shape, F32)
        for b in range(NUM_BUCKETS):
            acc = jnp.where(hits[b], table_ref[b, head], acc)
        out_ref[head] = acc


def _prompt_bias_body(table_ref, bucket_ref, out_ref):
    bucket = bucket_ref[...]
    hits = [bucket == b for b in range(NUM_BUCKETS)]
    key_chunk = lax.broadcasted_iota(jnp.int32, bucket.shape, 0) // CHUNK
    q_chunk = lax.broadcasted_iota(jnp.int32, bucket.shape, 1) // CHUNK
    in_window = jnp.logical_and(key_chunk >= q_chunk, key_chunk <= q_chunk + WINDOW // CHUNK)
    valid = [in_window, jnp.logical_and(in_window, key_chunk >= WINDOW // CHUNK)]
    for head in range(N_Q_HEADS):
        acc = jnp.zeros(bucket.shape, F32)
        for b in range(NUM_BUCKETS):
            acc = jnp.where(hits[b], table_ref[b, head], acc)
        h, rest = divmod(head, Q_PER_KV)
        pair, par = divmod(rest, HEADS_PER_TILE)
        pg, pl_ = divmod(pair, GROUP_PAIRS)
        for v in range(2):
            out_ref[v, h, pg, par * KBLK:(par + 1) * KBLK, pl_ * QBLK:(pl_ + 1) * QBLK] = (
                jnp.where(valid[v], acc * LOG2E, NEG_INF))


def _prompt_bias(table):
    rel = (np.arange(KBLK) - WINDOW)[:, None] - np.arange(QBLK)[None, :]
    bucket = jnp.asarray(_t5_bucket(rel))
    return pl.pallas_call(
        _prompt_bias_body,
        in_specs=[pl.BlockSpec(memory_space=pltpu.SMEM),
                  pl.BlockSpec(memory_space=pltpu.VMEM)],
        out_specs=pl.BlockSpec(memory_space=pltpu.VMEM),
        out_shape=jax.ShapeDtypeStruct((2, N_KV_HEADS, PAIR_GROUPS, 2 * KBLK, GROUP_PAIRS * QBLK), F32),
        name="rel_bias_prompt",
    )(table, bucket)


def _relative_bias(table, n_q, n_keys, n_past):
    rel = (np.arange(n_keys) - n_past)[None, :] - np.arange(n_q)[:, None]
    bucket = jnp.asarray(_t5_bucket(rel))
    bias = pl.pallas_call(
        _bias_body,
        in_specs=[pl.BlockSpec(memory_space=pltpu.SMEM),
                  pl.BlockSpec(memory_space=pltpu.VMEM)],
        out_specs=pl.BlockSpec(memory_space=pltpu.VMEM),
        out_shape=jax.ShapeDtypeStruct((N_Q_HEADS, n_q, n_keys), F32),
        name="rel_bias_%d" % n_q,
    )(table, bucket)
    return bias.reshape(N_KV_HEADS, Q_PER_KV * n_q, n_keys)


def _sink_attention(qh, kh, vh, bias, sink, invalid=None):
    s = lax.dot_general(qh, kh, (((1,), (1,)), ((), ())), preferred_element_type=F32) + bias
    if invalid is not None:
        s = jnp.where(invalid, NEG_INF, s)
    m = jnp.maximum(jnp.max(s, axis=-1, keepdims=True), sink)
    p = jnp.exp(s - m)
    denom = jnp.sum(p, axis=-1, keepdims=True) + jnp.exp(sink - m)
    return _dot(p.astype(BF16), vh) / denom


def _stack_heads(q, kv_head):
    base = kv_head * Q_PER_KV * HEAD_DIM
    return jnp.concatenate(
        [q[:, base + g * HEAD_DIM: base + (g + 1) * HEAD_DIM] for g in range(Q_PER_KV)], axis=0)


def _unstack_heads(o, n):
    return jnp.concatenate([o[g * n:(g + 1) * n, :] for g in range(Q_PER_KV)], axis=1)


def _merge_out(x1, attn, gm, ga, gb, wba, wbg, wo, g_post):
    merged = ga * _dot(attn, wba) + gb * _dot(gm, wbg)
    return x1 + _rms(_dot(merged.astype(BF16), wo), g_post)


def _prompt_mixer_body(ts, q_ref, kv_ref, kvp_ref, packed_ref, x1_ref,
                       bias_ref, sink_ref, ws_ref, bs_ref, wba_ref, wbg_ref, wo_ref, g_ref,
                       out_ref, attn_t_ref, gm_ref):
    u_ref, vn_ref, ga_ref, gb_ref = (packed_ref.at[0, :, k * D_MODEL:(k + 1) * D_MODEL] for k in range(4))
    first_variant = jnp.where(pl.program_id(1) == 0, 1, 0)
    kv_all = jnp.concatenate([kvp_ref[0], kv_ref[0]], axis=0)
    k_all = kv_all[:, :KV_WIDTH]
    v_t = jnp.transpose(kv_all[:, KV_WIDTH:]).astype(BF16)
    low = lax.broadcasted_iota(jnp.int32, k_all.shape, 1) < HEAD_DIM
    k_swapped = pltpu.roll(k_all, HEAD_DIM, axis=1)
    zero = jnp.zeros_like(k_all)
    k_par = [[jnp.where(low, k_all, zero), jnp.where(low, zero, k_swapped)],
             [jnp.where(low, k_swapped, zero), jnp.where(low, zero, k_all)]]
    k_par = [[k.astype(BF16) for k in ks] for ks in k_par]

    units = [(blk, h, pg) for blk in range(ts // QBLK) for h in range(N_KV_HEADS)
             for pg in range(PAIR_GROUPS)]

    def scores(blk, h, pg):
        keys = slice(blk * QBLK, blk * QBLK + KBLK)
        k_blk = jnp.concatenate([k_par[h][0][keys], k_par[h][1][keys]], axis=0)
        pair0 = h * PAIRS_PER_KV + pg * GROUP_PAIRS
        qa = q_ref[blk, pair0 * QBLK:(pair0 + GROUP_PAIRS) * QBLK, :]
        s = lax.dot_general(k_blk, qa, (((1,), (1,)), ((), ())), preferred_element_type=F32)
        variant = first_variant if blk == 0 else 0
        return s + bias_ref[variant, h, pg]

    def softmax(s, blk, h, pg):
        out = []
        for par in range(HEADS_PER_TILE):
            sp = s[par * KBLK:(par + 1) * KBLK]
            row = (h * PAIR_GROUPS + pg) * HEADS_PER_TILE + par
            sink = sink_ref[row:row + 1, :] * LOG2E
            m = jnp.maximum(jnp.max(sp, axis=0, keepdims=True), sink)
            p = jnp.exp2(sp - m)
            denom = jnp.sum(p, axis=0, keepdims=True) + jnp.exp2(sink - m)
            out.append((p.astype(BF16), 1.0 / denom))
        return out

    def weighted_values(probs, blk, h, pg):
        keys = slice(blk * QBLK, blk * QBLK + KBLK)
        cols = slice(blk * QBLK, (blk + 1) * QBLK)
        vh_t = v_t[h * HEAD_DIM:(h + 1) * HEAD_DIM, keys]
        pair0 = h * PAIRS_PER_KV + pg * GROUP_PAIRS
        for par, (p, inv) in enumerate(probs):
            o = _dot(vh_t, p) * inv
            for pl_ in range(GROUP_PAIRS):
                head = (pair0 + pl_) * HEADS_PER_TILE + par
                attn_t_ref[head * HEAD_DIM:(head + 1) * HEAD_DIM, cols] = o[:, pl_ * QBLK:(pl_ + 1) * QBLK]

    blk_i = lax.broadcasted_iota(jnp.int32, (GMLP_CHUNK, GMLP_CHUNK), 0) // CHUNK
    blk_j = lax.broadcasted_iota(jnp.int32, (GMLP_CHUNK, GMLP_CHUNK), 1) // CHUNK
    for g in range(GMLP_GROUPS):
        w = jnp.where(blk_j <= blk_i, ws_ref[g], 0.0).astype(BF16)
        b = bs_ref[:, g:g + 1]
        cols = slice(g * GMLP_GROUP_DIM, (g + 1) * GMLP_GROUP_DIM)
        for c in range(ts // GMLP_CHUNK):
            rows = slice(c * GMLP_CHUNK, (c + 1) * GMLP_CHUNK)
            sp = _dot(w, vn_ref[rows, cols]) + b
            gm_ref[rows, cols] = (u_ref[rows, cols].astype(F32) * sp).astype(BF16)

    def gmlp_branch(c):
        cols = slice(c * SIDE_COLS, (c + 1) * SIDE_COLS)
        return gb_ref[:, cols].astype(F32) * _dot(gm_ref[...], wbg_ref[:, cols])

    n_side = D_MODEL // SIDE_COLS
    side = []
    s_vals, p_vals = {}, {}
    for i in range(len(units) + 2):
        if i < len(units):
            s_vals[i] = scores(*units[i])
        if 1 <= i <= len(units):
            p_vals[i - 1] = softmax(s_vals.pop(i - 1), *units[i - 1])
        if i >= 2:
            weighted_values(p_vals.pop(i - 2), *units[i - 2])
        if i % (len(units) // n_side) == 1 and len(side) < n_side:
            side.append(gmlp_branch(len(side)))
    assert len(side) == n_side

    attn = jnp.transpose(attn_t_ref[...]).astype(BF16)
    merged = ga_ref[...].astype(F32) * _dot(attn, wba_ref[...]) + jnp.concatenate(side, axis=1)
    out_ref[0] = x1_ref[0] + _rms(_dot(merged.astype(BF16), wo_ref[...]), g_ref[...])


def _prompt_mixer(q, kv, packed, x1, bias, sink, w_s, b_s_t, wba, wbg, wo, g_post, *, ts):
    batch, seq, _ = kv.shape
    assert seq % ts == 0 and ts % QBLK == 0 and QBLK == GMLP_CHUNK == WINDOW
    tile = lambda b, t: (b, t, 0)
    prev = lambda b, t: (b, jnp.maximum(t * (ts // WINDOW) - 1, 0), 0)
    wide = pl.BlockSpec((1, ts, D_MODEL), tile)
    n_t = seq // ts
    q_spec = pl.BlockSpec((ts // QBLK, N_HEAD_PAIRS * QBLK, LANES), lambda b, t: (b * n_t + t, 0, 0))
    return pl.pallas_call(
        functools.partial(_prompt_mixer_body, ts),
        grid=(batch, n_t),
        in_specs=[q_spec, pl.BlockSpec((1, ts, 2 * KV_WIDTH), tile),
                  pl.BlockSpec((1, WINDOW, 2 * KV_WIDTH), prev),
                  pl.BlockSpec((1, ts, 4 * D_MODEL), tile), wide,
                  _resident(bias.shape), _resident(sink.shape), _resident(w_s.shape),
                  _resident(b_s_t.shape), _resident(wba.shape), _resident(wbg.shape),
                  _resident(wo.shape), _resident(g_post.shape)],
        out_specs=wide,
        out_shape=jax.ShapeDtypeStruct((batch, seq, D_MODEL), F32),
        scratch_shapes=[pltpu.VMEM((ATTN_WIDTH, ts), F32), pltpu.VMEM((ts, GMLP_WIDTH), BF16)],
        compiler_params=_params(2),
        name="prompt_mixer",
    )(q, kv, kv, packed, x1, bias, sink, w_s, b_s_t, wba, wbg, wo, g_post)


def _sample_mixer_body(n_batch, n_new, q_ref, kv_ref, ck_ref, cv_ref, u_ref, vn_ref, ga_ref, gb_ref,
                       x1_ref, bias_ref, sink_ref, ws_ref, bs_ref, wba_ref, wbg_ref, wo_ref, g_ref,
                       out_ref, wba16_ref, wbg16_ref, wo16_ref, attn_ref, gm_ref):
    for b in range(n_batch):
        rows = slice(b * n_new, (b + 1) * n_new)
        q = q_ref[rows, :]
        kv = kv_ref[rows, :]
        k_all = jnp.concatenate([ck_ref[b], kv[:, :KV_WIDTH]], axis=0).astype(BF16)
        v_all = jnp.concatenate([cv_ref[b], kv[:, KV_WIDTH:]], axis=0).astype(BF16)
        for h in range(N_KV_HEADS):
            cols = slice(h * HEAD_DIM, (h + 1) * HEAD_DIM)
            o = _sink_attention(_stack_heads(q, h), k_all[:, cols], v_all[:, cols],
                                bias_ref[h], sink_ref[h])
            width = Q_PER_KV * HEAD_DIM
            attn_ref[rows, h * width:(h + 1) * width] = _unstack_heads(o, n_new).astype(BF16)
        for g in range(GMLP_GROUPS):
            cols = slice(g * GMLP_GROUP_DIM, (g + 1) * GMLP_GROUP_DIM)
            w = ws_ref[g, :n_new, :n_new].astype(BF16)
            sp = _dot(w, vn_ref[rows, cols].astype(BF16)) + bs_ref[:n_new, g:g + 1]
            gm_ref[rows, cols] = (u_ref[rows, cols].astype(F32) * sp).astype(BF16)

    weights = []
    for w_ref, w16_ref in ((wba_ref, wba16_ref), (wbg_ref, wbg16_ref), (wo_ref, wo16_ref)):
        w16_ref[...] = w_ref[...].astype(BF16)
        weights.append(w16_ref[...])
    out_ref[...] = _merge_out(x1_ref[...], attn_ref[...], gm_ref[...],
                              ga_ref[...].astype(F32), gb_ref[...].astype(F32), *weights, g_ref[...])


def _sample_mixer(q, kv, cache_k, cache_v, u, vn, ga, gb, x1, bias, sink, w_s, b_s_t,
                  wba, wbg, wo, g_post, *, n_batch, n_new):
    assert n_new <= CHUNK
    t = q.shape[0]
    vmem = pl.BlockSpec(memory_space=pltpu.VMEM)
    return pl.pallas_call(
        functools.partial(_sample_mixer_body, n_batch, n_new),
        in_specs=[vmem] * 17,
        out_specs=[vmem] * 4,
        out_shape=[jax.ShapeDtypeStruct((t, D_MODEL), F32)] + [jax.ShapeDtypeStruct(w.shape, BF16)
                                                              for w in (wba, wbg, wo)],
        scratch_shapes=[pltpu.VMEM((t, ATTN_WIDTH), BF16), pltpu.VMEM((t, GMLP_WIDTH), BF16)],
        compiler_params=pltpu.CompilerParams(vmem_limit_bytes=VMEM_LIMIT_BYTES),
        name="sample_mixer",
    )(q, kv, cache_k, cache_v, u, vn, ga, gb, x1, bias, sink, w_s, b_s_t, wba, wbg, wo, g_post)


def _prompt_sink_rows(sinks):
    s = sinks.astype(F32).reshape(N_KV_HEADS, PAIR_GROUPS, GROUP_PAIRS, HEADS_PER_TILE)
    s = jnp.transpose(s, (0, 1, 3, 2))[..., None]
    s = jnp.broadcast_to(s, (N_KV_HEADS, PAIR_GROUPS, HEADS_PER_TILE, GROUP_PAIRS, QBLK))
    return s.reshape(N_KV_HEADS * PAIR_GROUPS * HEADS_PER_TILE, GROUP_PAIRS * QBLK)


def _sink_rows(sinks, n_q):
    s = jnp.broadcast_to(sinks.astype(F32).reshape(N_KV_HEADS, Q_PER_KV, 1), (N_KV_HEADS, Q_PER_KV, n_q))
    return s.reshape(N_KV_HEADS, Q_PER_KV * n_q, 1)


PROMPT_TM = 512
PROMPT_TS = 512


def kernel(x_prompt, x_sample, cache_win_k, cache_win_v, rel_bias_table, norm_gains, ffn1_w_gate, ffn1_w_up, ffn1_w_down, w_in, attn_sinks, gmlp_ln_g, gmlp_ln_b, gmlp_w_s, gmlp_b_s, w_branch_attn, w_branch_gmlp, w_out, ffn2_w_gate, ffn2_w_up, ffn2_w_down):
    depth = norm_gains.shape[0]
    batch, seq, _ = x_prompt.shape
    dec_batch, dec_seq, _ = x_sample.shape
    n_cache = cache_win_k.shape[2]
    assert seq % PROMPT_TS == 0 and dec_seq <= CHUNK

    bias_p = _prompt_bias(rel_bias_table)
    bias_s = _relative_bias(rel_bias_table, dec_seq, n_cache + dec_seq, n_cache)

    xp = x_prompt.reshape(batch * seq, D_MODEL)
    xs = x_sample.reshape(dec_batch * dec_seq, D_MODEL)
    t_s = xs.shape[0]
    kp, vp, ks, vs, gs = [], [], [], [], []
    for l in range(depth):
        g = norm_gains[l].astype(F32)
        ln = jnp.stack([gmlp_ln_g[l], gmlp_ln_b[l]]).astype(F32)
        w_s = gmlp_w_s[l].astype(F32)
        b_s_t = jnp.transpose(gmlp_b_s[l]).astype(F32)
        g_ffn1, g_post2, g_ffn2 = g[0:3], g[3:4], g[4:6]
        sink_p = _prompt_sink_rows(attn_sinks[l])
        sink_s = _sink_rows(attn_sinks[l], dec_seq)

        x1, h2, *w1 = _ffn_stream(xs, g_ffn1, ffn1_w_gate[l], ffn1_w_up[l], ffn1_w_down[l],
                                  emit_next=True, name="ffn1_sample")
        q, kv, u, vn, ga, gb, win = _in_proj(h2, w_in[l], ln, tm=t_s, vn_dtype=F32, q_blocked=False,
                                             name="in_proj_sample")
        ck = cache_win_k[l].reshape(dec_batch, n_cache, KV_WIDTH)
        cv = cache_win_v[l].reshape(dec_batch, n_cache, KV_WIDTH)
        x2, wba, wbg, wo = _sample_mixer(q, kv, ck, cv, u, vn, ga, gb, x1, bias_s, sink_s, w_s, b_s_t,
                                         w_branch_attn[l], w_branch_gmlp[l], w_out[l], g_post2,
                                         n_batch=dec_batch, n_new=dec_seq)
        xs, *w2 = _ffn_stream(x2, g_ffn2, ffn2_w_gate[l], ffn2_w_up[l], ffn2_w_down[l],
                              emit_next=False, name="ffn2_sample")
        ks.append(kv[:, :KV_WIDTH].reshape(dec_batch, dec_seq, N_KV_HEADS, HEAD_DIM))
        vs.append(kv[:, KV_WIDTH:].reshape(dec_batch, dec_seq, N_KV_HEADS, HEAD_DIM))
        gs.append(vn.reshape(dec_batch, dec_seq, GMLP_WIDTH))

        x1, h2 = _ffn(xp, g_ffn1, *w1, tm=PROMPT_TM, emit_next=True, name="ffn1_prompt")
        q, kv, packed = _in_proj(h2, win, ln, tm=2 * PROMPT_TM, vn_dtype=BF16, q_blocked=True,
                                 name="in_proj_prompt")
        b3 = lambda a: a.reshape(batch, seq, a.shape[-1])
        x2 = _prompt_mixer(q, b3(kv), b3(packed), b3(x1), bias_p, sink_p,
                           w_s, b_s_t, wba, wbg, wo, g_post2, ts=PROMPT_TS)
        (xp,) = _ffn(x2.reshape(batch * seq, D_MODEL), g_ffn2, *w2, tm=PROMPT_TM, emit_next=False,
                     name="ffn2_prompt")
        kv_win = b3(kv)[:, seq - WINDOW:, :]
        kp.append(kv_win[..., :KV_WIDTH].reshape(batch, WINDOW, N_KV_HEADS, HEAD_DIM))
        vp.append(kv_win[..., KV_WIDTH:].reshape(batch, WINDOW, N_KV_HEADS, HEAD_DIM))

    return (xp.reshape(batch, seq, D_MODEL), xs.reshape(dec_batch, dec_seq, D_MODEL),
            jnp.stack(kp), jnp.stack(vp), jnp.stack(ks), jnp.stack(vs), jnp.stack(gs))
```

```python
import functools
import math

import jax
import jax.numpy as jnp
import numpy as np
from jax import lax
from jax.experimental import pallas as pl
from jax.experimental.pallas import tpu as pltpu

D_MODEL = 1024
CHUNK = 64
N_Q_HEADS = 16
N_KV_HEADS = 2
HEAD_DIM = 64
Q_PER_KV = N_Q_HEADS // N_KV_HEADS
ATTN_WIDTH = N_Q_HEADS * HEAD_DIM
KV_WIDTH = N_KV_HEADS * HEAD_DIM
WINDOW = 128
GMLP_WIDTH = 1024
GMLP_GROUPS = 4
GMLP_GROUP_DIM = GMLP_WIDTH // GMLP_GROUPS
GMLP_CHUNK = 128
NUM_BUCKETS = 32
MAX_DISTANCE = 128
D_FF = 2816
EPS = 1e-6
NEG_INF = -1e30
LOG2E = math.log2(math.e)

OFF_Q = 0
OFF_KV = ATTN_WIDTH
OFF_U = OFF_KV + 2 * KV_WIDTH
OFF_GV = OFF_U + GMLP_WIDTH
OFF_GA = OFF_GV + GMLP_WIDTH
OFF_GB = OFF_GA + D_MODEL

V7X_VMEM_BYTES = 64 * 1024 * 1024
VMEM_LIMIT_BYTES = V7X_VMEM_BYTES - 8 * 1024 * 1024
MXU_TILE = 256
LANES = 128

QBLK = 2 * CHUNK
KBLK = WINDOW + QBLK
HEADS_PER_TILE = LANES // HEAD_DIM
N_HEAD_PAIRS = N_Q_HEADS // HEADS_PER_TILE
PAIRS_PER_KV = Q_PER_KV // HEADS_PER_TILE
GROUP_PAIRS = 4
PAIR_GROUPS = PAIRS_PER_KV // GROUP_PAIRS
SIDE_COLS = MXU_TILE

BF16 = jnp.bfloat16
F32 = jnp.float32


def _dot(a, b):
    return jnp.dot(a, b, preferred_element_type=F32)


def _rms(x, g):
    return x * lax.rsqrt(jnp.mean(x * x, axis=-1, keepdims=True) + EPS) * g


def _resident(shape):
    zeros = (0,) * len(shape)
    return pl.BlockSpec(shape, lambda *_: zeros, pipeline_mode=pl.Buffered(1))


def _params(n_axes):
    return pltpu.CompilerParams(dimension_semantics=("arbitrary",) * n_axes,
                                vmem_limit_bytes=VMEM_LIMIT_BYTES)


CAST_BLOCK_BYTES = 2 * 1024 * 1024


def _cast_body(*refs):
    n = len(refs) // 2
    for w_ref, o_ref in zip(refs[:n], refs[n:]):
        o_ref[...] = w_ref[...].astype(o_ref.dtype)


def _to_bf16(*ws):
    k = ws[0].shape[0]
    assert all(w.ndim == 2 and w.shape[0] == k for w in ws)
    n_max = max(w.shape[1] for w in ws)
    rows = [r for r in range(16, k + 1, 16) if k % r == 0 and r * n_max * 4 <= CAST_BLOCK_BYTES]
    bk = max(rows) if rows else k
    specs = [pl.BlockSpec((bk, w.shape[1]), lambda i: (i, 0)) for w in ws]
    return pl.pallas_call(
        _cast_body,
        grid=(k // bk,),
        in_specs=specs,
        out_specs=specs,
        out_shape=[jax.ShapeDtypeStruct(w.shape, BF16) for w in ws],
        compiler_params=_params(1),
        name="weights_to_bf16",
    )(*ws)


def _ordering_zero(*arrays):
    m = None
    for a in arrays:
        r = jnp.max(jnp.max(a.astype(F32), axis=0, keepdims=True), axis=1, keepdims=True)
        m = r if m is None else jnp.maximum(m, r)
    bits = lax.bitcast_convert_type(m, jnp.uint32)
    return lax.bitcast_convert_type((bits >> 16) >> 16, F32)


FF_CHUNK = MXU_TILE
FFN_SIDE_PIECES = 8


def _ffn_body(n_tiles, emit_next, xp_ref, xe_ref, g_ref, wg_ref, wu_ref, wd_ref, *refs):
    y_ref = refs[0]
    h_ref, acc_ref, act_ref = refs[-3:]
    s = pl.program_id(0)
    slot = s % 2
    tm = y_ref.shape[0]

    def pre_norm(dst, rows):
        h_ref[dst, rows, :] = _rms(xp_ref[rows, :], g_ref[0:1, :]).astype(BF16)
        return [h_ref[dst, rows, :]]

    def finish(rows):
        y = xe_ref[rows, :] + 0.5 * _rms(acc_ref[rows, :], g_ref[1:2, :])
        y_ref[rows, :] = y
        stored = [y_ref[rows, :]]
        if emit_next:
            refs[1][rows, :] = _rms(y, g_ref[2:3, :]).astype(BF16)
            stored.append(refs[1][rows, :])
        return stored

    @pl.when(s == 0)
    def _():
        pre_norm(0, slice(None))
        acc_ref[...] = jnp.zeros_like(acc_ref)

    @pl.when(jnp.logical_and(s >= 1, s <= n_tiles))
    def _():
        h = h_ref[1 - slot]
        anchor = None
        piece_rows = tm // FFN_SIDE_PIECES
        assert D_FF // FF_CHUNK > FFN_SIDE_PIECES
        for ci in range(D_FF // FF_CHUNK):
            cols = slice(ci * FF_CHUNK, (ci + 1) * FF_CHUNK)
            gate = _dot(h, wg_ref[:, cols])
            up = _dot(h, wu_ref[:, cols])
            if anchor is not None:
                up = up + anchor
            act_ref[:, cols] = (jax.nn.silu(gate) * up).astype(BF16)
            anchor = None
            if ci < FFN_SIDE_PIECES:
                rows = slice(ci * piece_rows, (ci + 1) * piece_rows)
                anchor = _ordering_zero(*(finish(rows) + pre_norm(slot, rows)))
        acc_ref[...] = _dot(act_ref[...], wd_ref[...])

    @pl.when(s == n_tiles + 1)
    def _():
        finish(slice(None))


def _ffn(x, gains, wg, wu, wd, *, tm, emit_next, name):
    t = x.shape[0]
    assert t % tm == 0
    n_tiles = t // tm
    head = lambda s: (jnp.minimum(s, n_tiles - 1), 0)
    tail = lambda s: (jnp.clip(s - 2, 0, n_tiles - 1), 0)
    out_shape = [jax.ShapeDtypeStruct((t, D_MODEL), F32)]
    out_specs = [pl.BlockSpec((tm, D_MODEL), tail)]
    if emit_next:
        out_shape.append(jax.ShapeDtypeStruct((t, D_MODEL), BF16))
        out_specs.append(pl.BlockSpec((tm, D_MODEL), tail))
    return pl.pallas_call(
        functools.partial(_ffn_body, n_tiles, emit_next),
        grid=(n_tiles + 2,),
        in_specs=[pl.BlockSpec((tm, D_MODEL), head), pl.BlockSpec((tm, D_MODEL), tail),
                  _resident(gains.shape), _resident(wg.shape), _resident(wu.shape), _resident(wd.shape)],
        out_specs=out_specs,
        out_shape=out_shape,
        scratch_shapes=[pltpu.VMEM((2, tm, D_MODEL), BF16), pltpu.VMEM((tm, D_MODEL), F32),
                        pltpu.VMEM((tm, D_FF), BF16)],
        compiler_params=_params(1),
        name=name,
    )(x, x, gains, wg, wu, wd)


def _ffn_stream_body(emit_next, x_ref, g_ref, wg_ref, wu_ref, wd_ref, *refs):
    n_out = 2 if emit_next else 1
    y_ref = refs[0]
    wg16_ref, wu16_ref, wd16_ref = refs[n_out:n_out + 3]
    h_ref, acc_ref = refs[-2:]
    c = pl.program_id(0)

    @pl.when(c == 0)
    def _():
        h_ref[...] = _rms(x_ref[...], g_ref[0:1, :]).astype(BF16)
        acc_ref[...] = jnp.zeros_like(acc_ref)

    wg, wu, wd = (r[...].astype(BF16) for r in (wg_ref, wu_ref, wd_ref))
    wg16_ref[...], wu16_ref[...], wd16_ref[...] = wg, wu, wd
    h = h_ref[...]
    act = (jax.nn.silu(_dot(h, wg)) * _dot(h, wu)).astype(BF16)
    acc_ref[...] += _dot(act, wd)

    @pl.when(c == pl.num_programs(0) - 1)
    def _():
        y = x_ref[...] + 0.5 * _rms(acc_ref[...], g_ref[1:2, :])
        y_ref[...] = y
        if emit_next:
            refs[1][...] = _rms(y, g_ref[2:3, :]).astype(BF16)


FFN_DMA_SLOTS = 3


def _ffn_dma_body(emit_next, x_ref, g_ref, wg_hbm, wu_hbm, wd_hbm, *refs):
    n_out = 2 if emit_next else 1
    y_ref = refs[0]
    wg16_hbm, wu16_hbm, wd16_hbm = refs[n_out:n_out + 3]
    in_g, in_u, in_d, st_g, st_u, st_d, in_sem, out_sem = refs[n_out + 3:]
    n_chunks = D_FF // FF_CHUNK

    def chunk(ref, c, by_rows):
        return ref.at[pl.ds(c * FF_CHUNK, FF_CHUNK), :] if by_rows else ref.at[:, pl.ds(c * FF_CHUNK, FF_CHUNK)]

    def fetch(c):
        s = c % FFN_DMA_SLOTS
        return [pltpu.make_async_copy(chunk(hbm, c, by_rows), buf.at[s], in_sem.at[k, s])
                for k, (hbm, buf, by_rows) in enumerate(((wg_hbm, in_g, False), (wu_hbm, in_u, False),
                                                         (wd_hbm, in_d, True)))]

    def write_back(c):
        s = c % 2
        return [pltpu.make_async_copy(buf.at[s], chunk(hbm, c, by_rows), out_sem.at[k, s])
                for k, (hbm, buf, by_rows) in enumerate(((wg16_hbm, st_g, False), (wu16_hbm, st_u, False),
                                                         (wd16_hbm, st_d, True)))]

    for c in range(FFN_DMA_SLOTS - 1):
        for cp in fetch(c):
            cp.start()
    h = _rms(x_ref[...], g_ref[0:1, :]).astype(BF16)
    acc = None
    for c in range(n_chunks):
        for cp in fetch(c):
            cp.wait()
        nxt = c + FFN_DMA_SLOTS - 1
        if nxt < n_chunks:
            for cp in fetch(nxt):
                cp.start()
        s = c % FFN_DMA_SLOTS
        wg, wu, wd = in_g[s].astype(BF16), in_u[s].astype(BF16), in_d[s].astype(BF16)
        if c >= 2:
            for cp in write_back(c - 2):
                cp.wait()
        st_g[c % 2], st_u[c % 2], st_d[c % 2] = wg, wu, wd
        for cp in write_back(c):
            cp.start()
        act = (jax.nn.silu(_dot(h, wg)) * _dot(h, wu)).astype(BF16)
        part = _dot(act, wd)
        acc = part if acc is None else acc + part
    for c in range(n_chunks - 2, n_chunks):
        for cp in write_back(c):
            cp.wait()
    y = x_ref[...] + 0.5 * _rms(acc, g_ref[1:2, :])
    y_ref[...] = y
    if emit_next:
        refs[1][...] = _rms(y, g_ref[2:3, :]).astype(BF16)


def _ffn_dma(x, gains, wg, wu, wd, *, emit_next, name):
    t = x.shape[0]
    vmem = pl.BlockSpec(memory_space=pltpu.VMEM)
    hbm = pl.BlockSpec(memory_space=pl.ANY)
    out_shape = [jax.ShapeDtypeStruct((t, D_MODEL), F32)]
    if emit_next:
        out_shape.append(jax.ShapeDtypeStruct((t, D_MODEL), BF16))
    n_act = len(out_shape)
    out_shape += [jax.ShapeDtypeStruct(w.shape, BF16) for w in (wg, wu, wd)]
    col, row = (D_MODEL, FF_CHUNK), (FF_CHUNK, D_MODEL)
    return pl.pallas_call(
        functools.partial(_ffn_dma_body, emit_next),
        in_specs=[vmem, vmem, hbm, hbm, hbm],
        out_specs=[vmem] * n_act + [hbm] * 3,
        out_shape=out_shape,
        scratch_shapes=[pltpu.VMEM((FFN_DMA_SLOTS,) + col, F32), pltpu.VMEM((FFN_DMA_SLOTS,) + col, F32),
                        pltpu.VMEM((FFN_DMA_SLOTS,) + row, F32),
                        pltpu.VMEM((2,) + col, BF16), pltpu.VMEM((2,) + col, BF16),
                        pltpu.VMEM((2,) + row, BF16),
                        pltpu.SemaphoreType.DMA((3, FFN_DMA_SLOTS)), pltpu.SemaphoreType.DMA((3, 2))],
        compiler_params=pltpu.CompilerParams(vmem_limit_bytes=VMEM_LIMIT_BYTES),
        name=name,
    )(x, gains, wg, wu, wd)


def _ffn_stream(x, gains, wg, wu, wd, *, emit_next, name):
    t = x.shape[0]
    whole = lambda shape: pl.BlockSpec(shape, lambda c: (0, 0))
    col = pl.BlockSpec((D_MODEL, FF_CHUNK), lambda c: (0, c))
    row = pl.BlockSpec((FF_CHUNK, D_MODEL), lambda c: (c, 0))
    out_shape = [jax.ShapeDtypeStruct((t, D_MODEL), F32)]
    if emit_next:
        out_shape.append(jax.ShapeDtypeStruct((t, D_MODEL), BF16))
    out_specs = [whole((t, D_MODEL))] * len(out_shape) + [col, col, row]
    out_shape += [jax.ShapeDtypeStruct(w.shape, BF16) for w in (wg, wu, wd)]
    return pl.pallas_call(
        functools.partial(_ffn_stream_body, emit_next),
        grid=(D_FF // FF_CHUNK,),
        in_specs=[whole((t, D_MODEL)), whole(gains.shape), col, col, row],
        out_specs=out_specs,
        out_shape=out_shape,
        scratch_shapes=[pltpu.VMEM((t, D_MODEL), BF16), pltpu.VMEM((t, D_MODEL), F32)],
        compiler_params=_params(1),
        name=name,
    )(x, gains, wg, wu, wd)


def _gelu(x):
    return 0.5 * x * (1.0 + lax.erf(x * np.sqrt(0.5).astype(np.float32)))


def _layer_norm(x, g, b):
    mu = jnp.mean(x, axis=-1, keepdims=True)
    xc = x - mu
    var = jnp.mean(xc * xc, axis=-1, keepdims=True)
    return xc * lax.rsqrt(var + EPS) * g + b


IN_PROJ_LN_PIECES = 8


def _in_proj_body(q_blocked, h_ref, w_ref, ln_ref, q_ref, kv_ref, *refs):
    if q_blocked:
        packed_ref, gv_ref = refs
        u_ref, vn_ref, ga_ref, gb_ref = (packed_ref.at[:, k * D_MODEL:(k + 1) * D_MODEL] for k in range(4))
        w16_ref = None
    else:
        u_ref, vn_ref, ga_ref, gb_ref, w16_ref, gv_ref = refs
    _in_proj_compute(q_blocked, h_ref, w_ref, w16_ref, ln_ref, q_ref, kv_ref, u_ref, vn_ref, ga_ref, gb_ref,
                     gv_ref)


def _in_proj_compute(q_blocked, h_ref, w_ref, w16_ref, ln_ref, q_ref, kv_ref, u_ref, vn_ref, ga_ref, gb_ref,
                     gv_ref):
    h = h_ref[...]
    tm = h.shape[0]
    n_chunks = D_MODEL // MXU_TILE
    q_scale = HEAD_DIM ** -0.5 * (LOG2E if q_blocked else 1.0)

    def store_q(c, r):
        q = (r * q_scale).astype(q_ref.dtype)
        if q_blocked:
            for b in range(tm // QBLK):
                for j in range(MXU_TILE // LANES):
                    p = c * (MXU_TILE // LANES) + j
                    q_ref[b, p * QBLK:(p + 1) * QBLK, :] = q[b * QBLK:(b + 1) * QBLK, j * LANES:(j + 1) * LANES]
        else:
            q_ref[:, c * MXU_TILE:(c + 1) * MXU_TILE] = q

    def store_cols(ref, fn):
        def store(c, r):
            ref[:, c * MXU_TILE:(c + 1) * MXU_TILE] = fn(r).astype(ref.dtype)
        return store

    def layer_norm_piece(k):
        rows = slice(k * (tm // IN_PROJ_LN_PIECES), (k + 1) * (tm // IN_PROJ_LN_PIECES))
        vn_ref[rows, :] = _layer_norm(_gelu(gv_ref[rows, :]), ln_ref[0:1, :], ln_ref[1:2, :]).astype(vn_ref.dtype)

    def run(off, c, store):
        cols = slice(off + c * MXU_TILE, off + (c + 1) * MXU_TILE)
        w = w_ref[:, cols]
        if w16_ref is not None:
            w = w.astype(BF16)
            w16_ref[:, cols] = w
        store(c, _dot(h, w))

    store_gv = lambda c, r: gv_ref.__setitem__((slice(None), slice(c * MXU_TILE, (c + 1) * MXU_TILE)), r)
    store_u, store_ga, store_gb = (store_cols(u_ref, _gelu), store_cols(ga_ref, jax.nn.sigmoid),
                                   store_cols(gb_ref, jax.nn.sigmoid))
    for c in range(n_chunks):
        run(OFF_GV, c, store_gv)
    run(OFF_KV, 0, lambda c, r: kv_ref.__setitem__(Ellipsis, r))
    for c in range(n_chunks):
        run(OFF_Q, c, store_q)
        layer_norm_piece(2 * c)
        run(OFF_GA, c, store_ga)
        layer_norm_piece(2 * c + 1)
    for c in range(n_chunks):
        run(OFF_U, c, store_u)
        run(OFF_GB, c, store_gb)


def _in_proj(h, w_in, ln, *, tm, vn_dtype, q_blocked, name):
    t = h.shape[0]
    assert t % tm == 0
    row = lambda i: (i, 0)
    wide = lambda dt: jax.ShapeDtypeStruct((t, D_MODEL), dt)
    wide_spec = pl.BlockSpec((tm, D_MODEL), row)
    if q_blocked:
        assert tm % QBLK == 0 and vn_dtype == BF16
        q_shape = jax.ShapeDtypeStruct((t // QBLK, N_HEAD_PAIRS * QBLK, LANES), BF16)
        q_spec = pl.BlockSpec((tm // QBLK, N_HEAD_PAIRS * QBLK, LANES), lambda i: (i, 0, 0))
        act_shapes = [jax.ShapeDtypeStruct((t, 4 * D_MODEL), BF16)]
        act_specs = [pl.BlockSpec((tm, 4 * D_MODEL), row)]
    else:
        assert t == tm and w_in.dtype == F32
        q_shape, q_spec = wide(BF16), wide_spec
        act_shapes = [wide(BF16), wide(vn_dtype), wide(BF16), wide(BF16),
                      jax.ShapeDtypeStruct(w_in.shape, BF16)]
        act_specs = [wide_spec] * 4 + [pl.BlockSpec(w_in.shape, lambda i: (0, 0))]
    return pl.pallas_call(
        functools.partial(_in_proj_body, q_blocked),
        grid=(t // tm,),
        in_specs=[wide_spec, _resident(w_in.shape), _resident(ln.shape)],
        out_specs=[q_spec, pl.BlockSpec((tm, 2 * KV_WIDTH), row)] + act_specs,
        out_shape=[q_shape, jax.ShapeDtypeStruct((t, 2 * KV_WIDTH), F32)] + act_shapes,
        scratch_shapes=[pltpu.VMEM((tm, GMLP_WIDTH), F32)],
        compiler_params=_params(1),
        name=name,
    )(h, w_in, ln)


def _t5_bucket(rel):
    half = NUM_BUCKETS // 2
    max_exact = half // 2
    ret = np.where(rel > 0, half, 0)
    n = np.abs(rel)
    nf = np.maximum(n, 1).astype(np.float32)
    scaled = (np.log(nf / np.float32(max_exact)) / np.float32(math.log(MAX_DISTANCE / max_exact))
              * np.float32(half - max_exact))
    exact = np.log(np.maximum(n, 1) / max_exact) / math.log(MAX_DISTANCE / max_exact) * (half - max_exact)
    assert np.array_equal(scaled.astype(np.int32), exact.astype(np.int32))
    large = np.minimum(max_exact + scaled.astype(np.int32), half - 1)
    return (ret + np.where(n < max_exact, n, large)).astype(np.int32)


def _bias_body(table_ref, bucket_ref, out_ref):
    bucket = bucket_ref[...]
    hits = [bucket == b for b in range(NUM_BUCKETS)]
    for head in range(N_Q_HEADS):
        acc = jnp.zeros(bucket.shape, F32)
        for b in range(NUM_BUCKETS):
            acc = jnp.where(hits[b], table_ref[b, head], acc)
        out_ref[head] = acc


def _prompt_bias_body(table_ref, bucket_ref, out_ref):
    bucket = bucket_ref[...]
    hits = [bucket == b for b in range(NUM_BUCKETS)]
    key_chunk = lax.broadcasted_iota(jnp.int32, bucket.shape, 0) // CHUNK
    q_chunk = lax.broadcasted_iota(jnp.int32, bucket.shape, 1) // CHUNK
    in_window = jnp.logical_and(key_chunk >= q_chunk, key_chunk <= q_chunk + WINDOW // CHUNK)
    valid = [in_window, jnp.logical_and(in_window, key_chunk >= WINDOW // CHUNK)]
    for head in range(N_Q_HEADS):
        acc = jnp.zeros(bucket.shape, F32)
        for b in range(NUM_BUCKETS):
            acc = jnp.where(hits[b], table_ref[b, head], acc)
        h, rest = divmod(head, Q_PER_KV)
        pair, par = divmod(rest, HEADS_PER_TILE)
        pg, pl_ = divmod(pair, GROUP_PAIRS)
        for v in range(2):
            out_ref[v, h, pg, par * KBLK:(par + 1) * KBLK, pl_ * QBLK:(pl_ + 1) * QBLK] = (
                jnp.where(valid[v], acc * LOG2E, NEG_INF))


def _prompt_bias(table):
    rel = (np.arange(KBLK) - WINDOW)[:, None] - np.arange(QBLK)[None, :]
    bucket = jnp.asarray(_t5_bucket(rel))
    return pl.pallas_call(
        _prompt_bias_body,
        in_specs=[pl.BlockSpec(memory_space=pltpu.SMEM),
                  pl.BlockSpec(memory_space=pltpu.VMEM)],
        out_specs=pl.BlockSpec(memory_space=pltpu.VMEM),
        out_shape=jax.ShapeDtypeStruct((2, N_KV_HEADS, PAIR_GROUPS, 2 * KBLK, GROUP_PAIRS * QBLK), F32),
        name="rel_bias_prompt",
    )(table, bucket)


def _relative_bias(table, n_q, n_keys, n_past):
    rel = (np.arange(n_keys) - n_past)[None, :] - np.arange(n_q)[:, None]
    bucket = jnp.asarray(_t5_bucket(rel))
    bias = pl.pallas_call(
        _bias_body,
        in_specs=[pl.BlockSpec(memory_space=pltpu.SMEM),
                  pl.BlockSpec(memory_space=pltpu.VMEM)],
        out_specs=pl.BlockSpec(memory_space=pltpu.VMEM),
        out_shape=jax.ShapeDtypeStruct((N_Q_HEADS, n_q, n_keys), F32),
        name="rel_bias_%d" % n_q,
    )(table, bucket)
    return bias.reshape(N_KV_HEADS, Q_PER_KV * n_q, n_keys)


def _sink_attention(qh, kh, vh, bias, sink, invalid=None):
    s = lax.dot_general(qh, kh, (((1,), (1,)), ((), ())), preferred_element_type=F32) + bias
    if invalid is not None:
        s = jnp.where(invalid, NEG_INF, s)
    m = jnp.maximum(jnp.max(s, axis=-1, keepdims=True), sink)
    p = jnp.exp(s - m)
    denom = jnp.sum(p, axis=-1, keepdims=True) + jnp.exp(sink - m)
    return _dot(p.astype(BF16), vh) / denom


def _stack_heads(q, kv_head):
    base = kv_head * Q_PER_KV * HEAD_DIM
    return jnp.concatenate(
        [q[:, base + g * HEAD_DIM: base + (g + 1) * HEAD_DIM] for g in range(Q_PER_KV)], axis=0)


def _unstack_heads(o, n):
    return jnp.concatenate([o[g * n:(g + 1) * n, :] for g in range(Q_PER_KV)], axis=1)


def _merge_out(x1, attn, gm, ga, gb, wba_ref, wbg_ref, wo_ref, g_post):
    merged = ga * _dot(attn, wba_ref[...]) + gb * _dot(gm, wbg_ref[...])
    return x1 + _rms(_dot(merged.astype(BF16), wo_ref[...]), g_post)


def _prompt_mixer_body(ts, q_ref, kv_ref, kvp_ref, packed_ref, x1_ref,
                       bias_ref, sink_ref, ws_ref, bs_ref, wba_ref, wbg_ref, wo_ref, g_ref,
                       out_ref, attn_t_ref, gm_ref):
    u_ref, vn_ref, ga_ref, gb_ref = (packed_ref.at[0, :, k * D_MODEL:(k + 1) * D_MODEL] for k in range(4))
    first_variant = jnp.where(pl.program_id(1) == 0, 1, 0)
    kv_all = jnp.concatenate([kvp_ref[0], kv_ref[0]], axis=0)
    k_all = kv_all[:, :KV_WIDTH]
    v_t = jnp.transpose(kv_all[:, KV_WIDTH:]).astype(BF16)
    low = lax.broadcasted_iota(jnp.int32, k_all.shape, 1) < HEAD_DIM
    k_swapped = pltpu.roll(k_all, HEAD_DIM, axis=1)
    zero = jnp.zeros_like(k_all)
    k_par = [[jnp.where(low, k_all, zero), jnp.where(low, zero, k_swapped)],
             [jnp.where(low, k_swapped, zero), jnp.where(low, zero, k_all)]]
    k_par = [[k.astype(BF16) for k in ks] for ks in k_par]

    units = [(blk, h, pg) for blk in range(ts // QBLK) for h in range(N_KV_HEADS)
             for pg in range(PAIR_GROUPS)]

    def scores(blk, h, pg):
        keys = slice(blk * QBLK, blk * QBLK + KBLK)
        k_blk = jnp.concatenate([k_par[h][0][keys], k_par[h][1][keys]], axis=0)
        pair0 = h * PAIRS_PER_KV + pg * GROUP_PAIRS
        qa = q_ref[blk, pair0 * QBLK:(pair0 + GROUP_PAIRS) * QBLK, :]
        s = lax.dot_general(k_blk, qa, (((1,), (1,)), ((), ())), preferred_element_type=F32)
        variant = first_variant if blk == 0 else 0
        return s + bias_ref[variant, h, pg]

    def softmax(s, blk, h, pg):
        out = []
        for par in range(HEADS_PER_TILE):
            sp = s[par * KBLK:(par + 1) * KBLK]
            row = (h * PAIR_GROUPS + pg) * HEADS_PER_TILE + par
            sink = sink_ref[row:row + 1, :] * LOG2E
            m = jnp.maximum(jnp.max(sp, axis=0, keepdims=True), sink)
            p = jnp.exp2(sp - m)
            denom = jnp.sum(p, axis=0, keepdims=True) + jnp.exp2(sink - m)
            out.append((p.astype(BF16), 1.0 / denom))
        return out

    def weighted_values(probs, blk, h, pg):
        keys = slice(blk * QBLK, blk * QBLK + KBLK)
        cols = slice(blk * QBLK, (blk + 1) * QBLK)
        vh_t = v_t[h * HEAD_DIM:(h + 1) * HEAD_DIM, keys]
        pair0 = h * PAIRS_PER_KV + pg * GROUP_PAIRS
        for par, (p, inv) in enumerate(probs):
            o = _dot(vh_t, p) * inv
            for pl_ in range(GROUP_PAIRS):
                head = (pair0 + pl_) * HEADS_PER_TILE + par
                attn_t_ref[head * HEAD_DIM:(head + 1) * HEAD_DIM, cols] = o[:, pl_ * QBLK:(pl_ + 1) * QBLK]

    blk_i = lax.broadcasted_iota(jnp.int32, (GMLP_CHUNK, GMLP_CHUNK), 0) // CHUNK
    blk_j = lax.broadcasted_iota(jnp.int32, (GMLP_CHUNK, GMLP_CHUNK), 1) // CHUNK
    for g in range(GMLP_GROUPS):
        w = jnp.where(blk_j <= blk_i, ws_ref[g], 0.0).astype(BF16)
        b = bs_ref[:, g:g + 1]
        cols = slice(g * GMLP_GROUP_DIM, (g + 1) * GMLP_GROUP_DIM)
        for c in range(ts // GMLP_CHUNK):
            rows = slice(c * GMLP_CHUNK, (c + 1) * GMLP_CHUNK)
            sp = _dot(w, vn_ref[rows, cols]) + b
            gm_ref[rows, cols] = (u_ref[rows, cols].astype(F32) * sp).astype(BF16)

    def gmlp_branch(c):
        cols = slice(c * SIDE_COLS, (c + 1) * SIDE_COLS)
        return gb_ref[:, cols].astype(F32) * _dot(gm_ref[...], wbg_ref[:, cols])

    n_side = D_MODEL // SIDE_COLS
    side = []
    s_vals, p_vals = {}, {}
    for i in range(len(units) + 2):
        if i < len(units):
            s_vals[i] = scores(*units[i])
        if 1 <= i <= len(units):
            p_vals[i - 1] = softmax(s_vals.pop(i - 1), *units[i - 1])
        if i >= 2:
            weighted_values(p_vals.pop(i - 2), *units[i - 2])
        if i % (len(units) // n_side) == 1 and len(side) < n_side:
            side.append(gmlp_branch(len(side)))
    assert len(side) == n_side

    attn = jnp.transpose(attn_t_ref[...]).astype(BF16)
    merged = ga_ref[...].astype(F32) * _dot(attn, wba_ref[...]) + jnp.concatenate(side, axis=1)
    out_ref[0] = x1_ref[0] + _rms(_dot(merged.astype(BF16), wo_ref[...]), g_ref[...])


def _prompt_mixer(q, kv, packed, x1, bias, sink, w_s, b_s_t, wba, wbg, wo, g_post, *, ts):
    batch, seq, _ = kv.shape
    assert seq % ts == 0 and ts % QBLK == 0 and QBLK == GMLP_CHUNK == WINDOW
    tile = lambda b, t: (b, t, 0)
    prev = lambda b, t: (b, jnp.maximum(t * (ts // WINDOW) - 1, 0), 0)
    wide = pl.BlockSpec((1, ts, D_MODEL), tile)
    n_t = seq // ts
    q_spec = pl.BlockSpec((ts // QBLK, N_HEAD_PAIRS * QBLK, LANES), lambda b, t: (b * n_t + t, 0, 0))
    return pl.pallas_call(
        functools.partial(_prompt_mixer_body, ts),
        grid=(batch, n_t),
        in_specs=[q_spec, pl.BlockSpec((1, ts, 2 * KV_WIDTH), tile),
                  pl.BlockSpec((1, WINDOW, 2 * KV_WIDTH), prev),
                  pl.BlockSpec((1, ts, 4 * D_MODEL), tile), wide,
                  _resident(bias.shape), _resident(sink.shape), _resident(w_s.shape),
                  _resident(b_s_t.shape), _resident(wba.shape), _resident(wbg.shape),
                  _resident(wo.shape), _resident(g_post.shape)],
        out_specs=wide,
        out_shape=jax.ShapeDtypeStruct((batch, seq, D_MODEL), F32),
        scratch_shapes=[pltpu.VMEM((ATTN_WIDTH, ts), F32), pltpu.VMEM((ts, GMLP_WIDTH), BF16)],
        compiler_params=_params(2),
        name="prompt_mixer",
    )(q, kv, kv, packed, x1, bias, sink, w_s, b_s_t, wba, wbg, wo, g_post)


def _sample_mixer_body(n_batch, n_new, q_ref, kv_ref, ck_ref, cv_ref, u_ref, vn_ref, ga_ref, gb_ref,
                       x1_ref, bias_ref, sink_ref, ws_ref, bs_ref, wba_ref, wbg_ref, wo_ref, g_ref,
                       out_ref, attn_ref, gm_ref):
    for b in range(n_batch):
        rows = slice(b * n_new, (b + 1) * n_new)
        q = q_ref[rows, :]
        kv = kv_ref[rows, :]
        k_all = jnp.concatenate([ck_ref[b], kv[:, :KV_WIDTH]], axis=0).astype(BF16)
        v_all = jnp.concatenate([cv_ref[b], kv[:, KV_WIDTH:]], axis=0).astype(BF16)
        for h in range(N_KV_HEADS):
            cols = slice(h * HEAD_DIM, (h + 1) * HEAD_DIM)
            o = _sink_attention(_stack_heads(q, h), k_all[:, cols], v_all[:, cols],
                                bias_ref[h], sink_ref[h])
            width = Q_PER_KV * HEAD_DIM
            attn_ref[rows, h * width:(h + 1) * width] = _unstack_heads(o, n_new).astype(BF16)
        for g in range(GMLP_GROUPS):
            cols = slice(g * GMLP_GROUP_DIM, (g + 1) * GMLP_GROUP_DIM)
            w = ws_ref[g, :n_new, :n_new].astype(BF16)
            sp = _dot(w, vn_ref[rows, cols].astype(BF16)) + bs_ref[:n_new, g:g + 1]
            gm_ref[rows, cols] = (u_ref[rows, cols].astype(F32) * sp).astype(BF16)

    out_ref[...] = _merge_out(x1_ref[...], attn_ref[...], gm_ref[...],
                              ga_ref[...].astype(F32), gb_ref[...].astype(F32),
                              wba_ref, wbg_ref, wo_ref, g_ref[...])


def _sample_mixer(q, kv, cache_k, cache_v, u, vn, ga, gb, x1, bias, sink, w_s, b_s_t,
                  wba, wbg, wo, g_post, *, n_batch, n_new):
    assert n_new <= CHUNK
    t = q.shape[0]
    vmem = pl.BlockSpec(memory_space=pltpu.VMEM)
    return pl.pallas_call(
        functools.partial(_sample_mixer_body, n_batch, n_new),
        in_specs=[vmem] * 17,
        out_specs=vmem,
        out_shape=jax.ShapeDtypeStruct((t, D_MODEL), F32),
        scratch_shapes=[pltpu.VMEM((t, ATTN_WIDTH), BF16), pltpu.VMEM((t, GMLP_WIDTH), BF16)],
        compiler_params=pltpu.CompilerParams(vmem_limit_bytes=VMEM_LIMIT_BYTES),
        name="sample_mixer",
    )(q, kv, cache_k, cache_v, u, vn, ga, gb, x1, bias, sink, w_s, b_s_t, wba, wbg, wo, g_post)


def _prompt_sink_rows(sinks):
    s = sinks.astype(F32).reshape(N_KV_HEADS, PAIR_GROUPS, GROUP_PAIRS, HEADS_PER_TILE)
    s = jnp.transpose(s, (0, 1, 3, 2))[..., None]
    s = jnp.broadcast_to(s, (N_KV_HEADS, PAIR_GROUPS, HEADS_PER_TILE, GROUP_PAIRS, QBLK))
    return s.reshape(N_KV_HEADS * PAIR_GROUPS * HEADS_PER_TILE, GROUP_PAIRS * QBLK)


def _sink_rows(sinks, n_q):
    s = jnp.broadcast_to(sinks.astype(F32).reshape(N_KV_HEADS, Q_PER_KV, 1), (N_KV_HEADS, Q_PER_KV, n_q))
    return s.reshape(N_KV_HEADS, Q_PER_KV * n_q, 1)


PROMPT_TM = 512
PROMPT_TS = 512


def kernel(x_prompt, x_sample, cache_win_k, cache_win_v, rel_bias_table, norm_gains, ffn1_w_gate, ffn1_w_up, ffn1_w_down, w_in, attn_sinks, gmlp_ln_g, gmlp_ln_b, gmlp_w_s, gmlp_b_s, w_branch_attn, w_branch_gmlp, w_out, ffn2_w_gate, ffn2_w_up, ffn2_w_down):
    depth = norm_gains.shape[0]
    batch, seq, _ = x_prompt.shape
    dec_batch, dec_seq, _ = x_sample.shape
    n_cache = cache_win_k.shape[2]
    assert seq % PROMPT_TS == 0 and dec_seq <= CHUNK

    bias_p = _prompt_bias(rel_bias_table)
    bias_s = _relative_bias(rel_bias_table, dec_seq, n_cache + dec_seq, n_cache)

    xp = x_prompt.reshape(batch * seq, D_MODEL)
    xs = x_sample.reshape(dec_batch * dec_seq, D_MODEL)
    t_s = xs.shape[0]
    kp, vp, ks, vs, gs = [], [], [], [], []
    for l in range(depth):
        g = norm_gains[l].astype(F32)
        wba, wbg, wo = _to_bf16(w_branch_attn[l], w_branch_gmlp[l], w_out[l])
        ln = jnp.stack([gmlp_ln_g[l], gmlp_ln_b[l]]).astype(F32)
        w_s = gmlp_w_s[l].astype(F32)
        b_s_t = jnp.transpose(gmlp_b_s[l]).astype(F32)
        g_ffn1, g_post2, g_ffn2 = g[0:3], g[3:4], g[4:6]
        sink_p = _prompt_sink_rows(attn_sinks[l])
        sink_s = _sink_rows(attn_sinks[l], dec_seq)

        x1, h2, *w1 = _ffn_dma(xs, g_ffn1, ffn1_w_gate[l], ffn1_w_up[l], ffn1_w_down[l],
                                  emit_next=True, name="ffn1_sample")
        q, kv, u, vn, ga, gb, win = _in_proj(h2, w_in[l], ln, tm=t_s, vn_dtype=F32, q_blocked=False,
                                             name="in_proj_sample")
        ck = cache_win_k[l].reshape(dec_batch, n_cache, KV_WIDTH)
        cv = cache_win_v[l].reshape(dec_batch, n_cache, KV_WIDTH)
        x2 = _sample_mixer(q, kv, ck, cv, u, vn, ga, gb, x1, bias_s, sink_s, w_s, b_s_t,
                           wba, wbg, wo, g_post2, n_batch=dec_batch, n_new=dec_seq)
        xs, *w2 = _ffn_dma(x2, g_ffn2, ffn2_w_gate[l], ffn2_w_up[l], ffn2_w_down[l],
                              emit_next=False, name="ffn2_sample")
        ks.append(kv[:, :KV_WIDTH].reshape(dec_batch, dec_seq, N_KV_HEADS, HEAD_DIM))
        vs.append(kv[:, KV_WIDTH:].reshape(dec_batch, dec_seq, N_KV_HEADS, HEAD_DIM))
        gs.append(vn.reshape(dec_batch, dec_seq, GMLP_WIDTH))

        x1, h2 = _ffn(xp, g_ffn1, *w1, tm=PROMPT_TM, emit_next=True, name="ffn1_prompt")
        q, kv, packed = _in_proj(h2, win, ln, tm=2 * PROMPT_TM, vn_dtype=BF16, q_blocked=True,
                                 name="in_proj_prompt")
        b3 = lambda a: a.reshape(batch, seq, a.shape[-1])
        x2 = _prompt_mixer(q, b3(kv), b3(packed), b3(x1), bias_p, sink_p,
                           w_s, b_s_t, wba, wbg, wo, g_post2, ts=PROMPT_TS)
        (xp,) = _ffn(x2.reshape(batch * seq, D_MODEL), g_ffn2, *w2, tm=PROMPT_TM, emit_next=False,
                     name="ffn2_prompt")
        kv_win = b3(kv)[:, seq - WINDOW:, :]
        kp.append(kv_win[..., :KV_WIDTH].reshape(batch, WINDOW, N_KV_HEADS, HEAD_DIM))
        vp.append(kv_win[..., KV_WIDTH:].reshape(batch, WINDOW, N_KV_HEADS, HEAD_DIM))

    return (xp.reshape(batch, seq, D_MODEL), xs.reshape(dec_batch, dec_seq, D_MODEL),
            jnp.stack(kp), jnp.stack(vp), jnp.stack(ks), jnp.stack(vs), jnp.stack(gs))
```

```python
import functools
import math

import jax
import jax.numpy as jnp
import numpy as np
from jax import lax
from jax.experimental import pallas as pl
from jax.experimental.pallas import tpu as pltpu

D_MODEL = 1024
CHUNK = 64
N_Q_HEADS = 16
N_KV_HEADS = 2
HEAD_DIM = 64
Q_PER_KV = N_Q_HEADS // N_KV_HEADS
ATTN_WIDTH = N_Q_HEADS * HEAD_DIM
KV_WIDTH = N_KV_HEADS * HEAD_DIM
WINDOW = 128
GMLP_WIDTH = 1024
GMLP_GROUPS = 4
GMLP_GROUP_DIM = GMLP_WIDTH // GMLP_GROUPS
GMLP_CHUNK = 128
NUM_BUCKETS = 32
MAX_DISTANCE = 128
D_FF = 2816
EPS = 1e-6
NEG_INF = -1e30
LOG2E = math.log2(math.e)

OFF_Q = 0
OFF_KV = ATTN_WIDTH
OFF_U = OFF_KV + 2 * KV_WIDTH
OFF_GV = OFF_U + GMLP_WIDTH
OFF_GA = OFF_GV + GMLP_WIDTH
OFF_GB = OFF_GA + D_MODEL

V7X_VMEM_BYTES = 64 * 1024 * 1024
VMEM_LIMIT_BYTES = V7X_VMEM_BYTES - 8 * 1024 * 1024
MXU_TILE = 256
LANES = 128

QBLK = 2 * CHUNK
KBLK = WINDOW + QBLK
HEADS_PER_TILE = LANES // HEAD_DIM
N_HEAD_PAIRS = N_Q_HEADS // HEADS_PER_TILE
PAIRS_PER_KV = Q_PER_KV // HEADS_PER_TILE
GROUP_PAIRS = 4
PAIR_GROUPS = PAIRS_PER_KV // GROUP_PAIRS
SIDE_COLS = MXU_TILE

BF16 = jnp.bfloat16
F32 = jnp.float32


def _dot(a, b):
    return jnp.dot(a, b, preferred_element_type=F32)


def _rms(x, g):
    return x * lax.rsqrt(jnp.mean(x * x, axis=-1, keepdims=True) + EPS) * g


def _resident(shape):
    zeros = (0,) * len(shape)
    return pl.BlockSpec(shape, lambda *_: zeros, pipeline_mode=pl.Buffered(1))


def _params(n_axes):
    return pltpu.CompilerParams(dimension_semantics=("arbitrary",) * n_axes,
                                vmem_limit_bytes=VMEM_LIMIT_BYTES)


def _ordering_zero(*arrays):
    m = None
    for a in arrays:
        r = jnp.max(jnp.max(a.astype(F32), axis=0, keepdims=True), axis=1, keepdims=True)
        m = r if m is None else jnp.maximum(m, r)
    bits = lax.bitcast_convert_type(m, jnp.uint32)
    return lax.bitcast_convert_type((bits >> 16) >> 16, F32)


FF_CHUNK = MXU_TILE
FFN_SIDE_PIECES = 8


def _ffn_body(n_tiles, emit_next, xp_ref, xe_ref, g_ref, wg_ref, wu_ref, wd_ref, *refs):
    y_ref = refs[0]
    h_ref, acc_ref, act_ref = refs[-3:]
    s = pl.program_id(0)
    slot = s % 2
    tm = y_ref.shape[0]

    def pre_norm(dst, rows):
        h_ref[dst, rows, :] = _rms(xp_ref[rows, :], g_ref[0:1, :]).astype(BF16)
        return [h_ref[dst, rows, :]]

    def finish(rows):
        y = xe_ref[rows, :] + 0.5 * _rms(acc_ref[rows, :], g_ref[1:2, :])
        y_ref[rows, :] = y
        stored = [y_ref[rows, :]]
        if emit_next:
            refs[1][rows, :] = _rms(y, g_ref[2:3, :]).astype(BF16)
            stored.append(refs[1][rows, :])
        return stored

    @pl.when(s == 0)
    def _():
        pre_norm(0, slice(None))
        acc_ref[...] = jnp.zeros_like(acc_ref)

    @pl.when(jnp.logical_and(s >= 1, s <= n_tiles))
    def _():
        h = h_ref[1 - slot]
        anchor = None
        piece_rows = tm // FFN_SIDE_PIECES
        assert D_FF // FF_CHUNK > FFN_SIDE_PIECES
        for ci in range(D_FF // FF_CHUNK):
            cols = slice(ci * FF_CHUNK, (ci + 1) * FF_CHUNK)
            gate = _dot(h, wg_ref[:, cols])
            up = _dot(h, wu_ref[:, cols])
            if anchor is not None:
                up = up + anchor
            act_ref[:, cols] = (jax.nn.silu(gate) * up).astype(BF16)
            anchor = None
            if ci < FFN_SIDE_PIECES:
                rows = slice(ci * piece_rows, (ci + 1) * piece_rows)
                anchor = _ordering_zero(*(finish(rows) + pre_norm(slot, rows)))
        acc_ref[...] = _dot(act_ref[...], wd_ref[...])

    @pl.when(s == n_tiles + 1)
    def _():
        finish(slice(None))


def _ffn(x, gains, wg, wu, wd, *, tm, emit_next, name):
    t = x.shape[0]
    assert t % tm == 0
    n_tiles = t // tm
    head = lambda s: (jnp.minimum(s, n_tiles - 1), 0)
    tail = lambda s: (jnp.clip(s - 2, 0, n_tiles - 1), 0)
    out_shape = [jax.ShapeDtypeStruct((t, D_MODEL), F32)]
    out_specs = [pl.BlockSpec((tm, D_MODEL), tail)]
    if emit_next:
        out_shape.append(jax.ShapeDtypeStruct((t, D_MODEL), BF16))
        out_specs.append(pl.BlockSpec((tm, D_MODEL), tail))
    return pl.pallas_call(
        functools.partial(_ffn_body, n_tiles, emit_next),
        grid=(n_tiles + 2,),
        in_specs=[pl.BlockSpec((tm, D_MODEL), head), pl.BlockSpec((tm, D_MODEL), tail),
                  _resident(gains.shape), _resident(wg.shape), _resident(wu.shape), _resident(wd.shape)],
        out_specs=out_specs,
        out_shape=out_shape,
        scratch_shapes=[pltpu.VMEM((2, tm, D_MODEL), BF16), pltpu.VMEM((tm, D_MODEL), F32),
                        pltpu.VMEM((tm, D_FF), BF16)],
        compiler_params=_params(1),
        name=name,
    )(x, x, gains, wg, wu, wd)


FFN_DMA_SLOTS = 3


def _ffn_dma_body(emit_next, x_ref, g_ref, wg_hbm, wu_hbm, wd_hbm, *refs):
    n_out = 2 if emit_next else 1
    y_ref = refs[0]
    wg16_hbm, wu16_hbm, wd16_hbm = refs[n_out:n_out + 3]
    in_g, in_u, in_d, st_g, st_u, st_d, in_sem, out_sem = refs[n_out + 3:]
    n_chunks = D_FF // FF_CHUNK

    def chunk(ref, c, by_rows):
        return ref.at[pl.ds(c * FF_CHUNK, FF_CHUNK), :] if by_rows else ref.at[:, pl.ds(c * FF_CHUNK, FF_CHUNK)]

    def fetch(c):
        s = c % FFN_DMA_SLOTS
        return [pltpu.make_async_copy(chunk(hbm, c, by_rows), buf.at[s], in_sem.at[k, s])
                for k, (hbm, buf, by_rows) in enumerate(((wg_hbm, in_g, False), (wu_hbm, in_u, False),
                                                         (wd_hbm, in_d, True)))]

    def write_back(c):
        s = c % 2
        return [pltpu.make_async_copy(buf.at[s], chunk(hbm, c, by_rows), out_sem.at[k, s])
                for k, (hbm, buf, by_rows) in enumerate(((wg16_hbm, st_g, False), (wu16_hbm, st_u, False),
                                                         (wd16_hbm, st_d, True)))]

    for c in range(FFN_DMA_SLOTS - 1):
        for cp in fetch(c):
            cp.start()
    h = _rms(x_ref[...], g_ref[0:1, :]).astype(BF16)
    acc = None
    for c in range(n_chunks):
        for cp in fetch(c):
            cp.wait()
        nxt = c + FFN_DMA_SLOTS - 1
        if nxt < n_chunks:
            for cp in fetch(nxt):
                cp.start()
        s = c % FFN_DMA_SLOTS
        wg, wu, wd = in_g[s].astype(BF16), in_u[s].astype(BF16), in_d[s].astype(BF16)
        if c >= 2:
            for cp in write_back(c - 2):
                cp.wait()
        st_g[c % 2], st_u[c % 2], st_d[c % 2] = wg, wu, wd
        for cp in write_back(c):
            cp.start()
        act = (jax.nn.silu(_dot(h, wg)) * _dot(h, wu)).astype(BF16)
        part = _dot(act, wd)
        acc = part if acc is None else acc + part
    for c in range(n_chunks - 2, n_chunks):
        for cp in write_back(c):
            cp.wait()
    y = x_ref[...] + 0.5 * _rms(acc, g_ref[1:2, :])
    y_ref[...] = y
    if emit_next:
        refs[1][...] = _rms(y, g_ref[2:3, :]).astype(BF16)


def _ffn_dma(x, gains, wg, wu, wd, *, emit_next, name):
    t = x.shape[0]
    vmem = pl.BlockSpec(memory_space=pltpu.VMEM)
    hbm = pl.BlockSpec(memory_space=pl.ANY)
    out_shape = [jax.ShapeDtypeStruct((t, D_MODEL), F32)]
    if emit_next:
        out_shape.append(jax.ShapeDtypeStruct((t, D_MODEL), BF16))
    n_act = len(out_shape)
    out_shape += [jax.ShapeDtypeStruct(w.shape, BF16) for w in (wg, wu, wd)]
    col, row = (D_MODEL, FF_CHUNK), (FF_CHUNK, D_MODEL)
    return pl.pallas_call(
        functools.partial(_ffn_dma_body, emit_next),
        in_specs=[vmem, vmem, hbm, hbm, hbm],
        out_specs=[vmem] * n_act + [hbm] * 3,
        out_shape=out_shape,
        scratch_shapes=[pltpu.VMEM((FFN_DMA_SLOTS,) + col, F32), pltpu.VMEM((FFN_DMA_SLOTS,) + col, F32),
                        pltpu.VMEM((FFN_DMA_SLOTS,) + row, F32),
                        pltpu.VMEM((2,) + col, BF16), pltpu.VMEM((2,) + col, BF16),
                        pltpu.VMEM((2,) + row, BF16),
                        pltpu.SemaphoreType.DMA((3, FFN_DMA_SLOTS)), pltpu.SemaphoreType.DMA((3, 2))],
        compiler_params=pltpu.CompilerParams(vmem_limit_bytes=VMEM_LIMIT_BYTES),
        name=name,
    )(x, gains, wg, wu, wd)


def _gelu(x):
    return 0.5 * x * (1.0 + lax.erf(x * np.sqrt(0.5).astype(np.float32)))


def _layer_norm(x, g, b):
    mu = jnp.mean(x, axis=-1, keepdims=True)
    xc = x - mu
    var = jnp.mean(xc * xc, axis=-1, keepdims=True)
    return xc * lax.rsqrt(var + EPS) * g + b


IN_PROJ_LN_PIECES = 8


def _in_proj_body(q_blocked, h_ref, w_ref, ln_ref, q_ref, kv_ref, *refs):
    if q_blocked:
        packed_ref, gv_ref = refs
        u_ref, vn_ref, ga_ref, gb_ref = (packed_ref.at[:, k * D_MODEL:(k + 1) * D_MODEL] for k in range(4))
        w16_ref = None
    else:
        u_ref, vn_ref, ga_ref, gb_ref, w16_ref, gv_ref = refs
    _in_proj_compute(q_blocked, h_ref, w_ref, w16_ref, ln_ref, q_ref, kv_ref, u_ref, vn_ref, ga_ref, gb_ref,
                     gv_ref)


def _in_proj_compute(q_blocked, h_ref, w_ref, w16_ref, ln_ref, q_ref, kv_ref, u_ref, vn_ref, ga_ref, gb_ref,
                     gv_ref):
    h = h_ref[...]
    tm = h.shape[0]
    n_chunks = D_MODEL // MXU_TILE
    q_scale = HEAD_DIM ** -0.5 * (LOG2E if q_blocked else 1.0)

    def store_q(c, r):
        q = (r * q_scale).astype(q_ref.dtype)
        if q_blocked:
            for b in range(tm // QBLK):
                for j in range(MXU_TILE // LANES):
                    p = c * (MXU_TILE // LANES) + j
                    q_ref[b, p * QBLK:(p + 1) * QBLK, :] = q[b * QBLK:(b + 1) * QBLK, j * LANES:(j + 1) * LANES]
        else:
            q_ref[:, c * MXU_TILE:(c + 1) * MXU_TILE] = q

    def store_cols(ref, fn):
        def store(c, r):
            ref[:, c * MXU_TILE:(c + 1) * MXU_TILE] = fn(r).astype(ref.dtype)
        return store

    def layer_norm_piece(k):
        rows = slice(k * (tm // IN_PROJ_LN_PIECES), (k + 1) * (tm // IN_PROJ_LN_PIECES))
        vn_ref[rows, :] = _layer_norm(_gelu(gv_ref[rows, :]), ln_ref[0:1, :], ln_ref[1:2, :]).astype(vn_ref.dtype)

    def run(off, c, store):
        cols = slice(off + c * MXU_TILE, off + (c + 1) * MXU_TILE)
        w = w_ref[:, cols]
        if w16_ref is not None:
            w = w.astype(BF16)
            w16_ref[:, cols] = w
        store(c, _dot(h, w))

    store_gv = lambda c, r: gv_ref.__setitem__((slice(None), slice(c * MXU_TILE, (c + 1) * MXU_TILE)), r)
    store_u, store_ga, store_gb = (store_cols(u_ref, _gelu), store_cols(ga_ref, jax.nn.sigmoid),
                                   store_cols(gb_ref, jax.nn.sigmoid))
    for c in range(n_chunks):
        run(OFF_GV, c, store_gv)
    run(OFF_KV, 0, lambda c, r: kv_ref.__setitem__(Ellipsis, r))
    for c in range(n_chunks):
        run(OFF_Q, c, store_q)
        layer_norm_piece(2 * c)
        run(OFF_GA, c, store_ga)
        layer_norm_piece(2 * c + 1)
    for c in range(n_chunks):
        run(OFF_U, c, store_u)
        run(OFF_GB, c, store_gb)


def _in_proj(h, w_in, ln, *, tm, vn_dtype, q_blocked, name):
    t = h.shape[0]
    assert t % tm == 0
    row = lambda i: (i, 0)
    wide = lambda dt: jax.ShapeDtypeStruct((t, D_MODEL), dt)
    wide_spec = pl.BlockSpec((tm, D_MODEL), row)
    if q_blocked:
        assert tm % QBLK == 0 and vn_dtype == BF16
        q_shape = jax.ShapeDtypeStruct((t // QBLK, N_HEAD_PAIRS * QBLK, LANES), BF16)
        q_spec = pl.BlockSpec((tm // QBLK, N_HEAD_PAIRS * QBLK, LANES), lambda i: (i, 0, 0))
        act_shapes = [jax.ShapeDtypeStruct((t, 4 * D_MODEL), BF16)]
        act_specs = [pl.BlockSpec((tm, 4 * D_MODEL), row)]
    else:
        assert t == tm and w_in.dtype == F32
        q_shape, q_spec = wide(BF16), wide_spec
        act_shapes = [wide(BF16), wide(vn_dtype), wide(BF16), wide(BF16),
                      jax.ShapeDtypeStruct(w_in.shape, BF16)]
        act_specs = [wide_spec] * 4 + [pl.BlockSpec(w_in.shape, lambda i: (0, 0))]
    return pl.pallas_call(
        functools.partial(_in_proj_body, q_blocked),
        grid=(t // tm,),
        in_specs=[wide_spec, _resident(w_in.shape), _resident(ln.shape)],
        out_specs=[q_spec, pl.BlockSpec((tm, 2 * KV_WIDTH), row)] + act_specs,
        out_shape=[q_shape, jax.ShapeDtypeStruct((t, 2 * KV_WIDTH), F32)] + act_shapes,
        scratch_shapes=[pltpu.VMEM((tm, GMLP_WIDTH), F32)],
        compiler_params=_params(1),
        name=name,
    )(h, w_in, ln)


def _t5_bucket(rel):
    half = NUM_BUCKETS // 2
    max_exact = half // 2
    ret = np.where(rel > 0, half, 0)
    n = np.abs(rel)
    nf = np.maximum(n, 1).astype(np.float32)
    scaled = (np.log(nf / np.float32(max_exact)) / np.float32(math.log(MAX_DISTANCE / max_exact))
              * np.float32(half - max_exact))
    exact = np.log(np.maximum(n, 1) / max_exact) / math.log(MAX_DISTANCE / max_exact) * (half - max_exact)
    assert np.array_equal(scaled.astype(np.int32), exact.astype(np.int32))
    large = np.minimum(max_exact + scaled.astype(np.int32), half - 1)
    return (ret + np.where(n < max_exact, n, large)).astype(np.int32)


def _bias_body(table_ref, bucket_ref, out_ref):
    bucket = bucket_ref[...]
    hits = [bucket == b for b in range(NUM_BUCKETS)]
    for head in range(N_Q_HEADS):
        acc = jnp.zeros(bucket.shape, F32)
        for b in range(NUM_BUCKETS):
            acc = jnp.where(hits[b], table_ref[b, head], acc)
        out_ref[head] = acc


def _prompt_bias_body(table_ref, bucket_ref, out_ref):
    bucket = bucket_ref[...]
    hits = [bucket == b for b in range(NUM_BUCKETS)]
    key_chunk = lax.broadcasted_iota(jnp.int32, bucket.shape, 0) // CHUNK
    q_chunk = lax.broadcasted_iota(jnp.int32, bucket.shape, 1) // CHUNK
    in_window = jnp.logical_and(key_chunk >= q_chunk, key_chunk <= q_chunk + WINDOW // CHUNK)
    valid = [in_window, jnp.logical_and(in_window, key_chunk >= WINDOW // CHUNK)]
    for head in range(N_Q_HEADS):
        acc = jnp.zeros(bucket.shape, F32)
        for b in range(NUM_BUCKETS):
            acc = jnp.where(hits[b], table_ref[b, head], acc)
        h, rest = divmod(head, Q_PER_KV)
        pair, par = divmod(rest, HEADS_PER_TILE)
        pg, pl_ = divmod(pair, GROUP_PAIRS)
        for v in range(2):
            out_ref[v, h, pg, par * KBLK:(par + 1) * KBLK, pl_ * QBLK:(pl_ + 1) * QBLK] = (
                jnp.where(valid[v], acc * LOG2E, NEG_INF))


def _prompt_bias(table):
    rel = (np.arange(KBLK) - WINDOW)[:, None] - np.arange(QBLK)[None, :]
    bucket = jnp.asarray(_t5_bucket(rel))
    return pl.pallas_call(
        _prompt_bias_body,
        in_specs=[pl.BlockSpec(memory_space=pltpu.SMEM),
                  pl.BlockSpec(memory_space=pltpu.VMEM)],
        out_specs=pl.BlockSpec(memory_space=pltpu.VMEM),
        out_shape=jax.ShapeDtypeStruct((2, N_KV_HEADS, PAIR_GROUPS, 2 * KBLK, GROUP_PAIRS * QBLK), F32),
        name="rel_bias_prompt",
    )(table, bucket)


def _relative_bias(table, n_q, n_keys, n_past):
    rel = (np.arange(n_keys) - n_past)[None, :] - np.arange(n_q)[:, None]
    bucket = jnp.asarray(_t5_bucket(rel))
    bias = pl.pallas_call(
        _bias_body,
        in_specs=[pl.BlockSpec(memory_space=pltpu.SMEM),
                  pl.BlockSpec(memory_space=pltpu.VMEM)],
        out_specs=pl.BlockSpec(memory_space=pltpu.VMEM),
        out_shape=jax.ShapeDtypeStruct((N_Q_HEADS, n_q, n_keys), F32),
        name="rel_bias_%d" % n_q,
    )(table, bucket)
    return bias.reshape(N_KV_HEADS, Q_PER_KV * n_q, n_keys)


def _sink_attention(qh, kh, vh, bias, sink, invalid=None):
    s = lax.dot_general(qh, kh, (((1,), (1,)), ((), ())), preferred_element_type=F32) + bias
    if invalid is not None:
        s = jnp.where(invalid, NEG_INF, s)
    m = jnp.maximum(jnp.max(s, axis=-1, keepdims=True), sink)
    p = jnp.exp(s - m)
    denom = jnp.sum(p, axis=-1, keepdims=True) + jnp.exp(sink - m)
    return _dot(p.astype(BF16), vh) / denom


def _stack_heads(q, kv_head):
    base = kv_head * Q_PER_KV * HEAD_DIM
    return jnp.concatenate(
        [q[:, base + g * HEAD_DIM: base + (g + 1) * HEAD_DIM] for g in range(Q_PER_KV)], axis=0)


def _unstack_heads(o, n):
    return jnp.concatenate([o[g * n:(g + 1) * n, :] for g in range(Q_PER_KV)], axis=1)


def _merge_out(x1, attn, gm, ga, gb, wba, wbg, wo, g_post):
    merged = ga * _dot(attn, wba) + gb * _dot(gm, wbg)
    return x1 + _rms(_dot(merged.astype(BF16), wo), g_post)


def _prompt_mixer_body(ts, q_ref, kv_ref, kvp_ref, packed_ref, x1_ref,
                       bias_ref, sink_ref, ws_ref, bs_ref, wba_ref, wbg_ref, wo_ref, g_ref,
                       out_ref, attn_t_ref, gm_ref):
    u_ref, vn_ref, ga_ref, gb_ref = (packed_ref.at[0, :, k * D_MODEL:(k + 1) * D_MODEL] for k in range(4))
    first_variant = jnp.where(pl.program_id(1) == 0, 1, 0)
    kv_all = jnp.concatenate([kvp_ref[0], kv_ref[0]], axis=0)
    k_all = kv_all[:, :KV_WIDTH]
    v_t = jnp.transpose(kv_all[:, KV_WIDTH:]).astype(BF16)
    low = lax.broadcasted_iota(jnp.int32, k_all.shape, 1) < HEAD_DIM
    k_swapped = pltpu.roll(k_all, HEAD_DIM, axis=1)
    zero = jnp.zeros_like(k_all)
    k_par = [[jnp.where(low, k_all, zero), jnp.where(low, zero, k_swapped)],
             [jnp.where(low, k_swapped, zero), jnp.where(low, zero, k_all)]]
    k_par = [[k.astype(BF16) for k in ks] for ks in k_par]

    units = [(blk, h, pg) for blk in range(ts // QBLK) for h in range(N_KV_HEADS)
             for pg in range(PAIR_GROUPS)]

    def scores(blk, h, pg):
        keys = slice(blk * QBLK, blk * QBLK + KBLK)
        k_blk = jnp.concatenate([k_par[h][0][keys], k_par[h][1][keys]], axis=0)
        pair0 = h * PAIRS_PER_KV + pg * GROUP_PAIRS
        qa = q_ref[blk, pair0 * QBLK:(pair0 + GROUP_PAIRS) * QBLK, :]
        s = lax.dot_general(k_blk, qa, (((1,), (1,)), ((), ())), preferred_element_type=F32)
        variant = first_variant if blk == 0 else 0
        return s + bias_ref[variant, h, pg]

    def softmax(s, blk, h, pg):
        out = []
        for par in range(HEADS_PER_TILE):
            sp = s[par * KBLK:(par + 1) * KBLK]
            row = (h * PAIR_GROUPS + pg) * HEADS_PER_TILE + par
            sink = sink_ref[row:row + 1, :] * LOG2E
            m = jnp.maximum(jnp.max(sp, axis=0, keepdims=True), sink)
            p = jnp.exp2(sp - m)
            denom = jnp.sum(p, axis=0, keepdims=True) + jnp.exp2(sink - m)
            out.append((p.astype(BF16), 1.0 / denom))
        return out

    def weighted_values(probs, blk, h, pg):
        keys = slice(blk * QBLK, blk * QBLK + KBLK)
        cols = slice(blk * QBLK, (blk + 1) * QBLK)
        vh_t = v_t[h * HEAD_DIM:(h + 1) * HEAD_DIM, keys]
        pair0 = h * PAIRS_PER_KV + pg * GROUP_PAIRS
        for par, (p, inv) in enumerate(probs):
            o = _dot(vh_t, p) * inv
            for pl_ in range(GROUP_PAIRS):
                head = (pair0 + pl_) * HEADS_PER_TILE + par
                attn_t_ref[head * HEAD_DIM:(head + 1) * HEAD_DIM, cols] = o[:, pl_ * QBLK:(pl_ + 1) * QBLK]

    blk_i = lax.broadcasted_iota(jnp.int32, (GMLP_CHUNK, GMLP_CHUNK), 0) // CHUNK
    blk_j = lax.broadcasted_iota(jnp.int32, (GMLP_CHUNK, GMLP_CHUNK), 1) // CHUNK
    for g in range(GMLP_GROUPS):
        w = jnp.where(blk_j <= blk_i, ws_ref[g], 0.0).astype(BF16)
        b = bs_ref[:, g:g + 1]
        cols = slice(g * GMLP_GROUP_DIM, (g + 1) * GMLP_GROUP_DIM)
        for c in range(ts // GMLP_CHUNK):
            rows = slice(c * GMLP_CHUNK, (c + 1) * GMLP_CHUNK)
            sp = _dot(w, vn_ref[rows, cols]) + b
            gm_ref[rows, cols] = (u_ref[rows, cols].astype(F32) * sp).astype(BF16)

    def gmlp_branch(c):
        cols = slice(c * SIDE_COLS, (c + 1) * SIDE_COLS)
        return gb_ref[:, cols].astype(F32) * _dot(gm_ref[...], wbg_ref[:, cols])

    n_side = D_MODEL // SIDE_COLS
    side = []
    s_vals, p_vals = {}, {}
    for i in range(len(units) + 2):
        if i < len(units):
            s_vals[i] = scores(*units[i])
        if 1 <= i <= len(units):
            p_vals[i - 1] = softmax(s_vals.pop(i - 1), *units[i - 1])
        if i >= 2:
            weighted_values(p_vals.pop(i - 2), *units[i - 2])
        if i % (len(units) // n_side) == 1 and len(side) < n_side:
            side.append(gmlp_branch(len(side)))
    assert len(side) == n_side

    attn = jnp.transpose(attn_t_ref[...]).astype(BF16)
    merged = ga_ref[...].astype(F32) * _dot(attn, wba_ref[...]) + jnp.concatenate(side, axis=1)
    out_ref[0] = x1_ref[0] + _rms(_dot(merged.astype(BF16), wo_ref[...]), g_ref[...])


def _prompt_mixer(q, kv, packed, x1, bias, sink, w_s, b_s_t, wba, wbg, wo, g_post, *, ts):
    batch, seq, _ = kv.shape
    assert seq % ts == 0 and ts % QBLK == 0 and QBLK == GMLP_CHUNK == WINDOW
    tile = lambda b, t: (b, t, 0)
    prev = lambda b, t: (b, jnp.maximum(t * (ts // WINDOW) - 1, 0), 0)
    wide = pl.BlockSpec((1, ts, D_MODEL), tile)
    n_t = seq // ts
    q_spec = pl.BlockSpec((ts // QBLK, N_HEAD_PAIRS * QBLK, LANES), lambda b, t: (b * n_t + t, 0, 0))
    return pl.pallas_call(
        functools.partial(_prompt_mixer_body, ts),
        grid=(batch, n_t),
        in_specs=[q_spec, pl.BlockSpec((1, ts, 2 * KV_WIDTH), tile),
                  pl.BlockSpec((1, WINDOW, 2 * KV_WIDTH), prev),
                  pl.BlockSpec((1, ts, 4 * D_MODEL), tile), wide,
                  _resident(bias.shape), _resident(sink.shape), _resident(w_s.shape),
                  _resident(b_s_t.shape), _resident(wba.shape), _resident(wbg.shape),
                  _resident(wo.shape), _resident(g_post.shape)],
        out_specs=wide,
        out_shape=jax.ShapeDtypeStruct((batch, seq, D_MODEL), F32),
        scratch_shapes=[pltpu.VMEM((ATTN_WIDTH, ts), F32), pltpu.VMEM((ts, GMLP_WIDTH), BF16)],
        compiler_params=_params(2),
        name="prompt_mixer",
    )(q, kv, kv, packed, x1, bias, sink, w_s, b_s_t, wba, wbg, wo, g_post)


def _sample_mixer_body(n_batch, n_new, q_ref, kv_ref, ck_ref, cv_ref, u_ref, vn_ref, ga_ref, gb_ref,
                       x1_ref, bias_ref, sink_ref, ws_ref, bs_ref, wba_ref, wbg_ref, wo_ref, g_ref,
                       out_ref, wba16_ref, wbg16_ref, wo16_ref, attn_ref, gm_ref):
    for b in range(n_batch):
        rows = slice(b * n_new, (b + 1) * n_new)
        q = q_ref[rows, :]
        kv = kv_ref[rows, :]
        k_all = jnp.concatenate([ck_ref[b], kv[:, :KV_WIDTH]], axis=0).astype(BF16)
        v_all = jnp.concatenate([cv_ref[b], kv[:, KV_WIDTH:]], axis=0).astype(BF16)
        for h in range(N_KV_HEADS):
            cols = slice(h * HEAD_DIM, (h + 1) * HEAD_DIM)
            o = _sink_attention(_stack_heads(q, h), k_all[:, cols], v_all[:, cols],
                                bias_ref[h], sink_ref[h])
            width = Q_PER_KV * HEAD_DIM
            attn_ref[rows, h * width:(h + 1) * width] = _unstack_heads(o, n_new).astype(BF16)
        for g in range(GMLP_GROUPS):
            cols = slice(g * GMLP_GROUP_DIM, (g + 1) * GMLP_GROUP_DIM)
            w = ws_ref[g, :n_new, :n_new].astype(BF16)
            sp = _dot(w, vn_ref[rows, cols].astype(BF16)) + bs_ref[:n_new, g:g + 1]
            gm_ref[rows, cols] = (u_ref[rows, cols].astype(F32) * sp).astype(BF16)

    weights = []
    for w_ref, w16_ref in ((wba_ref, wba16_ref), (wbg_ref, wbg16_ref), (wo_ref, wo16_ref)):
        w16_ref[...] = w_ref[...].astype(BF16)
        weights.append(w16_ref[...])
    out_ref[...] = _merge_out(x1_ref[...], attn_ref[...], gm_ref[...],
                              ga_ref[...].astype(F32), gb_ref[...].astype(F32), *weights, g_ref[...])


def _sample_mixer(q, kv, cache_k, cache_v, u, vn, ga, gb, x1, bias, sink, w_s, b_s_t,
                  wba, wbg, wo, g_post, *, n_batch, n_new):
    assert n_new <= CHUNK
    t = q.shape[0]
    vmem = pl.BlockSpec(memory_space=pltpu.VMEM)
    return pl.pallas_call(
        functools.partial(_sample_mixer_body, n_batch, n_new),
        in_specs=[vmem] * 17,
        out_specs=[vmem] * 4,
        out_shape=[jax.ShapeDtypeStruct((t, D_MODEL), F32)] + [jax.ShapeDtypeStruct(w.shape, BF16)
                                                              for w in (wba, wbg, wo)],
        scratch_shapes=[pltpu.VMEM((t, ATTN_WIDTH), BF16), pltpu.VMEM((t, GMLP_WIDTH), BF16)],
        compiler_params=pltpu.CompilerParams(vmem_limit_bytes=VMEM_LIMIT_BYTES),
        name="sample_mixer",
    )(q, kv, cache_k, cache_v, u, vn, ga, gb, x1, bias, sink, w_s, b_s_t, wba, wbg, wo, g_post)


def _prompt_sink_rows(sinks):
    s = sinks.astype(F32).reshape(N_KV_HEADS, PAIR_GROUPS, GROUP_PAIRS, HEADS_PER_TILE)
    s = jnp.transpose(s, (0, 1, 3, 2))[..., None]
    s = jnp.broadcast_to(s, (N_KV_HEADS, PAIR_GROUPS, HEADS_PER_TILE, GROUP_PAIRS, QBLK))
    return s.reshape(N_KV_HEADS * PAIR_GROUPS * HEADS_PER_TILE, GROUP_PAIRS * QBLK)


def _sink_rows(sinks, n_q):
    s = jnp.broadcast_to(sinks.astype(F32).reshape(N_KV_HEADS, Q_PER_KV, 1), (N_KV_HEADS, Q_PER_KV, n_q))
    return s.reshape(N_KV_HEADS, Q_PER_KV * n_q, 1)


PROMPT_TM = 512
PROMPT_TS = 512


def kernel(x_prompt, x_sample, cache_win_k, cache_win_v, rel_bias_table, norm_gains, ffn1_w_gate, ffn1_w_up, ffn1_w_down, w_in, attn_sinks, gmlp_ln_g, gmlp_ln_b, gmlp_w_s, gmlp_b_s, w_branch_attn, w_branch_gmlp, w_out, ffn2_w_gate, ffn2_w_up, ffn2_w_down):
    depth = norm_gains.shape[0]
    batch, seq, _ = x_prompt.shape
    dec_batch, dec_seq, _ = x_sample.shape
    n_cache = cache_win_k.shape[2]
    assert seq % PROMPT_TS == 0 and dec_seq <= CHUNK

    bias_p = _prompt_bias(rel_bias_table)
    bias_s = _relative_bias(rel_bias_table, dec_seq, n_cache + dec_seq, n_cache)

    xp = x_prompt.reshape(batch * seq, D_MODEL)
    xs = x_sample.reshape(dec_batch * dec_seq, D_MODEL)
    t_s = xs.shape[0]
    kp, vp, ks, vs, gs = [], [], [], [], []
    for l in range(depth):
        g = norm_gains[l].astype(F32)
        ln = jnp.stack([gmlp_ln_g[l], gmlp_ln_b[l]]).astype(F32)
        w_s = gmlp_w_s[l].astype(F32)
        b_s_t = jnp.transpose(gmlp_b_s[l]).astype(F32)
        g_ffn1, g_post2, g_ffn2 = g[0:3], g[3:4], g[4:6]
        sink_p = _prompt_sink_rows(attn_sinks[l])
        sink_s = _sink_rows(attn_sinks[l], dec_seq)

        x1, h2, *w1 = _ffn_dma(xs, g_ffn1, ffn1_w_gate[l], ffn1_w_up[l], ffn1_w_down[l],
                                  emit_next=True, name="ffn1_sample")
        q, kv, u, vn, ga, gb, win = _in_proj(h2, w_in[l], ln, tm=t_s, vn_dtype=F32, q_blocked=False,
                                             name="in_proj_sample")
        ck = cache_win_k[l].reshape(dec_batch, n_cache, KV_WIDTH)
        cv = cache_win_v[l].reshape(dec_batch, n_cache, KV_WIDTH)
        x2, wba, wbg, wo = _sample_mixer(q, kv, ck, cv, u, vn, ga, gb, x1, bias_s, sink_s, w_s, b_s_t,
                                         w_branch_attn[l], w_branch_gmlp[l], w_out[l], g_post2,
                                         n_batch=dec_batch, n_new=dec_seq)
        xs, *w2 = _ffn_dma(x2, g_ffn2, ffn2_w_gate[l], ffn2_w_up[l], ffn2_w_down[l],
                              emit_next=False, name="ffn2_sample")
        ks.append(kv[:, :KV_WIDTH].reshape(dec_batch, dec_seq, N_KV_HEADS, HEAD_DIM))
        vs.append(kv[:, KV_WIDTH:].reshape(dec_batch, dec_seq, N_KV_HEADS, HEAD_DIM))
        gs.append(vn.reshape(dec_batch, dec_seq, GMLP_WIDTH))

        x1, h2 = _ffn(xp, g_ffn1, *w1, tm=PROMPT_TM, emit_next=True, name="ffn1_prompt")
        q, kv, packed = _in_proj(h2, win, ln, tm=2 * PROMPT_TM, vn_dtype=BF16, q_blocked=True,
                                 name="in_proj_prompt")
        b3 = lambda a: a.reshape(batch, seq, a.shape[-1])
        x2 = _prompt_mixer(q, b3(kv), b3(packed), b3(x1), bias_p, sink_p,
                           w_s, b_s_t, wba, wbg, wo, g_post2, ts=PROMPT_TS)
        (xp,) = _ffn(x2.reshape(batch * seq, D_MODEL), g_ffn2, *w2, tm=PROMPT_TM, emit_next=False,
                     name="ffn2_prompt")
        kv_win = b3(kv)[:, seq - WINDOW:, :]
        kp.append(kv_win[..., :KV_WIDTH].reshape(batch, WINDOW, N_KV_HEADS, HEAD_DIM))
        vp.append(kv_win[..., KV_WIDTH:].reshape(batch, WINDOW, N_KV_HEADS, HEAD_DIM))

    return (xp.reshape(batch, seq, D_MODEL), xs.reshape(dec_batch, dec_seq, D_MODEL),
            jnp.stack(kp), jnp.stack(vp), jnp.stack(ks), jnp.stack(vs), jnp.stack(gs))
```
